```python
import math
import jax
import jax.numpy as jnp
from jax import lax
import numpy as np

D_MODEL = 1024
BATCH = 4
SEQ = 4096
DEPTH = 1
DEC_BATCH = 32
DEC_SEQ = 1
PAST_LEN = 16384
PAGE_SIZE = 128

DA_HEADS = 4
DA_VD = D_MODEL // 2 // DA_HEADS
DA_HD = DA_VD // 2
DA_WIDTH = DA_HEADS * DA_VD
RET_HEADS = 4
RET_DV = D_MODEL // 2 // RET_HEADS
RET_DK = RET_DV // 2
RET_WIDTH = RET_HEADS * RET_DV
MIX_WIDTH = DA_WIDTH + RET_WIDTH
RET_CHUNK = 128
Q_BLOCK = 128
ROPE_BASE = 10000.0
N_EXPERTS = 32
TOP_K = 4
D_FF = D_MODEL
SWIGLU_LIMIT = 7.0
SWIGLU_ALPHA = 1.702
MOE_BLOCK = 128
NORM_EPS = 1e-5
NEG_INF = -1e30

kernel_name = "hymba_diffattn_retnet_moe_step"


def rms_norm(x, g):
    xf = x.astype(jnp.float32)
    y = xf * lax.rsqrt(jnp.mean(xf * xf, axis=-1, keepdims=True) + NORM_EPS)
    if g is not None:
        y = y * g.astype(jnp.float32)
    return y.astype(x.dtype)


def alibi_slopes():
    return 2.0 ** (-8.0 * jnp.arange(1, DA_HEADS + 1, dtype=jnp.float32) / DA_HEADS)


def retention_log_decay():
    return jnp.log(1.0 - 2.0 ** (-5.0 - jnp.arange(RET_HEADS, dtype=jnp.float32)))


def rotary(x, pos):
    half = x.shape[-1] // 2
    inv = ROPE_BASE ** (-jnp.arange(half, dtype=jnp.float32) / half)
    ang = pos.astype(jnp.float32)[:, None] * inv[None, :]
    cos = jnp.cos(ang)[None, :, None, :]
    sin = jnp.sin(ang)[None, :, None, :]
    xf = x.astype(jnp.float32)
    x1, x2 = xf[..., :half], xf[..., half:]
    return jnp.concatenate([x1 * cos - x2 * sin, x1 * sin + x2 * cos], axis=-1)


def split_projection(p):
    b, t = p.shape[:2]
    sizes = (DA_HEADS * 2 * DA_HD, DA_HEADS * 2 * DA_HD, DA_WIDTH,
             RET_HEADS * RET_DK, RET_HEADS * RET_DK, RET_WIDTH, RET_WIDTH)
    offs = np.cumsum(sizes)[:-1].tolist()
    qa, ka, va, qr, kr, vr, gr = jnp.split(p, offs, axis=-1)
    return (qa.reshape(b, t, DA_HEADS, 2, DA_HD), ka.reshape(b, t, DA_HEADS, 2, DA_HD),
            va.reshape(b, t, DA_HEADS, DA_VD), qr.reshape(b, t, RET_HEADS, RET_DK),
            kr.reshape(b, t, RET_HEADS, RET_DK), vr.reshape(b, t, RET_HEADS, RET_DV), gr)


def diff_lambda(lq1, lk1, lq2, lk2, lam_init):
    f = jnp.float32
    return (jnp.exp(jnp.sum(lq1.astype(f) * lk1.astype(f)))
            - jnp.exp(jnp.sum(lq2.astype(f) * lk2.astype(f))) + lam_init)


def diff_attn_prompt(q, k, v, lam):
    b, s = q.shape[:2]
    nq = s // Q_BLOCK
    scale = DA_HD ** -0.5
    slopes = alibi_slopes()[None, :, None, None, None]
    kpos = jnp.arange(s)
    qb = jnp.moveaxis(q.reshape(b, nq, Q_BLOCK, DA_HEADS, 2, DA_HD), 1, 0)

    def block(args):
        qblk, i = args
        qpos = i * Q_BLOCK + jnp.arange(Q_BLOCK)
        dist = (qpos[:, None] - kpos[None, :]).astype(jnp.float32)
        sc = jnp.einsum('bqhcd,bkhcd->bhcqk', qblk, k,
                        preferred_element_type=jnp.float32) * scale
        sc = jnp.where(dist >= 0, sc - slopes * dist, NEG_INF)
        p = jax.nn.softmax(sc, axis=-1)
        w = (p[:, :, 0] - lam * p[:, :, 1]).astype(v.dtype)
        return jnp.einsum('bhqk,bkhd->bqhd', w, v, preferred_element_type=jnp.float32)

    o = lax.map(block, (qb, jnp.arange(nq)))
    return jnp.moveaxis(o, 0, 1).reshape(b, s, DA_HEADS, DA_VD)


def diff_attn_sample(q, k_new, v_new, k_past, v_past, lam):
    t = q.shape[1]
    past_len = k_past.shape[1]
    scale = DA_HD ** -0.5
    slopes = alibi_slopes()[None, :, None, None, None]
    qpos = past_len + jnp.arange(t)
    d_past = (qpos[:, None] - jnp.arange(past_len)[None, :]).astype(jnp.float32)
    s_past = jnp.einsum('bqhcd,bkhcd->bhcqk', q, k_past,
                        preferred_element_type=jnp.float32) * scale - slopes * d_past
    d_new = (jnp.arange(t)[:, None] - jnp.arange(t)[None, :]).astype(jnp.float32)
    s_new = jnp.einsum('bqhcd,bkhcd->bhcqk', q, k_new,
                       preferred_element_type=jnp.float32) * scale
    s_new = jnp.where(d_new >= 0, s_new - slopes * d_new, NEG_INF)
    p = jax.nn.softmax(jnp.concatenate([s_past, s_new], axis=-1), axis=-1)
    w = p[:, :, 0] - lam * p[:, :, 1]
    o = jnp.einsum('bhqk,bkhd->bqhd', w[..., :past_len].astype(v_past.dtype), v_past,
                   preferred_element_type=jnp.float32)
    o = o + jnp.einsum('bhqk,bkhd->bqhd', w[..., past_len:].astype(v_new.dtype), v_new,
                       preferred_element_type=jnp.float32)
    return o


def retention_chunk(S, q, k, v, log_g):
    L = q.shape[1]
    v = v.astype(jnp.float32)
    idx = jnp.arange(L, dtype=jnp.float32)
    diff = idx[:, None] - idx[None, :]
    decay = jnp.where(diff >= 0,
                      jnp.exp(jnp.maximum(diff, 0.0)[None] * log_g[:, None, None]), 0.0)
    qk = jnp.einsum('blhd,bmhd->bhlm', q, k) * decay[None]
    o = jnp.einsum('bhlm,bmhe->blhe', qk, v)
    q_dec = q * jnp.exp((idx + 1.0)[:, None] * log_g[None, :])[None, :, :, None]
    o = o + jnp.einsum('blhd,bhde->blhe', q_dec, S)
    k_dec = k * jnp.exp((L - 1.0 - idx)[:, None] * log_g[None, :])[None, :, :, None]
    S_new = jnp.exp(L * log_g)[None, :, None, None] * S + jnp.einsum('blhd,blhe->bhde', k_dec, v)
    return S_new, o


def retention_prompt(q, k, v, log_g):
    b, s = q.shape[:2]
    nc = s // RET_CHUNK

    def to_chunks(a):
        return jnp.moveaxis(a.reshape(b, nc, RET_CHUNK, *a.shape[2:]), 1, 0)

    S0 = jnp.zeros((b, RET_HEADS, RET_DK, RET_DV), jnp.float32)

    def step(S, qkv):
        qc, kc, vc = qkv
        return retention_chunk(S, qc, kc, vc, log_g)

    S, o = lax.scan(step, S0, (to_chunks(q), to_chunks(k), to_chunks(v)))
    return S, jnp.moveaxis(o, 0, 1).reshape(b, s, RET_HEADS, RET_DV)


def mix_out(oa, o_r, gr, sub_g, lam_init, w_out_l):
    b, t = oa.shape[:2]
    a = rms_norm(oa, sub_g) * (1.0 - lam_init)
    r = rms_norm(o_r, None).reshape(b, t, RET_WIDTH) * jax.nn.silu(gr.astype(jnp.float32))
    cat = jnp.concatenate([a.reshape(b, t, DA_WIDTH), r], axis=-1).astype(w_out_l.dtype)
    return cat @ w_out_l


def clamped_swiglu(u):
    g = jnp.minimum(u[..., ::2], SWIGLU_LIMIT)
    lin = jnp.clip(u[..., 1::2], -SWIGLU_LIMIT, SWIGLU_LIMIT)
    return g * jax.nn.sigmoid(SWIGLU_ALPHA * g) * (lin + 1.0)


def moe_ffn(h, w_router, b_router, w_up, b_up, w_down, b_down):
    n, d = h.shape
    logits = (h @ w_router).astype(jnp.float32) + b_router.astype(jnp.float32)
    top_v, top_i = lax.top_k(logits, TOP_K)
    gates = jax.nn.softmax(top_v, axis=-1)
    n_assign = n * TOP_K
    n_blocks = -(-n_assign // MOE_BLOCK) + N_EXPERTS
    flat_e = top_i.reshape(-1)
    flat_tok = (jnp.arange(n_assign) // TOP_K).astype(jnp.int32)
    counts = jnp.bincount(flat_e, length=N_EXPERTS)
    start = jnp.cumsum(counts) - counts
    padded = (counts + MOE_BLOCK - 1) // MOE_BLOCK * MOE_BLOCK
    pad_end = jnp.cumsum(padded)
    pad_start = pad_end - padded
    order = jnp.argsort(flat_e)
    sorted_e = flat_e[order]
    dest_sorted = (pad_start[sorted_e] + jnp.arange(n_assign) - start[sorted_e]).astype(jnp.int32)
    dest = jnp.zeros((n_assign,), jnp.int32).at[order].set(dest_sorted)
    slot_tok = jnp.full((n_blocks * MOE_BLOCK,), n, jnp.int32).at[dest].set(flat_tok)
    block_e = jnp.minimum(jnp.searchsorted(pad_end, jnp.arange(n_blocks) * MOE_BLOCK, side='right'),
                          N_EXPERTS - 1)
    h_pad = jnp.concatenate([h, jnp.zeros((1, d), h.dtype)], axis=0)

    def run_block(args):
        tok, e = args
        xb = h_pad[tok]
        u = (xb @ w_up[e]).astype(jnp.float32) + b_up[e].astype(jnp.float32)
        y = clamped_swiglu(u).astype(w_down.dtype)
        return (y @ w_down[e]).astype(jnp.float32) + b_down[e].astype(jnp.float32)

    out = lax.map(run_block, (slot_tok.reshape(n_blocks, MOE_BLOCK), block_e)).reshape(-1, d)
    y = (out[dest] * gates.reshape(-1, 1)).reshape(n, TOP_K, d).sum(axis=1)
    return y.astype(h.dtype)


def trunk(x, start, past, norm_attn_g, w_in, lam_q1, lam_k1, lam_q2, lam_k2, da_subln_g,
          w_out, norm_ffn_g, w_router, b_router, w_up, b_up, w_down, b_down, norm_final_g):
    b, t, d = x.shape
    pos = start + jnp.arange(t)
    log_g = retention_log_decay()
    new_k, new_v, new_s = [], [], []
    for l in range(DEPTH):
        h = rms_norm(x, norm_attn_g[l])
        qa, ka, va, qr, kr, vr, gr = split_projection(h @ w_in[l])
        lam_init = 0.8 - 0.6 * math.exp(-0.3 * l)
        lam = diff_lambda(lam_q1[l], lam_k1[l], lam_q2[l], lam_k2[l], lam_init)
        qr = rotary(qr, pos)
        kr = rotary(kr, pos) * (RET_DK ** -0.5)
        if past is None:
            oa = diff_attn_prompt(qa, ka, va, lam)
            s_new, o_r = retention_prompt(qr, kr, vr, log_g)
        else:
            cache_k, cache_v, state_ret, page_table = past
            k_past = cache_k[l, page_table].reshape(b, -1, DA_HEADS, 2, DA_HD)
            v_past = cache_v[l, page_table].reshape(b, -1, DA_HEADS, DA_VD)
            oa = diff_attn_sample(qa, ka, va, k_past, v_past, lam)
            s_new, o_r = retention_chunk(state_ret[l].astype(jnp.float32), qr, kr, vr, log_g)
        x = x + mix_out(oa, o_r, gr, da_subln_g[l], lam_init, w_out[l]).astype(x.dtype)
        hf = rms_norm(x, norm_ffn_g[l])
        x = x + moe_ffn(hf.reshape(b * t, d), w_router[l], b_router[l], w_up[l], b_up[l],
                        w_down[l], b_down[l]).reshape(b, t, d).astype(x.dtype)
        new_k.append(ka)
        new_v.append(va)
        new_s.append(s_new.astype(x.dtype))
    return (rms_norm(x, norm_final_g), jnp.stack(new_k), jnp.stack(new_v), jnp.stack(new_s))


def setup_inputs(seed: int = 0) -> dict:
    key = jax.random.key(seed)
    ks = jax.random.split(key, 24)
    f32 = jnp.float32
    n_pages = PAST_LEN // PAGE_SIZE
    n_pool = (DEC_BATCH * n_pages * 5 + 3) // 4
    in_cols = 3 * DA_HEADS * 2 * DA_HD - DA_HEADS * 2 * DA_HD + DA_WIDTH + 2 * RET_HEADS * RET_DK + 2 * RET_WIDTH

    def nrm(k, shape, scale):
        return jax.random.normal(k, shape, f32) * scale

    ret_scale = jnp.sqrt(2.0 ** (5.0 + jnp.arange(RET_HEADS, dtype=f32)) / RET_DK)
    return {
        "x_prompt": nrm(ks[0], (BATCH, SEQ, D_MODEL), 1.0),
        "x_sample": nrm(ks[1], (DEC_BATCH, DEC_SEQ, D_MODEL), 1.0),
        "cache_k": nrm(ks[2], (DEPTH, n_pool, PAGE_SIZE, DA_HEADS, 2, DA_HD), 1.0),
        "cache_v": nrm(ks[3], (DEPTH, n_pool, PAGE_SIZE, DA_HEADS, DA_VD), 1.0),
        "state_ret": nrm(ks[4], (DEPTH, DEC_BATCH, RET_HEADS, RET_DK, RET_DV), 1.0)
                     * ret_scale[None, None, :, None, None],
        "page_table": jax.random.permutation(ks[5], n_pool)[: DEC_BATCH * n_pages]
                      .reshape(DEC_BATCH, n_pages).astype(jnp.int32),
        "norm_attn_g": 1.0 + nrm(ks[6], (DEPTH, D_MODEL), 0.01),
        "w_in": nrm(ks[7], (DEPTH, D_MODEL, in_cols), D_MODEL ** -0.5),
        "lam_q1": nrm(ks[8], (DEPTH, DA_HD), 0.1),
        "lam_k1": nrm(ks[9], (DEPTH, DA_HD), 0.1),
        "lam_q2": nrm(ks[10], (DEPTH, DA_HD), 0.1),
        "lam_k2": nrm(ks[11], (DEPTH, DA_HD), 0.1),
        "da_subln_g": 1.0 + nrm(ks[12], (DEPTH, DA_VD), 0.01),
        "w_out": nrm(ks[13], (DEPTH, MIX_WIDTH, D_MODEL), MIX_WIDTH ** -0.5),
        "norm_ffn_g": 1.0 + nrm(ks[14], (DEPTH, D_MODEL), 0.01),
        "w_router": nrm(ks[15], (DEPTH, D_MODEL, N_EXPERTS), D_MODEL ** -0.5),
        "b_router": nrm(ks[16], (DEPTH, N_EXPERTS), 0.01),
        "w_up": nrm(ks[17], (DEPTH, N_EXPERTS, D_MODEL, 2 * D_FF), D_MODEL ** -0.5),
        "b_up": nrm(ks[18], (DEPTH, N_EXPERTS, 2 * D_FF), 0.01),
        "w_down": nrm(ks[19], (DEPTH, N_EXPERTS, D_FF, D_MODEL), D_FF ** -0.5),
        "b_down": nrm(ks[20], (DEPTH, N_EXPERTS, D_MODEL), 0.01),
        "norm_final_g": 1.0 + nrm(ks[21], (D_MODEL,), 0.01),
    }


def reference(x_prompt, x_sample, cache_k, cache_v, state_ret, page_table,
              norm_attn_g, w_in, lam_q1, lam_k1, lam_q2, lam_k2, da_subln_g, w_out,
              norm_ffn_g, w_router, b_router, w_up, b_up, w_down, b_down, norm_final_g):
    weights = (norm_attn_g, w_in, lam_q1, lam_k1, lam_q2, lam_k2, da_subln_g, w_out,
               norm_ffn_g, w_router, b_router, w_up, b_up, w_down, b_down, norm_final_g)
    y_prompt, k_prompt, v_prompt, ret_prompt = trunk(x_prompt, 0, None, *weights)
    past_len = page_table.shape[1] * cache_k.shape[2]
    y_sample, k_sample, v_sample, ret_sample = trunk(
        x_sample, past_len, (cache_k, cache_v, state_ret, page_table), *weights)
    return (y_prompt, y_sample, k_prompt, v_prompt, ret_prompt, k_sample, v_sample, ret_sample)
```

```python
import functools
import math

import numpy as np
import jax
import jax.numpy as jnp
from jax import lax
from jax.experimental import pallas as pl
from jax.experimental.pallas import tpu as pltpu
from jax.experimental.pallas import tpu_sc as plsc

F32 = jnp.float32
BF16 = jnp.bfloat16
I32 = jnp.int32

D_MODEL = 1024
DA_HEADS = 4
DA_VD = 128
DA_HD = 64
DA_WIDTH = DA_HEADS * DA_VD
RET_HEADS = 4
RET_DV = 128
RET_DK = 64
RET_WIDTH = RET_HEADS * RET_DV
RET_QK_WIDTH = RET_HEADS * RET_DK
RET_CHUNK = 128
ROPE_BASE = 10000.0
N_EXPERTS = 32
TOP_K = 4
D_FF = 1024
SWIGLU_LIMIT = 7.0
SWIGLU_ALPHA = 1.702
NORM_EPS = 1e-5
NEG_INF = -1e30
LAM_INIT = 0.8 - 0.6 * math.exp(-0.3 * 0)
LANES = 128
HALF = D_MODEL // 2
MOE_TM = 256
SC_WINDOW = 128
SC_COLS = 256
SC_SUBCORES = 32
VMEM_LIMIT = 56 * 1024 * 1024

_C_QA, _C_KA, _C_VA = 0, 512, 1024
_C_QR, _C_QRS, _C_KR, _C_KRS = 1536, 1792, 2048, 2304
_C_VR, _C_GR, _C_END = 2560, 3072, 3584


def _mm(a, b):
    return jnp.dot(a.astype(BF16), b.astype(BF16), preferred_element_type=F32)


def _bf(x):
    return x.astype(BF16).astype(F32)


def _rms(x):
    return x * lax.rsqrt(jnp.mean(x * x, axis=-1, keepdims=True) + NORM_EPS)


def _silu(g):
    return g / (1.0 + jnp.exp(-g))


def _pack_pairs(x):
    w = x.shape[-1] // 2
    lo = lax.bitcast_convert_type(x[:, :w].astype(BF16).astype(F32), jnp.uint32)
    hi = lax.bitcast_convert_type(x[:, w:].astype(BF16).astype(F32), jnp.uint32)
    return lax.bitcast_convert_type(hi | (lo >> 16), I32)


def _unpack_pairs(p):
    u = lax.bitcast_convert_type(p, jnp.uint32)
    lo = lax.bitcast_convert_type(u << 16, F32)
    hi = lax.bitcast_convert_type(u & jnp.uint32(0xFFFF0000), F32)
    return lo, hi


def _alibi_slopes():
    return np.asarray([2.0 ** (-8.0 * (h + 1) / DA_HEADS) for h in range(DA_HEADS)], np.float32)


def _inproj_kernel(x_ref, g_ref, w_ref, cos_ref, sin_ref,
                   qa_ref, ka_ref, va_ref, kab_ref, vab_ref,
                   qr_ref, kr_ref, vr_ref, gr_ref):
    hm = (_rms(x_ref[...]) * g_ref[...]).astype(BF16)

    def mm(lo, hi):
        return _mm(hm, w_ref[:, lo:hi])

    qa_ref[...] = mm(_C_QA, _C_KA).astype(qa_ref.dtype)
    ka = mm(_C_KA, _C_VA)
    ka_ref[...] = ka
    kab_ref[...] = ka.astype(BF16)
    va = mm(_C_VA, _C_QR)
    va_ref[...] = va
    vab_ref[...] = va.astype(BF16)
    cos = cos_ref[...]
    sin = sin_ref[...]
    qr_ref[...] = (mm(_C_QR, _C_QRS) * cos + mm(_C_QRS, _C_KR) * sin).astype(qr_ref.dtype)
    kr_ref[...] = (mm(_C_KR, _C_KRS) * cos + mm(_C_KRS, _C_VR) * sin).astype(kr_ref.dtype)
    vr_ref[...] = mm(_C_VR, _C_GR).astype(vr_ref.dtype)
    gr_ref[...] = mm(_C_GR, _C_END).astype(gr_ref.dtype)


def _inproj(x, g, w_ext, cos_t, sin_t, *, tm, act):
    n = x.shape[0]
    n_tab = cos_t.shape[0] // tm
    row = lambda w: pl.BlockSpec((tm, w), lambda i: (i, 0))
    tab = pl.BlockSpec((tm, RET_QK_WIDTH), lambda i: (i % n_tab, 0))
    out_shape = (
        jax.ShapeDtypeStruct((n, DA_WIDTH), act),
        jax.ShapeDtypeStruct((n, DA_WIDTH), F32),
        jax.ShapeDtypeStruct((n, DA_WIDTH), F32),
        jax.ShapeDtypeStruct((n, DA_WIDTH), BF16),
        jax.ShapeDtypeStruct((n, DA_WIDTH), BF16),
        jax.ShapeDtypeStruct((n, RET_QK_WIDTH), act),
        jax.ShapeDtypeStruct((n, RET_QK_WIDTH), act),
        jax.ShapeDtypeStruct((n, RET_WIDTH), act),
        jax.ShapeDtypeStruct((n, RET_WIDTH), act),
    )
    return pl.pallas_call(
        _inproj_kernel,
        grid=(n // tm,),
        in_specs=[row(D_MODEL),
                  pl.BlockSpec((1, D_MODEL), lambda i: (0, 0)),
                  pl.BlockSpec((D_MODEL, _C_END), lambda i: (0, 0)),
                  tab, tab],
        out_specs=(row(DA_WIDTH), row(DA_WIDTH), row(DA_WIDTH), row(DA_WIDTH), row(DA_WIDTH),
                   row(RET_QK_WIDTH), row(RET_QK_WIDTH), row(RET_WIDTH), row(RET_WIDTH)),
        out_shape=out_shape,
        compiler_params=pltpu.CompilerParams(dimension_semantics=("parallel",),
                                             vmem_limit_bytes=VMEM_LIMIT),
        name="inproj",
    )(x, g, w_ext, cos_t, sin_t)


def _lambda_value(lamv_ref):
    lv = lamv_ref[...]
    s1 = jnp.sum(lv[0:1] * lv[1:2], axis=-1, keepdims=True)
    s2 = jnp.sum(lv[2:3] * lv[3:4], axis=-1, keepdims=True)
    return jnp.exp(s1) - jnp.exp(s2) + LAM_INIT


def _attn_prompt_kernel(qi_tab, ki_tab, q_ref, k_ref, v_ref, rel_ref, relm_ref, off_ref,
                        lamv_ref, subg_ref, o_ref, m_sc, l_sc, acc_sc):
    hd = pl.program_id(1)
    t = pl.program_id(2)
    qi = qi_tab[t]
    ki = ki_tab[t]

    @pl.when(ki == 0)
    def _():
        m_sc[...] = jnp.full(m_sc.shape, NEG_INF, F32)
        l_sc[...] = jnp.zeros(l_sc.shape, F32)
        acc_sc[...] = jnp.zeros(acc_sc.shape, F32)

    off = off_ref[hd, t]

    def update(rel):
        q = q_ref[...]
        k = k_ref[...]
        v = v_ref[...]
        for c in range(2):
            s = lax.dot_general(q[:, c * DA_HD:(c + 1) * DA_HD], k[:, c * DA_HD:(c + 1) * DA_HD],
                                (((1,), (1,)), ((), ())), preferred_element_type=F32)
            tt = s + rel
            m_old = m_sc[c]
            m_new = jnp.maximum(m_old, jnp.max(tt, axis=-1, keepdims=True) + off)
            alpha = jnp.exp(m_old - m_new)
            p = jnp.exp(tt + (off - m_new))
            l_sc[c] = alpha * l_sc[c] + jnp.sum(p, axis=-1, keepdims=True)
            acc_sc[c] = alpha * acc_sc[c] + jnp.dot(p.astype(BF16), v, preferred_element_type=F32)
            m_sc[c] = m_new

    @pl.when(ki < qi)
    def _():
        update(rel_ref[0])

    @pl.when(ki == qi)
    def _():
        update(relm_ref[0])
        lam = _lambda_value(lamv_ref)
        o = acc_sc[0] / l_sc[0] - lam * (acc_sc[1] / l_sc[1])
        a = _rms(o) * subg_ref[...] * (1.0 - LAM_INIT)
        o_ref[...] = a.astype(o_ref.dtype)


def _attn_prompt(qa, kab, vab, lamv, subg, *, batch, seq, tq):
    nq = seq // tq
    steps = [(i, j) for i in range(nq) for j in range(i + 1)]
    qi_tab = jnp.asarray([s[0] for s in steps], I32)
    ki_tab = jnp.asarray([s[1] for s in steps], I32)
    slopes = _alibi_slopes()
    dist = (np.arange(tq)[:, None] - np.arange(tq)[None, :]).astype(np.float32)
    rel = -slopes[:, None, None] * dist[None]
    relm = np.where(dist[None] >= 0, rel, np.float32(NEG_INF)).astype(np.float32)
    blk = np.asarray([(s[0] - s[1]) * tq for s in steps], np.float32)
    off = -slopes[:, None] * blk[None, :]
    n = batch * seq
    qspec = pl.BlockSpec((tq, DA_VD), lambda b, h, t, qt, kt: (b * nq + qt[t], h))
    kspec = pl.BlockSpec((tq, DA_VD), lambda b, h, t, qt, kt: (b * nq + kt[t], h))
    relspec = pl.BlockSpec((1, tq, tq), lambda b, h, t, qt, kt: (h, 0, 0))
    grid_spec = pltpu.PrefetchScalarGridSpec(
        num_scalar_prefetch=2,
        grid=(batch, DA_HEADS, len(steps)),
        in_specs=[qspec, kspec, kspec, relspec, relspec,
                  pl.BlockSpec(memory_space=pltpu.SMEM),
                  pl.BlockSpec((4, DA_HD), lambda b, h, t, qt, kt: (0, 0)),
                  pl.BlockSpec((1, DA_VD), lambda b, h, t, qt, kt: (0, 0))],
        out_specs=qspec,
        scratch_shapes=[pltpu.VMEM((2, tq, 1), F32), pltpu.VMEM((2, tq, 1), F32),
                        pltpu.VMEM((2, tq, DA_VD), F32)],
    )
    return pl.pallas_call(
        _attn_prompt_kernel,
        grid_spec=grid_spec,
        out_shape=jax.ShapeDtypeStruct((n, DA_WIDTH), BF16),
        compiler_params=pltpu.CompilerParams(
            dimension_semantics=("parallel", "parallel", "arbitrary"),
            vmem_limit_bytes=VMEM_LIMIT),
        name="attn_prompt",
    )(qi_tab, ki_tab, qa, kab, vab, jnp.asarray(rel), jnp.asarray(relm), jnp.asarray(off),
      lamv, subg)


def _ret_prompt_kernel(q_ref, k_ref, v_ref, g_ref, dec_ref, qdec_ref, kdec_ref, sdec_ref,
                       r_ref, s_ref, *, batch):
    c = pl.program_id(0)

    @pl.when(c == 0)
    def _():
        s_ref[...] = jnp.zeros(s_ref.shape, F32)

    qdec = qdec_ref[...]
    kdec = kdec_ref[...]
    for b in range(batch):
        q = q_ref[b].astype(F32)
        k = k_ref[b].astype(F32)
        qd = (q * qdec).astype(BF16)
        kd = (k * kdec).astype(BF16)
        qb = q_ref[b]
        kb = k_ref[b]
        for h in range(RET_HEADS):
            ks = slice(h * RET_DK, (h + 1) * RET_DK)
            vs = slice(h * RET_DV, (h + 1) * RET_DV)
            vh = v_ref[b, :, vs]
            qk = lax.dot_general(qb[:, ks], kb[:, ks], (((1,), (1,)), ((), ())),
                                 preferred_element_type=F32) * dec_ref[h]
            s_old = s_ref[b, h]
            o = jnp.dot(qk.astype(BF16), vh, preferred_element_type=F32)
            o = o + jnp.dot(qd[:, ks], s_old.astype(BF16), preferred_element_type=F32)
            s_ref[b, h] = sdec_ref[h] * s_old + lax.dot_general(
                kd[:, ks], vh, (((0,), (0,)), ((), ())), preferred_element_type=F32)
            gate = _silu(g_ref[b, :, vs].astype(F32))
            r_ref[b, :, vs] = (_rms(o) * gate).astype(r_ref.dtype)


def _ret_tables(length):
    log_g = jnp.log(1.0 - 2.0 ** (-5.0 - jnp.arange(RET_HEADS, dtype=F32)))
    idx = jnp.arange(length, dtype=F32)
    diff = idx[:, None] - idx[None, :]
    dec = jnp.where(diff >= 0, jnp.exp(jnp.maximum(diff, 0.0)[None] * log_g[:, None, None]), 0.0)
    qdec = jnp.exp((idx + 1.0)[:, None] * log_g[None, :])
    kdec = jnp.exp((length - 1.0 - idx)[:, None] * log_g[None, :])
    sdec = jnp.exp(length * log_g)
    rep = lambda a: jnp.repeat(a, RET_DK, axis=1)
    sdec_b = jnp.broadcast_to(sdec[:, None, None], (RET_HEADS, RET_DK, RET_DV))
    return dec, rep(qdec), rep(kdec), sdec_b


def _ret_prompt(qr, kr, vr, gr, *, batch, seq):
    nc = seq // RET_CHUNK
    dec, qdec, kdec, sdec = _ret_tables(RET_CHUNK)
    q3 = qr.reshape(batch, seq, RET_QK_WIDTH)
    k3 = kr.reshape(batch, seq, RET_QK_WIDTH)
    v3 = vr.reshape(batch, seq, RET_WIDTH)
    g3 = gr.reshape(batch, seq, RET_WIDTH)
    blk = lambda w: pl.BlockSpec((batch, RET_CHUNK, w), lambda c: (0, c, 0))
    const = lambda shape: pl.BlockSpec(shape, lambda c: (0,) * len(shape))
    r, s = pl.pallas_call(
        functools.partial(_ret_prompt_kernel, batch=batch),
        grid=(nc,),
        in_specs=[blk(RET_QK_WIDTH), blk(RET_QK_WIDTH), blk(RET_WIDTH), blk(RET_WIDTH),
                  const((RET_HEADS, RET_CHUNK, RET_CHUNK)),
                  const((RET_CHUNK, RET_QK_WIDTH)), const((RET_CHUNK, RET_QK_WIDTH)),
                  const((RET_HEADS, RET_DK, RET_DV))],
        out_specs=(blk(RET_WIDTH), const((batch, RET_HEADS, RET_DK, RET_DV))),
        out_shape=(jax.ShapeDtypeStruct((batch, seq, RET_WIDTH), BF16),
                   jax.ShapeDtypeStruct((batch, RET_HEADS, RET_DK, RET_DV), F32)),
        compiler_params=pltpu.CompilerParams(dimension_semantics=("arbitrary",),
                                             vmem_limit_bytes=VMEM_LIMIT),
        name="ret_prompt",
    )(q3, k3, v3, g3, dec, qdec, kdec, sdec)
    return r.reshape(batch * seq, RET_WIDTH), s


def _attn_sample_kernel(pt_ref, q_ref, kn_ref, vn_ref, rel_ref, slope_ref, lamv_ref, subg_ref,
                        *rest, pages_per_step, past_len, n_steps):
    g_pages = pages_per_step
    k_refs = rest[:g_pages]
    v_refs = rest[g_pages:2 * g_pages]
    o_ref = rest[2 * g_pages]
    s_sc, wnew_sc, acc_sc = rest[2 * g_pages + 1:]
    j = pl.program_id(1)
    rows = 2 * DA_HEADS

    def q_rows():
        col_group = lax.broadcasted_iota(I32, (rows, DA_WIDTH), 1) // DA_HD
        row_id = lax.broadcasted_iota(I32, (rows, DA_WIDTH), 0)
        want = 2 * (row_id % DA_HEADS) + row_id // DA_HEADS
        qb = jnp.broadcast_to(_bf(q_ref[0]), (rows, DA_WIDTH))
        return jnp.where(col_group == want, qb, 0.0)

    @pl.when(j < n_steps)
    def _():
        qr = q_rows().astype(BF16)
        for g in range(g_pages):
            page = j * g_pages + g
            s = lax.dot_general(qr, k_refs[g][0].astype(BF16), (((1,), (1,)), ((), ())),
                                preferred_element_type=F32)
            base = (page * LANES - past_len).astype(F32)
            s_sc[page] = s + (rel_ref[...] + slope_ref[...] * base)

    @pl.when(j == n_steps - 1)
    def _():
        s_all = s_sc[...]
        s_new = jnp.sum(q_rows() * _bf(kn_ref[0]), axis=-1, keepdims=True)
        m = jnp.maximum(jnp.max(jnp.max(s_all, axis=0), axis=-1, keepdims=True), s_new)
        p = jnp.exp(s_all - m)
        p_new = jnp.exp(s_new - m)
        l = jnp.sum(jnp.sum(p, axis=0), axis=-1, keepdims=True) + p_new
        wn = p / l
        wn_new = p_new / l
        lam = _lambda_value(lamv_ref)
        s_sc[:, 0:DA_HEADS, :] = wn[:, 0:DA_HEADS, :] - lam * wn[:, DA_HEADS:rows, :]
        s_sc[:, DA_HEADS:rows, :] = jnp.zeros((s_sc.shape[0], DA_HEADS, LANES), F32)
        wnew_sc[...] = wn_new[0:DA_HEADS] - lam * wn_new[DA_HEADS:rows]
        acc_sc[...] = jnp.zeros(acc_sc.shape, F32)

    @pl.when(j >= n_steps)
    def _():
        acc = acc_sc[...]
        for g in range(g_pages):
            page = (j - n_steps) * g_pages + g
            acc = acc + jnp.dot(s_sc[page].astype(BF16), v_refs[g][0].astype(BF16),
                                preferred_element_type=F32)
        acc_sc[...] = acc

    @pl.when(j == 2 * n_steps - 1)
    def _():
        acc = acc_sc[...]
        w_new = _bf(wnew_sc[...])
        v_new = _bf(vn_ref[0])
        outs = []
        for h in range(DA_HEADS):
            vs = slice(h * DA_VD, (h + 1) * DA_VD)
            o = acc[h:h + 1, vs] + w_new[h:h + 1, :] * v_new[:, vs]
            outs.append(_rms(o) * subg_ref[...] * (1.0 - LAM_INIT))
        o_ref[0] = jnp.concatenate(outs, axis=-1)


def _attn_sample(q, k_new, v_new, cache_k, cache_v, page_table, lamv, subg, *, pages_per_step):
    nb, n_pages = page_table.shape
    page = cache_k.shape[1]
    assert page == LANES
    past_len = n_pages * page
    g_pages = pages_per_step
    n_steps = n_pages // g_pages
    rows = 2 * DA_HEADS
    slope_rows = np.tile(_alibi_slopes(), 2)[:, None]
    rel = slope_rows * np.arange(page, dtype=np.float32)[None, :]
    pt = page_table.reshape(-1).astype(I32)
    row3 = lambda: pl.BlockSpec((1, 1, DA_WIDTH), lambda b, j, pt: (b, 0, 0))
    const2 = lambda shape: pl.BlockSpec(shape, lambda b, j, pt: (0, 0))

    def k_spec(g):
        return pl.BlockSpec(
            (1, page, DA_WIDTH),
            lambda b, j, pt: (pt[b * n_pages + jnp.minimum(j, n_steps - 1) * g_pages + g], 0, 0))

    def v_spec(g):
        return pl.BlockSpec(
            (1, page, DA_WIDTH),
            lambda b, j, pt: (pt[b * n_pages + jnp.maximum(j - n_steps, 0) * g_pages + g], 0, 0))

    grid_spec = pltpu.PrefetchScalarGridSpec(
        num_scalar_prefetch=1,
        grid=(nb, 2 * n_steps),
        in_specs=[row3(), row3(), row3(), const2((rows, page)), const2((rows, 1)),
                  const2((4, DA_HD)), const2((1, DA_VD))]
                 + [k_spec(g) for g in range(g_pages)] + [v_spec(g) for g in range(g_pages)],
        out_specs=row3(),
        scratch_shapes=[pltpu.VMEM((n_pages, rows, page), F32), pltpu.VMEM((DA_HEADS, 1), F32),
                        pltpu.VMEM((rows, DA_WIDTH), F32)],
    )
    out = pl.pallas_call(
        functools.partial(_attn_sample_kernel, pages_per_step=g_pages, past_len=past_len,
                          n_steps=n_steps),
        grid_spec=grid_spec,
        out_shape=jax.ShapeDtypeStruct((nb, 1, DA_WIDTH), F32),
        compiler_params=pltpu.CompilerParams(dimension_semantics=("parallel", "arbitrary"),
                                             vmem_limit_bytes=VMEM_LIMIT),
        name="attn_sample",
    )(pt, q.reshape(nb, 1, DA_WIDTH), k_new.reshape(nb, 1, DA_WIDTH), v_new.reshape(nb, 1, DA_WIDTH),
      jnp.asarray(rel), jnp.asarray(slope_rows), lamv, subg,
      *([cache_k] * g_pages), *([cache_v] * g_pages))
    return out.reshape(nb, DA_WIDTH)


def _ret_sample_kernel(q_ref, k_ref, qc_ref, kc_ref, v_ref, g_ref, s_ref, qdec_ref, sdec_ref,
                       r_ref, so_ref):
    q = _bf(q_ref[0])
    k = _bf(k_ref[0])
    qd_col = _bf(qc_ref[0] * qdec_ref[...])
    k_col = _bf(kc_ref[0])
    outs = []
    for h in range(RET_HEADS):
        ks = slice(h * RET_DK, (h + 1) * RET_DK)
        vs = slice(h * RET_DV, (h + 1) * RET_DV)
        vh = _bf(v_ref[0, :, vs])
        s_old = s_ref[0, h]
        qk = jnp.sum(q[:, ks] * k[:, ks], axis=-1, keepdims=True)
        o = _bf(qk) * vh + jnp.sum(qd_col[ks, :] * _bf(s_old), axis=0, keepdims=True)
        so_ref[0, h] = sdec_ref[h] * s_old + k_col[ks, :] * vh
        outs.append(_rms(o) * _silu(g_ref[0, :, vs]))
    r_ref[0] = jnp.concatenate(outs, axis=-1)


def _ret_sample(qr, kr, vr, gr, state):
    nb = qr.shape[0]
    _, qdec, _, sdec = _ret_tables(1)
    row3 = lambda w: pl.BlockSpec((1, 1, w), lambda b: (b, 0, 0))
    col3 = pl.BlockSpec((1, RET_QK_WIDTH, 1), lambda b: (b, 0, 0))
    st = pl.BlockSpec((1, RET_HEADS, RET_DK, RET_DV), lambda b: (b, 0, 0, 0))
    r, s_new = pl.pallas_call(
        _ret_sample_kernel,
        grid=(nb,),
        in_specs=[row3(RET_QK_WIDTH), row3(RET_QK_WIDTH), col3, col3, row3(RET_WIDTH),
                  row3(RET_WIDTH), st,
                  pl.BlockSpec((RET_QK_WIDTH, 1), lambda b: (0, 0)),
                  pl.BlockSpec((RET_HEADS, RET_DK, RET_DV), lambda b: (0, 0, 0))],
        out_specs=(row3(RET_WIDTH), st),
        out_shape=(jax.ShapeDtypeStruct((nb, 1, RET_WIDTH), F32),
                   jax.ShapeDtypeStruct((nb, RET_HEADS, RET_DK, RET_DV), F32)),
        compiler_params=pltpu.CompilerParams(dimension_semantics=("parallel",)),
        name="ret_sample",
    )(qr.reshape(nb, 1, -1), kr.reshape(nb, 1, -1), qr.reshape(nb, -1, 1), kr.reshape(nb, -1, 1),
      vr.reshape(nb, 1, -1), gr.reshape(nb, 1, -1), state, qdec.reshape(-1, 1), sdec)
    return r.reshape(nb, RET_WIDTH), s_new


def _outproj_kernel(a_ref, r_ref, x_ref, wo_ref, g_ref, wr_ref, br_ref,
                    x1_ref, hfp_ref, ti_ref, tg_ref):
    mix = _mm(a_ref[...], wo_ref[:DA_WIDTH, :]) + _mm(r_ref[...], wo_ref[DA_WIDTH:, :])
    x1 = x_ref[...] + mix
    x1_ref[...] = x1
    hf = _rms(x1) * g_ref[...]
    hfp_ref[...] = _pack_pairs(hf)
    logits = _mm(hf, wr_ref[...]) + br_ref[...]
    lane = lax.broadcasted_iota(I32, logits.shape, 1)
    work = logits
    top_v, top_i = [], []
    for _ in range(TOP_K):
        m = jnp.max(work, axis=-1, keepdims=True)
        idx = jnp.min(jnp.where(work == m, lane, LANES), axis=-1, keepdims=True)
        top_v.append(m)
        top_i.append(idx)
        work = jnp.where(lane == idx, -jnp.inf, work)
    es = [jnp.exp(v - top_v[0]) for v in top_v]
    denom = es[0] + es[1] + es[2] + es[3]
    ti = jnp.zeros(logits.shape, I32)
    tg = jnp.zeros(logits.shape, F32)
    for kk in range(TOP_K):
        ti = jnp.where(lane == kk, top_i[kk], ti)
        tg = jnp.where(lane == kk, es[kk] / denom, tg)
    ti_ref[...] = ti
    tg_ref[...] = tg


def _outproj(a, r, x, w_out, g, w_router, b_router, *, tm):
    n = x.shape[0]
    row = lambda w: pl.BlockSpec((tm, w), lambda i: (i, 0))
    const = lambda shape: pl.BlockSpec(shape, lambda i: (0, 0))
    return pl.pallas_call(
        _outproj_kernel,
        grid=(n // tm,),
        in_specs=[row(DA_WIDTH), row(RET_WIDTH), row(D_MODEL), const((D_MODEL, D_MODEL)),
                  const((1, D_MODEL)), const((D_MODEL, LANES)), const((1, LANES))],
        out_specs=(row(D_MODEL), row(HALF), row(LANES), row(LANES)),
        out_shape=(jax.ShapeDtypeStruct((n, D_MODEL), F32),
                   jax.ShapeDtypeStruct((n, HALF), I32),
                   jax.ShapeDtypeStruct((n, LANES), I32),
                   jax.ShapeDtypeStruct((n, LANES), F32)),
        compiler_params=pltpu.CompilerParams(dimension_semantics=("parallel",),
                                             vmem_limit_bytes=VMEM_LIMIT),
        name="outproj",
    )(a, r, x, w_out, g, w_router, b_router)


def _sc_gather(x, idx):
    m = idx.shape[0]
    cols = x.shape[1]
    mesh = plsc.VectorSubcoreMesh(core_axis_name="c", subcore_axis_name="s")

    @pl.kernel(out_type=jax.ShapeDtypeStruct((m, cols), x.dtype), mesh=mesh)
    def gather_kernel(x_hbm, i_hbm, o_hbm):
        def body(i_vmem, o_vmem):
            pltpu.sync_copy(x_hbm.at[i_vmem.at[0]], o_vmem)

        pltpu.emit_pipeline(
            body,
            grid=(m // SC_WINDOW,),
            in_specs=[pl.BlockSpec((1, SC_WINDOW), lambda i: (0, i))],
            out_specs=[pl.BlockSpec((SC_WINDOW, cols), lambda i: (i, 0))],
            core_axis_name=("c", "s"),
            dimension_semantics=(pltpu.PARALLEL,),
        )(i_hbm, o_hbm)

    return gather_kernel(x, idx.reshape(1, m))


def _gather_rows(x, idx):
    split = HALF // SC_COLS
    idx2 = (idx[:, None] * split + jnp.arange(split, dtype=I32)[None, :]).reshape(-1)
    unit = SC_WINDOW * SC_SUBCORES
    padded = -(-idx2.shape[0] // unit) * unit
    idx2 = jnp.pad(idx2, (0, padded - idx2.shape[0]))
    out = _sc_gather(x.reshape(-1, SC_COLS), idx2)
    return out.reshape(padded // split, HALF)


def _moe_kernel(be_ref, na_ref, xs_ref, gate_ref, wup_ref, bup_ref, wdn_ref, bdn_ref, out_ref):
    i = pl.program_id(0)

    @pl.when(i < na_ref[0])
    def _():
        lo, hi = _unpack_pairs(xs_ref[...])
        u = (jnp.dot(lo.astype(BF16), wup_ref[0, :HALF, :], preferred_element_type=F32)
             + jnp.dot(hi.astype(BF16), wup_ref[0, HALF:, :], preferred_element_type=F32)
             + bup_ref[0])
        g = jnp.minimum(u[:, :D_FF], SWIGLU_LIMIT)
        lin = jnp.clip(u[:, D_FF:], -SWIGLU_LIMIT, SWIGLU_LIMIT)
        y = g * (1.0 / (1.0 + jnp.exp(-SWIGLU_ALPHA * g))) * (lin + 1.0)
        o = jnp.dot(y.astype(BF16), wdn_ref[0], preferred_element_type=F32) + bdn_ref[0]
        out_ref[...] = _pack_pairs(o * gate_ref[...])

    @pl.when(i >= na_ref[0])
    def _():
        out_ref[...] = jnp.zeros(out_ref.shape, I32)


def _moe(block_e, n_active, xs, slot_gate, w_up_b, b_up_d, w_down_b, b_down):
    n_blocks = block_e.shape[0]
    n_slots = n_blocks * MOE_TM

    def act(i, na):
        return jnp.minimum(i, na[0] - 1)

    grid_spec = pltpu.PrefetchScalarGridSpec(
        num_scalar_prefetch=2,
        grid=(n_blocks,),
        in_specs=[
            pl.BlockSpec((MOE_TM, HALF), lambda i, be, na: (act(i, na), 0)),
            pl.BlockSpec((MOE_TM, 1), lambda i, be, na: (act(i, na), 0)),
            pl.BlockSpec((1, D_MODEL, 2 * D_FF), lambda i, be, na: (be[act(i, na)], 0, 0)),
            pl.BlockSpec((1, 1, 2 * D_FF), lambda i, be, na: (be[act(i, na)], 0, 0)),
            pl.BlockSpec((1, D_FF, D_MODEL), lambda i, be, na: (be[act(i, na)], 0, 0)),
            pl.BlockSpec((1, 1, D_MODEL), lambda i, be, na: (be[act(i, na)], 0, 0)),
        ],
        out_specs=pl.BlockSpec((MOE_TM, HALF), lambda i, be, na: (i, 0)),
    )
    return pl.pallas_call(
        _moe_kernel,
        grid_spec=grid_spec,
        out_shape=jax.ShapeDtypeStruct((n_slots, HALF), I32),
        compiler_params=pltpu.CompilerParams(dimension_semantics=("arbitrary",),
                                             vmem_limit_bytes=VMEM_LIMIT),
        name="moe_ffn",
    )(block_e, n_active, xs, slot_gate, w_up_b, b_up_d, w_down_b, b_down)


def _route(top_i, top_g, n_tokens):
    n_assign = n_tokens * TOP_K
    n_blocks = -(-n_assign // MOE_TM) + N_EXPERTS
    n_slots = n_blocks * MOE_TM
    flat_e = top_i.reshape(-1)
    onehot = (flat_e[:, None] == jnp.arange(N_EXPERTS, dtype=I32)[None, :]).astype(I32)
    csum = jnp.cumsum(onehot, axis=0)
    rank = jnp.take_along_axis(csum, flat_e[:, None], axis=1)[:, 0] - 1
    counts = csum[-1]
    padded = (counts + MOE_TM - 1) // MOE_TM * MOE_TM
    pad_end = jnp.cumsum(padded)
    pad_start = pad_end - padded
    dest = (pad_start[flat_e] + rank).astype(I32)
    flat_tok = (jnp.arange(n_assign, dtype=I32) // TOP_K)
    slot_tok = jnp.zeros((n_slots,), I32).at[dest].set(flat_tok)
    slot_gate = jnp.zeros((n_slots,), F32).at[dest].set(top_g.reshape(-1))
    block_e = jnp.minimum(
        jnp.searchsorted(pad_end, jnp.arange(n_blocks, dtype=I32) * MOE_TM, side="right"),
        N_EXPERTS - 1).astype(I32)
    n_active = (pad_end[-1] // MOE_TM).astype(I32).reshape(1)
    return dest, slot_tok, slot_gate.reshape(n_slots, 1), block_e, n_active


def _combine_kernel(yg_ref, x1_ref, g_ref, o_ref):
    acc_lo = jnp.zeros((yg_ref.shape[0], HALF), F32)
    acc_hi = jnp.zeros((yg_ref.shape[0], HALF), F32)
    for kk in range(TOP_K):
        lo, hi = _unpack_pairs(yg_ref[:, kk * HALF:(kk + 1) * HALF])
        acc_lo = acc_lo + lo
        acc_hi = acc_hi + hi
    x2 = x1_ref[...] + jnp.concatenate([acc_lo, acc_hi], axis=-1)
    o_ref[...] = _rms(x2) * g_ref[...]


def _combine(yg, x1, g, *, tm, row_offset):
    n = x1.shape[0]
    blk0 = row_offset // tm
    return pl.pallas_call(
        _combine_kernel,
        grid=(n // tm,),
        in_specs=[pl.BlockSpec((tm, TOP_K * HALF), lambda i: (i + blk0, 0)),
                  pl.BlockSpec((tm, D_MODEL), lambda i: (i, 0)),
                  pl.BlockSpec((1, D_MODEL), lambda i: (0, 0))],
        out_specs=pl.BlockSpec((tm, D_MODEL), lambda i: (i, 0)),
        out_shape=jax.ShapeDtypeStruct((n, D_MODEL), F32),
        compiler_params=pltpu.CompilerParams(dimension_semantics=("parallel",),
                                             vmem_limit_bytes=VMEM_LIMIT),
        name="combine",
    )(yg, x1, g)


def _extend_w_in(w):
    swap = np.arange(RET_QK_WIDTH).reshape(RET_HEADS, 2, RET_DK // 2)[:, ::-1, :].reshape(-1)
    qa, ka, va = w[:, 0:512], w[:, 512:1024], w[:, 1024:1536]
    qr, kr = w[:, 1536:1792], w[:, 1792:2048]
    vr, gr = w[:, 2048:2560], w[:, 2560:3072]
    sa = DA_HD ** -0.5
    sr = RET_DK ** -0.5
    return jnp.concatenate([qa * sa, ka, va, qr, qr[:, swap], kr * sr, kr[:, swap] * sr, vr, gr], axis=1)


def _rotary_tables(pos):
    half = RET_DK // 2
    inv = ROPE_BASE ** (-jnp.arange(half, dtype=F32) / half)
    ang = pos.astype(F32)[:, None] * inv[None, :]
    cos = jnp.cos(ang)
    sin = jnp.sin(ang)
    cos_t = jnp.tile(jnp.concatenate([cos, cos], axis=1), (1, RET_HEADS))
    sin_t = jnp.tile(jnp.concatenate([-sin, sin], axis=1), (1, RET_HEADS))
    return cos_t, sin_t


def kernel(x_prompt, x_sample, cache_k, cache_v, state_ret, page_table, norm_attn_g, w_in, lam_q1,
           lam_k1, lam_q2, lam_k2, da_subln_g, w_out, norm_ffn_g, w_router, b_router, w_up, b_up,
           w_down, b_down, norm_final_g):
    return _forward(x_prompt, x_sample, cache_k, cache_v, state_ret, page_table, norm_attn_g, w_in,
                    lam_q1, lam_k1, lam_q2, lam_k2, da_subln_g, w_out, norm_ffn_g, w_router,
                    b_router, w_up, b_up, w_down, b_down, norm_final_g,
                    tm=512, tq=512, pages_per_step=8, gather=_gather_rows)


def _forward(x_prompt, x_sample, cache_k, cache_v, state_ret, page_table, norm_attn_g, w_in, lam_q1,
             lam_k1, lam_q2, lam_k2, da_subln_g, w_out, norm_ffn_g, w_router, b_router, w_up, b_up,
             w_down, b_down, norm_final_g, *, tm, tq, pages_per_step, gather):
    batch, seq, _ = x_prompt.shape
    nb = x_sample.shape[0]
    n_pool, page = cache_k.shape[1], cache_k.shape[2]
    past_len = page_table.shape[1] * page
    n_prompt = batch * seq

    w_ext_b = _extend_w_in(w_in[0]).astype(BF16)
    g_attn = norm_attn_g[0].reshape(1, D_MODEL)
    g_ffn = norm_ffn_g[0].reshape(1, D_MODEL)
    g_fin = norm_final_g.reshape(1, D_MODEL)
    lamv = jnp.stack([lam_q1[0], lam_k1[0], lam_q2[0], lam_k2[0]]).astype(F32)
    subg = da_subln_g[0].reshape(1, DA_VD)
    w_o_b = w_out[0].astype(BF16)
    w_r = jnp.pad(w_router[0], ((0, 0), (0, LANES - N_EXPERTS))).astype(BF16)
    b_r = jnp.pad(b_router[0], (0, LANES - N_EXPERTS), constant_values=NEG_INF).reshape(1, LANES)
    w_up_b = jnp.concatenate([w_up[0][:, :, 0::2], w_up[0][:, :, 1::2]], axis=-1).astype(BF16)
    b_up_d = jnp.concatenate([b_up[0][:, 0::2], b_up[0][:, 1::2]], axis=-1).reshape(N_EXPERTS, 1, 2 * D_FF)
    w_down_b = w_down[0].astype(BF16)
    b_dn = b_down[0].reshape(N_EXPERTS, 1, D_MODEL)
    cos_p, sin_p = _rotary_tables(jnp.arange(seq))
    cos_s, sin_s = _rotary_tables(past_len + jnp.zeros((nb,), I32))

    xp = x_prompt.reshape(n_prompt, D_MODEL)
    qa, ka, va, kab, vab, qr, kr, vr, gr = _inproj(xp, g_attn, w_ext_b, cos_p, sin_p,
                                                   tm=tm, act=BF16)
    a_p = _attn_prompt(qa, kab, vab, lamv, subg, batch=batch, seq=seq, tq=tq)
    r_p, ret_p = _ret_prompt(qr, kr, vr, gr, batch=batch, seq=seq)
    x1_p, hfp_p, ti_p, tg_p = _outproj(a_p, r_p, xp, w_o_b, g_ffn, w_r, b_r, tm=tm)

    xs_ = x_sample.reshape(nb, D_MODEL)
    qa_s, ka_s, va_s, _, _, qr_s, kr_s, vr_s, gr_s = _inproj(xs_, g_attn, w_ext_b, cos_s, sin_s,
                                                             tm=nb, act=F32)
    ck = cache_k[0].reshape(n_pool, page, DA_WIDTH)
    cv = cache_v[0].reshape(n_pool, page, DA_WIDTH)
    a_s = _attn_sample(qa_s, ka_s, va_s, ck, cv, page_table, lamv, subg,
                       pages_per_step=pages_per_step)
    r_s, ret_s = _ret_sample(qr_s, kr_s, vr_s, gr_s, state_ret[0])
    x1_s, hfp_s, ti_s, tg_s = _outproj(a_s, r_s, xs_, w_o_b, g_ffn, w_r, b_r, tm=nb)

    n_tok = n_prompt + nb
    hfp = jnp.concatenate([hfp_p, hfp_s], axis=0)
    top_i = jnp.concatenate([ti_p[:, :TOP_K], ti_s[:, :TOP_K]], axis=0)
    top_g = jnp.concatenate([tg_p[:, :TOP_K], tg_s[:, :TOP_K]], axis=0)
    dest, slot_tok, slot_gate, block_e, n_active = _route(top_i, top_g, n_tok)
    xs_sorted = gather(hfp, slot_tok)
    out_sorted = _moe(block_e, n_active, xs_sorted, slot_gate, w_up_b, b_up_d, w_down_b, b_dn)
    yg = gather(out_sorted, dest)
    yg = yg.reshape(yg.shape[0] // TOP_K, TOP_K * HALF)
    y_p = _combine(yg, x1_p, g_fin, tm=tm, row_offset=0)
    y_s = _combine(yg, x1_s, g_fin, tm=nb, row_offset=n_prompt)

    depth = 1
    return (y_p.reshape(batch, seq, D_MODEL),
            y_s.reshape(nb, 1, D_MODEL),
            ka.reshape(depth, batch, seq, DA_HEADS, 2, DA_HD),
            va.reshape(depth, batch, seq, DA_HEADS, DA_VD),
            ret_p.reshape(depth, batch, RET_HEADS, RET_DK, RET_DV),
            ka_s.reshape(depth, nb, 1, DA_HEADS, 2, DA_HD),
            va_s.reshape(depth, nb, 1, DA_HEADS, DA_VD),
            ret_s.reshape(depth, nb, RET_HEADS, RET_DK, RET_DV))
```

```python
import functools
import math

import numpy as np
import jax
import jax.numpy as jnp
from jax import lax
from jax.experimental import pallas as pl
from jax.experimental.pallas import tpu as pltpu
from jax.experimental.pallas import tpu_sc as plsc

F32 = jnp.float32
BF16 = jnp.bfloat16
I32 = jnp.int32

D_MODEL = 1024
DA_HEADS = 4
DA_VD = 128
DA_HD = 64
DA_WIDTH = DA_HEADS * DA_VD
RET_HEADS = 4
RET_DV = 128
RET_DK = 64
RET_WIDTH = RET_HEADS * RET_DV
RET_QK_WIDTH = RET_HEADS * RET_DK
RET_CHUNK = 128
ROPE_BASE = 10000.0
N_EXPERTS = 32
TOP_K = 4
D_FF = 1024
SWIGLU_LIMIT = 7.0
SWIGLU_ALPHA = 1.702
NORM_EPS = 1e-5
NEG_INF = -1e30
LAM_INIT = 0.8 - 0.6 * math.exp(-0.3 * 0)
LANES = 128
HALF = D_MODEL // 2
QUARTER = D_MODEL // 4
MOE_TM = 256
PREP_ROWS = 64
SC_WINDOW = 128
SC_SUBCORES = 32
SC_UNIT = SC_WINDOW * SC_SUBCORES
VMEM_LIMIT = 56 * 1024 * 1024

_C_QA, _C_KA, _C_VA = 0, 512, 1024
_C_QR, _C_QRS, _C_KR, _C_KRS = 1536, 1792, 2048, 2304
_C_VR, _C_GR, _C_END = 2560, 3072, 3584


def _mm(a, b):
    return jnp.dot(a.astype(BF16), b.astype(BF16), preferred_element_type=F32)


def _bf(x):
    return x.astype(BF16).astype(F32)


def _rms(x):
    return x * lax.rsqrt(jnp.mean(x * x, axis=-1, keepdims=True) + NORM_EPS)


def _silu(g):
    return g / (1.0 + jnp.exp(-g))


def _pack_pairs(x):
    w = x.shape[-1] // 2
    lo = lax.bitcast_convert_type(x[:, :w].astype(BF16).astype(F32), jnp.uint32)
    hi = lax.bitcast_convert_type(x[:, w:].astype(BF16).astype(F32), jnp.uint32)
    return lax.bitcast_convert_type(hi | (lo >> 16), I32)


def _unpack_pairs(p):
    u = lax.bitcast_convert_type(p, jnp.uint32)
    lo = lax.bitcast_convert_type(u << 16, F32)
    hi = lax.bitcast_convert_type(u & jnp.uint32(0xFFFF0000), F32)
    return lo, hi


def _alibi_slopes():
    return np.asarray([2.0 ** (-8.0 * (h + 1) / DA_HEADS) for h in range(DA_HEADS)], np.float32)


def _inproj_kernel(x_ref, g_ref, w_ref, cos_ref, sin_ref,
                   qa_ref, ka_ref, va_ref, kab_ref, vab_ref,
                   qr_ref, kr_ref, vr_ref, gr_ref):
    hm = (_rms(x_ref[...]) * g_ref[...]).astype(BF16)

    def mm(lo, hi):
        return _mm(hm, w_ref[:, lo:hi])

    qa_ref[...] = mm(_C_QA, _C_KA).astype(qa_ref.dtype)
    ka = mm(_C_KA, _C_VA)
    ka_ref[...] = ka
    kab_ref[...] = ka.astype(BF16)
    va = mm(_C_VA, _C_QR)
    va_ref[...] = va
    vab_ref[...] = va.astype(BF16)
    cos = cos_ref[...]
    sin = sin_ref[...]
    qr_ref[...] = (mm(_C_QR, _C_QRS) * cos + mm(_C_QRS, _C_KR) * sin).astype(qr_ref.dtype)
    kr_ref[...] = (mm(_C_KR, _C_KRS) * cos + mm(_C_KRS, _C_VR) * sin).astype(kr_ref.dtype)
    vr_ref[...] = mm(_C_VR, _C_GR).astype(vr_ref.dtype)
    gr_ref[...] = mm(_C_GR, _C_END).astype(gr_ref.dtype)


def _inproj(x, g, w_ext, cos_t, sin_t, *, tm, act):
    n = x.shape[0]
    n_tab = cos_t.shape[0] // tm
    row = lambda w: pl.BlockSpec((tm, w), lambda i: (i, 0))
    tab = pl.BlockSpec((tm, RET_QK_WIDTH), lambda i: (i % n_tab, 0))
    out_shape = (
        jax.ShapeDtypeStruct((n, DA_WIDTH), act),
        jax.ShapeDtypeStruct((n, DA_WIDTH), F32),
        jax.ShapeDtypeStruct((n, DA_WIDTH), F32),
        jax.ShapeDtypeStruct((n, DA_WIDTH), BF16),
        jax.ShapeDtypeStruct((n, DA_WIDTH), BF16),
        jax.ShapeDtypeStruct((n, RET_QK_WIDTH), act),
        jax.ShapeDtypeStruct((n, RET_QK_WIDTH), act),
        jax.ShapeDtypeStruct((n, RET_WIDTH), act),
        jax.ShapeDtypeStruct((n, RET_WIDTH), act),
    )
    return pl.pallas_call(
        _inproj_kernel,
        grid=(n // tm,),
        in_specs=[row(D_MODEL),
                  pl.BlockSpec((1, D_MODEL), lambda i: (0, 0)),
                  pl.BlockSpec((D_MODEL, _C_END), lambda i: (0, 0)),
                  tab, tab],
        out_specs=(row(DA_WIDTH), row(DA_WIDTH), row(DA_WIDTH), row(DA_WIDTH), row(DA_WIDTH),
                   row(RET_QK_WIDTH), row(RET_QK_WIDTH), row(RET_WIDTH), row(RET_WIDTH)),
        out_shape=out_shape,
        compiler_params=pltpu.CompilerParams(dimension_semantics=("parallel",),
                                             vmem_limit_bytes=VMEM_LIMIT),
        name="inproj",
    )(x, g, w_ext, cos_t, sin_t)


def _lambda_value(lamv_ref):
    lv = lamv_ref[...]
    s1 = jnp.sum(lv[0:1] * lv[1:2], axis=-1, keepdims=True)
    s2 = jnp.sum(lv[2:3] * lv[3:4], axis=-1, keepdims=True)
    return jnp.exp(s1) - jnp.exp(s2) + LAM_INIT


def _attn_prompt_kernel(qi_tab, ki_tab, q_ref, k_ref, v_ref, rel_ref, relm_ref, off_ref,
                        lamv_ref, subg_ref, o_ref, m_sc, l_sc, acc_sc):
    hd = pl.program_id(1)
    t = pl.program_id(2)
    qi = qi_tab[t]
    ki = ki_tab[t]

    @pl.when(ki == 0)
    def _():
        m_sc[...] = jnp.full(m_sc.shape, NEG_INF, F32)
        l_sc[...] = jnp.zeros(l_sc.shape, F32)
        acc_sc[...] = jnp.zeros(acc_sc.shape, F32)

    off = off_ref[hd, t]

    def update(rel):
        q = q_ref[...]
        k = k_ref[...]
        v = v_ref[...]
        for c in range(2):
            s = lax.dot_general(q[:, c * DA_HD:(c + 1) * DA_HD], k[:, c * DA_HD:(c + 1) * DA_HD],
                                (((1,), (1,)), ((), ())), preferred_element_type=F32)
            tt = s + rel
            m_old = m_sc[c]
            m_new = jnp.maximum(m_old, jnp.max(tt, axis=-1, keepdims=True) + off)
            alpha = jnp.exp(m_old - m_new)
            p = jnp.exp(tt + (off - m_new))
            l_sc[c] = alpha * l_sc[c] + jnp.sum(p, axis=-1, keepdims=True)
            acc_sc[c] = alpha * acc_sc[c] + jnp.dot(p.astype(BF16), v, preferred_element_type=F32)
            m_sc[c] = m_new

    @pl.when(ki < qi)
    def _():
        update(rel_ref[0])

    @pl.when(ki == qi)
    def _():
        update(relm_ref[0])
        lam = _lambda_value(lamv_ref)
        o = acc_sc[0] / l_sc[0] - lam * (acc_sc[1] / l_sc[1])
        a = _rms(o) * subg_ref[...] * (1.0 - LAM_INIT)
        o_ref[...] = a.astype(o_ref.dtype)


def _attn_prompt(qa, kab, vab, lamv, subg, *, batch, seq, tq):
    nq = seq // tq
    steps = [(i, j) for i in range(nq) for j in range(i + 1)]
    qi_tab = jnp.asarray([s[0] for s in steps], I32)
    ki_tab = jnp.asarray([s[1] for s in steps], I32)
    slopes = _alibi_slopes()
    dist = (np.arange(tq)[:, None] - np.arange(tq)[None, :]).astype(np.float32)
    rel = -slopes[:, None, None] * dist[None]
    relm = np.where(dist[None] >= 0, rel, np.float32(NEG_INF)).astype(np.float32)
    blk = np.asarray([(s[0] - s[1]) * tq for s in steps], np.float32)
    off = -slopes[:, None] * blk[None, :]
    n = batch * seq
    qspec = pl.BlockSpec((tq, DA_VD), lambda b, h, t, qt, kt: (b * nq + qt[t], h))
    kspec = pl.BlockSpec((tq, DA_VD), lambda b, h, t, qt, kt: (b * nq + kt[t], h))
    relspec = pl.BlockSpec((1, tq, tq), lambda b, h, t, qt, kt: (h, 0, 0))
    grid_spec = pltpu.PrefetchScalarGridSpec(
        num_scalar_prefetch=2,
        grid=(batch, DA_HEADS, len(steps)),
        in_specs=[qspec, kspec, kspec, relspec, relspec,
                  pl.BlockSpec(memory_space=pltpu.SMEM),
                  pl.BlockSpec((4, DA_HD), lambda b, h, t, qt, kt: (0, 0)),
                  pl.BlockSpec((1, DA_VD), lambda b, h, t, qt, kt: (0, 0))],
        out_specs=qspec,
        scratch_shapes=[pltpu.VMEM((2, tq, 1), F32), pltpu.VMEM((2, tq, 1), F32),
                        pltpu.VMEM((2, tq, DA_VD), F32)],
    )
    return pl.pallas_call(
        _attn_prompt_kernel,
        grid_spec=grid_spec,
        out_shape=jax.ShapeDtypeStruct((n, DA_WIDTH), BF16),
        compiler_params=pltpu.CompilerParams(
            dimension_semantics=("parallel", "parallel", "arbitrary"),
            vmem_limit_bytes=VMEM_LIMIT),
        name="attn_prompt",
    )(qi_tab, ki_tab, qa, kab, vab, jnp.asarray(rel), jnp.asarray(relm), jnp.asarray(off),
      lamv, subg)


def _ret_prompt_kernel(q_ref, k_ref, v_ref, g_ref, dec_ref, qdec_ref, kdec_ref, sdec_ref,
                       r_ref, s_ref, *, batch):
    c = pl.program_id(0)

    @pl.when(c == 0)
    def _():
        s_ref[...] = jnp.zeros(s_ref.shape, F32)

    qdec = qdec_ref[...]
    kdec = kdec_ref[...]
    for b in range(batch):
        q = q_ref[b].astype(F32)
        k = k_ref[b].astype(F32)
        qd = (q * qdec).astype(BF16)
        kd = (k * kdec).astype(BF16)
        qb = q_ref[b]
        kb = k_ref[b]
        for h in range(RET_HEADS):
            ks = slice(h * RET_DK, (h + 1) * RET_DK)
            vs = slice(h * RET_DV, (h + 1) * RET_DV)
            vh = v_ref[b, :, vs]
            qk = lax.dot_general(qb[:, ks], kb[:, ks], (((1,), (1,)), ((), ())),
                                 preferred_element_type=F32) * dec_ref[h]
            s_old = s_ref[b, h]
            o = jnp.dot(qk.astype(BF16), vh, preferred_element_type=F32)
            o = o + jnp.dot(qd[:, ks], s_old.astype(BF16), preferred_element_type=F32)
            s_ref[b, h] = sdec_ref[h] * s_old + lax.dot_general(
                kd[:, ks], vh, (((0,), (0,)), ((), ())), preferred_element_type=F32)
            gate = _silu(g_ref[b, :, vs].astype(F32))
            r_ref[b, :, vs] = (_rms(o) * gate).astype(r_ref.dtype)


def _ret_tables(length):
    log_g = jnp.log(1.0 - 2.0 ** (-5.0 - jnp.arange(RET_HEADS, dtype=F32)))
    idx = jnp.arange(length, dtype=F32)
    diff = idx[:, None] - idx[None, :]
    dec = jnp.where(diff >= 0, jnp.exp(jnp.maximum(diff, 0.0)[None] * log_g[:, None, None]), 0.0)
    qdec = jnp.exp((idx + 1.0)[:, None] * log_g[None, :])
    kdec = jnp.exp((length - 1.0 - idx)[:, None] * log_g[None, :])
    sdec = jnp.exp(length * log_g)
    rep = lambda a: jnp.repeat(a, RET_DK, axis=1)
    sdec_b = jnp.broadcast_to(sdec[:, None, None], (RET_HEADS, RET_DK, RET_DV))
    return dec, rep(qdec), rep(kdec), sdec_b


def _ret_prompt(qr, kr, vr, gr, *, batch, seq):
    nc = seq // RET_CHUNK
    dec, qdec, kdec, sdec = _ret_tables(RET_CHUNK)
    q3 = qr.reshape(batch, seq, RET_QK_WIDTH)
    k3 = kr.reshape(batch, seq, RET_QK_WIDTH)
    v3 = vr.reshape(batch, seq, RET_WIDTH)
    g3 = gr.reshape(batch, seq, RET_WIDTH)
    blk = lambda w: pl.BlockSpec((batch, RET_CHUNK, w), lambda c: (0, c, 0))
    const = lambda shape: pl.BlockSpec(shape, lambda c: (0,) * len(shape))
    r, s = pl.pallas_call(
        functools.partial(_ret_prompt_kernel, batch=batch),
        grid=(nc,),
        in_specs=[blk(RET_QK_WIDTH), blk(RET_QK_WIDTH), blk(RET_WIDTH), blk(RET_WIDTH),
                  const((RET_HEADS, RET_CHUNK, RET_CHUNK)),
                  const((RET_CHUNK, RET_QK_WIDTH)), const((RET_CHUNK, RET_QK_WIDTH)),
                  const((RET_HEADS, RET_DK, RET_DV))],
        out_specs=(blk(RET_WIDTH), const((batch, RET_HEADS, RET_DK, RET_DV))),
        out_shape=(jax.ShapeDtypeStruct((batch, seq, RET_WIDTH), BF16),
                   jax.ShapeDtypeStruct((batch, RET_HEADS, RET_DK, RET_DV), F32)),
        compiler_params=pltpu.CompilerParams(dimension_semantics=("arbitrary",),
                                             vmem_limit_bytes=VMEM_LIMIT),
        name="ret_prompt",
    )(q3, k3, v3, g3, dec, qdec, kdec, sdec)
    return r.reshape(batch * seq, RET_WIDTH), s


def _attn_sample_kernel(pt_ref, q_ref, kn_ref, vn_ref, rel_ref, slope_ref, lamv_ref, subg_ref,
                        *rest, pages_per_step, past_len, n_steps):
    g_pages = pages_per_step
    k_refs = rest[:g_pages]
    v_refs = rest[g_pages:2 * g_pages]
    o_ref = rest[2 * g_pages]
    s_sc, wnew_sc, acc_sc = rest[2 * g_pages + 1:]
    j = pl.program_id(1)
    rows = 2 * DA_HEADS

    def q_rows():
        col_group = lax.broadcasted_iota(I32, (rows, DA_WIDTH), 1) // DA_HD
        row_id = lax.broadcasted_iota(I32, (rows, DA_WIDTH), 0)
        want = 2 * (row_id % DA_HEADS) + row_id // DA_HEADS
        qb = jnp.broadcast_to(_bf(q_ref[0]), (rows, DA_WIDTH))
        return jnp.where(col_group == want, qb, 0.0)

    @pl.when(j < n_steps)
    def _():
        qr = q_rows().astype(BF16)
        for g in range(g_pages):
            page = j * g_pages + g
            s = lax.dot_general(qr, k_refs[g][0].astype(BF16), (((1,), (1,)), ((), ())),
                                preferred_element_type=F32)
            base = (page * LANES - past_len).astype(F32)
            s_sc[page] = s + (rel_ref[...] + slope_ref[...] * base)

    @pl.when(j == n_steps - 1)
    def _():
        s_all = s_sc[...]
        s_new = jnp.sum(q_rows() * _bf(kn_ref[0]), axis=-1, keepdims=True)
        m = jnp.maximum(jnp.max(jnp.max(s_all, axis=0), axis=-1, keepdims=True), s_new)
        p = jnp.exp(s_all - m)
        p_new = jnp.exp(s_new - m)
        l = jnp.sum(jnp.sum(p, axis=0), axis=-1, keepdims=True) + p_new
        wn = p / l
        wn_new = p_new / l
        lam = _lambda_value(lamv_ref)
        s_sc[:, 0:DA_HEADS, :] = wn[:, 0:DA_HEADS, :] - lam * wn[:, DA_HEADS:rows, :]
        s_sc[:, DA_HEADS:rows, :] = jnp.zeros((s_sc.shape[0], DA_HEADS, LANES), F32)
        wnew_sc[...] = wn_new[0:DA_HEADS] - lam * wn_new[DA_HEADS:rows]
        acc_sc[...] = jnp.zeros(acc_sc.shape, F32)

    @pl.when(j >= n_steps)
    def _():
        acc = acc_sc[...]
        for g in range(g_pages):
            page = (j - n_steps) * g_pages + g
            acc = acc + jnp.dot(s_sc[page].astype(BF16), v_refs[g][0].astype(BF16),
                                preferred_element_type=F32)
        acc_sc[...] = acc

    @pl.when(j == 2 * n_steps - 1)
    def _():
        acc = acc_sc[...]
        w_new = _bf(wnew_sc[...])
        v_new = _bf(vn_ref[0])
        outs = []
        for h in range(DA_HEADS):
            vs = slice(h * DA_VD, (h + 1) * DA_VD)
            o = acc[h:h + 1, vs] + w_new[h:h + 1, :] * v_new[:, vs]
            outs.append(_rms(o) * subg_ref[...] * (1.0 - LAM_INIT))
        o_ref[0] = jnp.concatenate(outs, axis=-1)


def _attn_sample(q, k_new, v_new, cache_k, cache_v, page_table, lamv, subg, *, pages_per_step):
    nb, n_pages = page_table.shape
    page = cache_k.shape[1]
    assert page == LANES
    past_len = n_pages * page
    g_pages = pages_per_step
    n_steps = n_pages // g_pages
    rows = 2 * DA_HEADS
    slope_rows = np.tile(_alibi_slopes(), 2)[:, None]
    rel = slope_rows * np.arange(page, dtype=np.float32)[None, :]
    pt = page_table.reshape(-1).astype(I32)
    row3 = lambda: pl.BlockSpec((1, 1, DA_WIDTH), lambda b, j, pt: (b, 0, 0))
    const2 = lambda shape: pl.BlockSpec(shape, lambda b, j, pt: (0, 0))

    def k_spec(g):
        return pl.BlockSpec(
            (1, page, DA_WIDTH),
            lambda b, j, pt: (pt[b * n_pages + jnp.minimum(j, n_steps - 1) * g_pages + g], 0, 0))

    def v_spec(g):
        return pl.BlockSpec(
            (1, page, DA_WIDTH),
            lambda b, j, pt: (pt[b * n_pages + jnp.maximum(j - n_steps, 0) * g_pages + g], 0, 0))

    grid_spec = pltpu.PrefetchScalarGridSpec(
        num_scalar_prefetch=1,
        grid=(nb, 2 * n_steps),
        in_specs=[row3(), row3(), row3(), const2((rows, page)), const2((rows, 1)),
                  const2((4, DA_HD)), const2((1, DA_VD))]
                 + [k_spec(g) for g in range(g_pages)] + [v_spec(g) for g in range(g_pages)],
        out_specs=row3(),
        scratch_shapes=[pltpu.VMEM((n_pages, rows, page), F32), pltpu.VMEM((DA_HEADS, 1), F32),
                        pltpu.VMEM((rows, DA_WIDTH), F32)],
    )
    out = pl.pallas_call(
        functools.partial(_attn_sample_kernel, pages_per_step=g_pages, past_len=past_len,
                          n_steps=n_steps),
        grid_spec=grid_spec,
        out_shape=jax.ShapeDtypeStruct((nb, 1, DA_WIDTH), F32),
        compiler_params=pltpu.CompilerParams(dimension_semantics=("parallel", "arbitrary"),
                                             vmem_limit_bytes=VMEM_LIMIT),
        name="attn_sample",
    )(pt, q.reshape(nb, 1, DA_WIDTH), k_new.reshape(nb, 1, DA_WIDTH), v_new.reshape(nb, 1, DA_WIDTH),
      jnp.asarray(rel), jnp.asarray(slope_rows), lamv, subg,
      *([cache_k] * g_pages), *([cache_v] * g_pages))
    return out.reshape(nb, DA_WIDTH)


def _ret_sample_kernel(q_ref, k_ref, qc_ref, kc_ref, v_ref, g_ref, s_ref, qdec_ref, sdec_ref,
                       r_ref, so_ref):
    q = _bf(q_ref[0])
    k = _bf(k_ref[0])
    qd_col = _bf(qc_ref[0] * qdec_ref[...])
    k_col = _bf(kc_ref[0])
    outs = []
    for h in range(RET_HEADS):
        ks = slice(h * RET_DK, (h + 1) * RET_DK)
        vs = slice(h * RET_DV, (h + 1) * RET_DV)
        vh = _bf(v_ref[0, :, vs])
        s_old = s_ref[0, h]
        qk = jnp.sum(q[:, ks] * k[:, ks], axis=-1, keepdims=True)
        o = _bf(qk) * vh + jnp.sum(qd_col[ks, :] * _bf(s_old), axis=0, keepdims=True)
        so_ref[0, h] = sdec_ref[h] * s_old + k_col[ks, :] * vh
        outs.append(_rms(o) * _silu(g_ref[0, :, vs]))
    r_ref[0] = jnp.concatenate(outs, axis=-1)


def _ret_sample(qr, kr, vr, gr, state):
    nb = qr.shape[0]
    _, qdec, _, sdec = _ret_tables(1)
    row3 = lambda w: pl.BlockSpec((1, 1, w), lambda b: (b, 0, 0))
    col3 = pl.BlockSpec((1, RET_QK_WIDTH, 1), lambda b: (b, 0, 0))
    st = pl.BlockSpec((1, RET_HEADS, RET_DK, RET_DV), lambda b: (b, 0, 0, 0))
    r, s_new = pl.pallas_call(
        _ret_sample_kernel,
        grid=(nb,),
        in_specs=[row3(RET_QK_WIDTH), row3(RET_QK_WIDTH), col3, col3, row3(RET_WIDTH),
                  row3(RET_WIDTH), st,
                  pl.BlockSpec((RET_QK_WIDTH, 1), lambda b: (0, 0)),
                  pl.BlockSpec((RET_HEADS, RET_DK, RET_DV), lambda b: (0, 0, 0))],
        out_specs=(row3(RET_WIDTH), st),
        out_shape=(jax.ShapeDtypeStruct((nb, 1, RET_WIDTH), F32),
                   jax.ShapeDtypeStruct((nb, RET_HEADS, RET_DK, RET_DV), F32)),
        compiler_params=pltpu.CompilerParams(dimension_semantics=("parallel",)),
        name="ret_sample",
    )(qr.reshape(nb, 1, -1), kr.reshape(nb, 1, -1), qr.reshape(nb, -1, 1), kr.reshape(nb, -1, 1),
      vr.reshape(nb, 1, -1), gr.reshape(nb, 1, -1), state, qdec.reshape(-1, 1), sdec)
    return r.reshape(nb, RET_WIDTH), s_new


def _outproj_kernel(a_ref, r_ref, x_ref, wo_ref, g_ref, wr_ref, br_ref,
                    x1_ref, hfp_ref, ti_ref, tg_ref):
    mix = _mm(a_ref[...], wo_ref[:DA_WIDTH, :]) + _mm(r_ref[...], wo_ref[DA_WIDTH:, :])
    x1 = x_ref[...] + mix
    x1_ref[...] = x1
    hf = _rms(x1) * g_ref[...]
    hfp_ref[0] = _pack_pairs(hf[:, :HALF])
    hfp_ref[1] = _pack_pairs(hf[:, HALF:])
    logits = _mm(hf, wr_ref[...]) + br_ref[...]
    lane = lax.broadcasted_iota(I32, logits.shape, 1)
    work = logits
    top_v, top_i = [], []
    for _ in range(TOP_K):
        m = jnp.max(work, axis=-1, keepdims=True)
        idx = jnp.min(jnp.where(work == m, lane, LANES), axis=-1, keepdims=True)
        top_v.append(m)
        top_i.append(idx)
        work = jnp.where(lane == idx, -jnp.inf, work)
    es = [jnp.exp(v - top_v[0]) for v in top_v]
    denom = es[0] + es[1] + es[2] + es[3]
    ti = jnp.zeros(logits.shape, I32)
    tg = jnp.zeros(logits.shape, F32)
    for kk in range(TOP_K):
        ti = jnp.where(lane == kk, top_i[kk], ti)
        tg = jnp.where(lane == kk, es[kk] / denom, tg)
    ti_ref[...] = ti
    tg_ref[...] = tg


def _outproj(a, r, x, w_out, g, w_router, b_router, *, tm):
    n = x.shape[0]
    row = lambda w: pl.BlockSpec((tm, w), lambda i: (i, 0))
    const = lambda shape: pl.BlockSpec(shape, lambda i: (0, 0))
    return pl.pallas_call(
        _outproj_kernel,
        grid=(n // tm,),
        in_specs=[row(DA_WIDTH), row(RET_WIDTH), row(D_MODEL), const((D_MODEL, D_MODEL)),
                  const((1, D_MODEL)), const((D_MODEL, LANES)), const((1, LANES))],
        out_specs=(row(D_MODEL), pl.BlockSpec((2, tm, QUARTER), lambda i: (0, i, 0)),
                   row(LANES), row(LANES)),
        out_shape=(jax.ShapeDtypeStruct((n, D_MODEL), F32),
                   jax.ShapeDtypeStruct((2, n, QUARTER), I32),
                   jax.ShapeDtypeStruct((n, LANES), I32),
                   jax.ShapeDtypeStruct((n, LANES), F32)),
        compiler_params=pltpu.CompilerParams(dimension_semantics=("parallel",),
                                             vmem_limit_bytes=VMEM_LIMIT),
        name="outproj",
    )(a, r, x, w_out, g, w_router, b_router)


def _sc_gather(x, idx):
    m = idx.shape[0]
    cols = x.shape[1]
    mesh = plsc.VectorSubcoreMesh(core_axis_name="c", subcore_axis_name="s")

    @pl.kernel(out_type=jax.ShapeDtypeStruct((m, cols), x.dtype), mesh=mesh)
    def gather_kernel(x_hbm, i_hbm, o_hbm):
        def body(i_vmem, o_vmem):
            pltpu.sync_copy(x_hbm.at[i_vmem.at[0]], o_vmem)

        pltpu.emit_pipeline(
            body,
            grid=(m // SC_WINDOW,),
            in_specs=[pl.BlockSpec((1, SC_WINDOW), lambda i: (0, i))],
            out_specs=[pl.BlockSpec((SC_WINDOW, cols), lambda i: (i, 0))],
            core_axis_name=("c", "s"),
            dimension_semantics=(pltpu.PARALLEL,),
        )(i_hbm, o_hbm)

    return gather_kernel(x, idx.reshape(1, m))


def _pad_to(idx, n):
    return jnp.pad(idx, (0, n - idx.shape[0]))


def _prep_expert_weights(wup_ref, wdn_ref, wupb, wdnb):
    lane = lax.broadcasted_iota(I32, (PREP_ROWS, LANES), 1)
    first_half = lane < LANES // 2
    idx_even = (2 * lane) % LANES
    idx_odd = (2 * lane + 1) % LANES

    def body(r, carry):
        rows = pl.ds(pl.multiple_of(r * PREP_ROWS, PREP_ROWS), PREP_ROWS)
        for ct in range(D_FF // LANES):
            a = wup_ref[0, rows, 2 * ct * LANES:(2 * ct + 1) * LANES]
            b = wup_ref[0, rows, (2 * ct + 1) * LANES:(2 * ct + 2) * LANES]
            even = jnp.where(first_half, jnp.take_along_axis(a, idx_even, axis=1),
                             jnp.take_along_axis(b, idx_even, axis=1))
            odd = jnp.where(first_half, jnp.take_along_axis(a, idx_odd, axis=1),
                            jnp.take_along_axis(b, idx_odd, axis=1))
            wupb[rows, ct * LANES:(ct + 1) * LANES] = even.astype(BF16)
            wupb[rows, D_FF + ct * LANES:D_FF + (ct + 1) * LANES] = odd.astype(BF16)
        wdnb[rows, :] = wdn_ref[0, rows, :].astype(BF16)
        return carry

    lax.fori_loop(0, D_MODEL // PREP_ROWS, body, 0)


def _moe_kernel(be_ref, na_ref, xs_ref, wup_ref, bup_ref, wdn_ref, bdn_ref, out_ref, wupb, wdnb):
    i = pl.program_id(0)
    n_act = na_ref[0]
    e = be_ref[jnp.minimum(i, n_act - 1)]
    e_prev = be_ref[jnp.maximum(i - 1, 0)]

    @pl.when(jnp.logical_and(i < n_act, jnp.logical_or(i == 0, e != e_prev)))
    def _():
        _prep_expert_weights(wup_ref, wdn_ref, wupb, wdnb)

    @pl.when(i < n_act)
    def _():
        lo0, hi0 = _unpack_pairs(xs_ref[0])
        lo1, hi1 = _unpack_pairs(xs_ref[1])
        x = jnp.concatenate([lo0, hi0, lo1, hi1], axis=-1).astype(BF16)
        u = jnp.dot(x, wupb[...], preferred_element_type=F32) + bup_ref[0]
        g = jnp.minimum(u[:, :D_FF], SWIGLU_LIMIT)
        lin = jnp.clip(u[:, D_FF:], -SWIGLU_LIMIT, SWIGLU_LIMIT)
        y = g * (1.0 / (1.0 + jnp.exp(-SWIGLU_ALPHA * g))) * (lin + 1.0)
        o = jnp.dot(y.astype(BF16), wdnb[...], preferred_element_type=F32) + bdn_ref[0]
        out_ref[0] = _pack_pairs(o[:, :HALF])
        out_ref[1] = _pack_pairs(o[:, HALF:])

    @pl.when(i >= n_act)
    def _():
        out_ref[...] = jnp.zeros(out_ref.shape, I32)


def _moe(block_e, n_active, xs, w_up, b_up_d, w_down, b_down):
    n_blocks = block_e.shape[0]
    n_slots = n_blocks * MOE_TM

    def act(i, na):
        return jnp.minimum(i, na[0] - 1)

    grid_spec = pltpu.PrefetchScalarGridSpec(
        num_scalar_prefetch=2,
        grid=(n_blocks,),
        in_specs=[
            pl.BlockSpec((2, MOE_TM, QUARTER), lambda i, be, na: (0, act(i, na), 0)),
            pl.BlockSpec((1, D_MODEL, 2 * D_FF), lambda i, be, na: (be[act(i, na)], 0, 0)),
            pl.BlockSpec((1, 1, 2 * D_FF), lambda i, be, na: (be[act(i, na)], 0, 0)),
            pl.BlockSpec((1, D_FF, D_MODEL), lambda i, be, na: (be[act(i, na)], 0, 0)),
            pl.BlockSpec((1, 1, D_MODEL), lambda i, be, na: (be[act(i, na)], 0, 0)),
        ],
        out_specs=pl.BlockSpec((2, MOE_TM, QUARTER), lambda i, be, na: (0, i, 0)),
        scratch_shapes=[pltpu.VMEM((D_MODEL, 2 * D_FF), BF16), pltpu.VMEM((D_FF, D_MODEL), BF16)],
    )
    return pl.pallas_call(
        _moe_kernel,
        grid_spec=grid_spec,
        out_shape=jax.ShapeDtypeStruct((2, n_slots, QUARTER), I32),
        compiler_params=pltpu.CompilerParams(dimension_semantics=("arbitrary",),
                                             vmem_limit_bytes=VMEM_LIMIT),
        name="moe_ffn",
    )(block_e, n_active, xs, w_up, b_up_d, w_down, b_down)


def _route(top_i, n_tokens):
    n_assign = n_tokens * TOP_K
    n_blocks = -(-n_assign // MOE_TM) + N_EXPERTS
    n_slots = n_blocks * MOE_TM
    flat_e = top_i.reshape(-1)
    onehot = (flat_e[:, None] == jnp.arange(N_EXPERTS, dtype=I32)[None, :]).astype(I32)
    csum = jnp.cumsum(onehot, axis=0)
    rank = jnp.sum(onehot * csum, axis=1) - 1
    counts = csum[-1]
    padded = (counts + MOE_TM - 1) // MOE_TM * MOE_TM
    pad_end = jnp.cumsum(padded)
    pad_start = pad_end - padded
    dest = (jnp.sum(onehot * pad_start[None, :], axis=1) + rank).astype(I32)
    flat_tok = (jnp.arange(n_assign, dtype=I32) // TOP_K)
    slot_tok = jnp.zeros((n_slots,), I32).at[dest].set(flat_tok)
    block_e = jnp.minimum(
        jnp.searchsorted(pad_end, jnp.arange(n_blocks, dtype=I32) * MOE_TM, side="right"),
        N_EXPERTS - 1).astype(I32)
    n_active = (pad_end[-1] // MOE_TM).astype(I32).reshape(1)
    return dest.reshape(n_tokens, TOP_K), slot_tok, block_e, n_active


def _combine_kernel(yg_ref, tg_ref, x1_ref, g_ref, o_ref):
    tg = tg_ref[...]
    parts = [jnp.zeros((x1_ref.shape[0], QUARTER), F32) for _ in range(4)]
    for kk in range(TOP_K):
        gate = tg[:, kk:kk + 1]
        lo0, hi0 = _unpack_pairs(yg_ref[0, kk])
        lo1, hi1 = _unpack_pairs(yg_ref[1, kk])
        for j, piece in enumerate((lo0, hi0, lo1, hi1)):
            parts[j] = parts[j] + gate * piece
    x2 = x1_ref[...] + jnp.concatenate(parts, axis=-1)
    o_ref[...] = _rms(x2) * g_ref[...]


def _combine(yg, tg, x1, g, *, tm, row_offset):
    n = x1.shape[0]
    blk0 = row_offset // tm
    return pl.pallas_call(
        _combine_kernel,
        grid=(n // tm,),
        in_specs=[pl.BlockSpec((2, TOP_K, tm, QUARTER), lambda i: (0, 0, i + blk0, 0)),
                  pl.BlockSpec((tm, LANES), lambda i: (i, 0)),
                  pl.BlockSpec((tm, D_MODEL), lambda i: (i, 0)),
                  pl.BlockSpec((1, D_MODEL), lambda i: (0, 0))],
        out_specs=pl.BlockSpec((tm, D_MODEL), lambda i: (i, 0)),
        out_shape=jax.ShapeDtypeStruct((n, D_MODEL), F32),
        compiler_params=pltpu.CompilerParams(dimension_semantics=("parallel",),
                                             vmem_limit_bytes=VMEM_LIMIT),
        name="combine",
    )(yg, tg, x1, g)


def _extend_w_in(w):
    swap = np.arange(RET_QK_WIDTH).reshape(RET_HEADS, 2, RET_DK // 2)[:, ::-1, :].reshape(-1)
    qa, ka, va = w[:, 0:512], w[:, 512:1024], w[:, 1024:1536]
    qr, kr = w[:, 1536:1792], w[:, 1792:2048]
    vr, gr = w[:, 2048:2560], w[:, 2560:3072]
    sa = DA_HD ** -0.5
    sr = RET_DK ** -0.5
    return jnp.concatenate([qa * sa, ka, va, qr, qr[:, swap], kr * sr, kr[:, swap] * sr, vr, gr], axis=1)


def _rotary_tables(pos):
    half = RET_DK // 2
    inv = ROPE_BASE ** (-jnp.arange(half, dtype=F32) / half)
    ang = pos.astype(F32)[:, None] * inv[None, :]
    cos = jnp.cos(ang)
    sin = jnp.sin(ang)
    cos_t = jnp.tile(jnp.concatenate([cos, cos], axis=1), (1, RET_HEADS))
    sin_t = jnp.tile(jnp.concatenate([-sin, sin], axis=1), (1, RET_HEADS))
    return cos_t, sin_t


def kernel(x_prompt, x_sample, cache_k, cache_v, state_ret, page_table, norm_attn_g, w_in, lam_q1,
           lam_k1, lam_q2, lam_k2, da_subln_g, w_out, norm_ffn_g, w_router, b_router, w_up, b_up,
           w_down, b_down, norm_final_g):
    return _forward(x_prompt, x_sample, cache_k, cache_v, state_ret, page_table, norm_attn_g, w_in,
                    lam_q1, lam_k1, lam_q2, lam_k2, da_subln_g, w_out, norm_ffn_g, w_router,
                    b_router, w_up, b_up, w_down, b_down, norm_final_g,
                    tm=512, tq=512, pages_per_step=8, gather=_sc_gather)


def _forward(x_prompt, x_sample, cache_k, cache_v, state_ret, page_table, norm_attn_g, w_in, lam_q1,
             lam_k1, lam_q2, lam_k2, da_subln_g, w_out, norm_ffn_g, w_router, b_router, w_up, b_up,
             w_down, b_down, norm_final_g, *, tm, tq, pages_per_step, gather):
    batch, seq, _ = x_prompt.shape
    nb = x_sample.shape[0]
    n_pool, page = cache_k.shape[1], cache_k.shape[2]
    past_len = page_table.shape[1] * page
    n_prompt = batch * seq

    w_ext_b = _extend_w_in(w_in[0]).astype(BF16)
    g_attn = norm_attn_g[0].reshape(1, D_MODEL)
    g_ffn = norm_ffn_g[0].reshape(1, D_MODEL)
    g_fin = norm_final_g.reshape(1, D_MODEL)
    lamv = jnp.stack([lam_q1[0], lam_k1[0], lam_q2[0], lam_k2[0]]).astype(F32)
    subg = da_subln_g[0].reshape(1, DA_VD)
    w_o_b = w_out[0].astype(BF16)
    w_r = jnp.pad(w_router[0], ((0, 0), (0, LANES - N_EXPERTS))).astype(BF16)
    b_r = jnp.pad(b_router[0], (0, LANES - N_EXPERTS), constant_values=NEG_INF).reshape(1, LANES)
    assert cache_k.shape[0] == 1 and w_up.shape[0] == 1
    w_up_e = w_up.reshape(N_EXPERTS, D_MODEL, 2 * D_FF)
    b_up_d = jnp.concatenate([b_up[0][:, 0::2], b_up[0][:, 1::2]], axis=-1).reshape(N_EXPERTS, 1, 2 * D_FF)
    w_down_e = w_down.reshape(N_EXPERTS, D_FF, D_MODEL)
    b_dn = b_down.reshape(N_EXPERTS, 1, D_MODEL)
    cos_p, sin_p = _rotary_tables(jnp.arange(seq))
    cos_s, sin_s = _rotary_tables(past_len + jnp.zeros((nb,), I32))

    xp = x_prompt.reshape(n_prompt, D_MODEL)
    qa, ka, va, kab, vab, qr, kr, vr, gr = _inproj(xp, g_attn, w_ext_b, cos_p, sin_p,
                                                   tm=tm, act=BF16)
    a_p = _attn_prompt(qa, kab, vab, lamv, subg, batch=batch, seq=seq, tq=tq)
    r_p, ret_p = _ret_prompt(qr, kr, vr, gr, batch=batch, seq=seq)
    x1_p, hfp_p, ti_p, tg_p = _outproj(a_p, r_p, xp, w_o_b, g_ffn, w_r, b_r, tm=tm)

    xs_ = x_sample.reshape(nb, D_MODEL)
    qa_s, ka_s, va_s, _, _, qr_s, kr_s, vr_s, gr_s = _inproj(xs_, g_attn, w_ext_b, cos_s, sin_s,
                                                             tm=nb, act=F32)
    ck = cache_k.reshape(n_pool, page, DA_WIDTH)
    cv = cache_v.reshape(n_pool, page, DA_WIDTH)
    a_s = _attn_sample(qa_s, ka_s, va_s, ck, cv, page_table, lamv, subg,
                       pages_per_step=pages_per_step)
    r_s, ret_s = _ret_sample(qr_s, kr_s, vr_s, gr_s,
                             state_ret.reshape(nb, RET_HEADS, RET_DK, RET_DV))
    x1_s, hfp_s, ti_s, tg_s = _outproj(a_s, r_s, xs_, w_o_b, g_ffn, w_r, b_r, tm=nb)

    n_tok = n_prompt + nb
    hfp = jnp.concatenate([hfp_p, hfp_s], axis=1)
    top_i = jnp.concatenate([ti_p[:, :TOP_K], ti_s[:, :TOP_K]], axis=0)
    dest, slot_tok, block_e, n_active = _route(top_i, n_tok)
    n_slots = block_e.shape[0] * MOE_TM
    half_unit = SC_UNIT // 2
    slots_pad = -(-n_slots // half_unit) * half_unit
    tok_pad = _pad_to(slot_tok, slots_pad)
    xs_sorted = gather(hfp.reshape(2 * n_tok, QUARTER), jnp.concatenate([tok_pad, n_tok + tok_pad]))
    out_sorted = _moe(block_e, n_active, xs_sorted.reshape(2, slots_pad, QUARTER),
                      w_up_e, b_up_d, w_down_e, b_dn)
    rows_pad = -(-n_tok // half_unit) * half_unit
    dest_k = jnp.pad(dest.T, ((0, 0), (0, rows_pad - n_tok)))
    dest_idx = jnp.concatenate([dest_k, n_slots + dest_k]).reshape(-1)
    yg = gather(out_sorted.reshape(2 * n_slots, QUARTER), dest_idx)
    yg = yg.reshape(2, TOP_K, rows_pad, QUARTER)
    y_p = _combine(yg, tg_p, x1_p, g_fin, tm=tm, row_offset=0)
    y_s = _combine(yg, tg_s, x1_s, g_fin, tm=nb, row_offset=n_prompt)

    depth = 1
    return (y_p.reshape(batch, seq, D_MODEL),
            y_s.reshape(nb, 1, D_MODEL),
            ka.reshape(depth, batch, seq, DA_HEADS, 2, DA_HD),
            va.reshape(depth, batch, seq, DA_HEADS, DA_VD),
            ret_p.reshape(depth, batch, RET_HEADS, RET_DK, RET_DV),
            ka_s.reshape(depth, nb, 1, DA_HEADS, 2, DA_HD),
            va_s.reshape(depth, nb, 1, DA_HEADS, DA_VD),
            ret_s.reshape(depth, nb, RET_HEADS, RET_DK, RET_DV))
```

```python
import functools
import math

import numpy as np
import jax
import jax.numpy as jnp
from jax import lax
from jax.experimental import pallas as pl
from jax.experimental.pallas import tpu as pltpu
from jax.experimental.pallas import tpu_sc as plsc

F32 = jnp.float32
BF16 = jnp.bfloat16
I32 = jnp.int32

D_MODEL = 1024
DA_HEADS = 4
DA_VD = 128
DA_HD = 64
DA_WIDTH = DA_HEADS * DA_VD
RET_HEADS = 4
RET_DV = 128
RET_DK = 64
RET_WIDTH = RET_HEADS * RET_DV
RET_QK_WIDTH = RET_HEADS * RET_DK
RET_CHUNK = 128
ROPE_BASE = 10000.0
N_EXPERTS = 32
TOP_K = 4
D_FF = 1024
SWIGLU_LIMIT = 7.0
SWIGLU_ALPHA = 1.702
NORM_EPS = 1e-5
NEG_INF = -1e30
LAM_INIT = 0.8 - 0.6 * math.exp(-0.3 * 0)
LANES = 128
HALF = D_MODEL // 2
QUARTER = D_MODEL // 4
MOE_TM = 256
PREP_ROWS = 64
SC_WINDOW = 128
SC_SUBCORES = 32
SC_UNIT = SC_WINDOW * SC_SUBCORES
VMEM_LIMIT = 56 * 1024 * 1024

_C_QA, _C_KA, _C_VA = 0, 512, 1024
_C_QR, _C_QRS, _C_KR, _C_KRS = 1536, 1792, 2048, 2304
_C_VR, _C_GR, _C_END = 2560, 3072, 3584


def _mm(a, b):
    return jnp.dot(a.astype(BF16), b.astype(BF16), preferred_element_type=F32)


def _bf(x):
    return x.astype(BF16).astype(F32)


def _rms(x):
    return x * lax.rsqrt(jnp.mean(x * x, axis=-1, keepdims=True) + NORM_EPS)


def _silu(g):
    return g / (1.0 + jnp.exp(-g))


def _pack_pairs(x):
    w = x.shape[-1] // 2
    lo = lax.bitcast_convert_type(x[:, :w].astype(BF16).astype(F32), jnp.uint32)
    hi = lax.bitcast_convert_type(x[:, w:].astype(BF16).astype(F32), jnp.uint32)
    return lax.bitcast_convert_type(hi | (lo >> 16), I32)


def _unpack_pairs(p):
    u = lax.bitcast_convert_type(p, jnp.uint32)
    lo = lax.bitcast_convert_type(u << 16, F32)
    hi = lax.bitcast_convert_type(u & jnp.uint32(0xFFFF0000), F32)
    return lo, hi


def _alibi_slopes():
    return np.asarray([2.0 ** (-8.0 * (h + 1) / DA_HEADS) for h in range(DA_HEADS)], np.float32)


def _inproj_kernel(x_ref, g_ref, w_ref, cos_ref, sin_ref,
                   qa_ref, ka_ref, va_ref, kab_ref, vab_ref,
                   qr_ref, kr_ref, vr_ref, gr_ref):
    hm = (_rms(x_ref[...]) * g_ref[...]).astype(BF16)

    def mm(lo, hi):
        return _mm(hm, w_ref[:, lo:hi])

    qa_ref[...] = mm(_C_QA, _C_KA).astype(qa_ref.dtype)
    ka = mm(_C_KA, _C_VA)
    ka_ref[...] = ka
    kab_ref[...] = ka.astype(BF16)
    va = mm(_C_VA, _C_QR)
    va_ref[...] = va
    vab_ref[...] = va.astype(BF16)
    cos = cos_ref[...]
    sin = sin_ref[...]
    qr_ref[...] = (mm(_C_QR, _C_QRS) * cos + mm(_C_QRS, _C_KR) * sin).astype(qr_ref.dtype)
    kr_ref[...] = (mm(_C_KR, _C_KRS) * cos + mm(_C_KRS, _C_VR) * sin).astype(kr_ref.dtype)
    vr_ref[...] = mm(_C_VR, _C_GR).astype(vr_ref.dtype)
    gr_ref[...] = mm(_C_GR, _C_END).astype(gr_ref.dtype)


def _inproj(x, g, w_ext, cos_t, sin_t, *, tm, act):
    n = x.shape[0]
    n_tab = cos_t.shape[0] // tm
    row = lambda w: pl.BlockSpec((tm, w), lambda i: (i, 0))
    tab = pl.BlockSpec((tm, RET_QK_WIDTH), lambda i: (i % n_tab, 0))
    out_shape = (
        jax.ShapeDtypeStruct((n, DA_WIDTH), act),
        jax.ShapeDtypeStruct((n, DA_WIDTH), F32),
        jax.ShapeDtypeStruct((n, DA_WIDTH), F32),
        jax.ShapeDtypeStruct((n, DA_WIDTH), BF16),
        jax.ShapeDtypeStruct((n, DA_WIDTH), BF16),
        jax.ShapeDtypeStruct((n, RET_QK_WIDTH), act),
        jax.ShapeDtypeStruct((n, RET_QK_WIDTH), act),
        jax.ShapeDtypeStruct((n, RET_WIDTH), act),
        jax.ShapeDtypeStruct((n, RET_WIDTH), act),
    )
    return pl.pallas_call(
        _inproj_kernel,
        grid=(n // tm,),
        in_specs=[row(D_MODEL),
                  pl.BlockSpec((1, D_MODEL), lambda i: (0, 0)),
                  pl.BlockSpec((D_MODEL, _C_END), lambda i: (0, 0)),
                  tab, tab],
        out_specs=(row(DA_WIDTH), row(DA_WIDTH), row(DA_WIDTH), row(DA_WIDTH), row(DA_WIDTH),
                   row(RET_QK_WIDTH), row(RET_QK_WIDTH), row(RET_WIDTH), row(RET_WIDTH)),
        out_shape=out_shape,
        compiler_params=pltpu.CompilerParams(dimension_semantics=("parallel",),
                                             vmem_limit_bytes=VMEM_LIMIT),
        name="inproj",
    )(x, g, w_ext, cos_t, sin_t)


def _lambda_value(lamv_ref):
    lv = lamv_ref[...]
    s1 = jnp.sum(lv[0:1] * lv[1:2], axis=-1, keepdims=True)
    s2 = jnp.sum(lv[2:3] * lv[3:4], axis=-1, keepdims=True)
    return jnp.exp(s1) - jnp.exp(s2) + LAM_INIT


def _attn_prompt_kernel(qi_tab, ki_tab, q_ref, k_ref, v_ref, rel_ref, relm_ref, off_ref,
                        lamv_ref, subg_ref, o_ref, m_sc, l_sc, acc_sc):
    hd = pl.program_id(1)
    t = pl.program_id(2)
    qi = qi_tab[t]
    ki = ki_tab[t]

    @pl.when(ki == 0)
    def _():
        m_sc[...] = jnp.full(m_sc.shape, NEG_INF, F32)
        l_sc[...] = jnp.zeros(l_sc.shape, F32)
        acc_sc[...] = jnp.zeros(acc_sc.shape, F32)

    off = off_ref[hd, t]

    def update(rel):
        q = q_ref[...]
        k = k_ref[...]
        v = v_ref[...]
        for c in range(2):
            s = lax.dot_general(q[:, c * DA_HD:(c + 1) * DA_HD], k[:, c * DA_HD:(c + 1) * DA_HD],
                                (((1,), (1,)), ((), ())), preferred_element_type=F32)
            tt = s + rel
            m_old = m_sc[c]
            m_new = jnp.maximum(m_old, jnp.max(tt, axis=-1, keepdims=True) + off)
            alpha = jnp.exp(m_old - m_new)
            p = jnp.exp(tt + (off - m_new))
            l_sc[c] = alpha * l_sc[c] + jnp.sum(p, axis=-1, keepdims=True)
            acc_sc[c] = alpha * acc_sc[c] + jnp.dot(p.astype(BF16), v, preferred_element_type=F32)
            m_sc[c] = m_new

    @pl.when(ki < qi)
    def _():
        update(rel_ref[0])

    @pl.when(ki == qi)
    def _():
        update(relm_ref[0])
        lam = _lambda_value(lamv_ref)
        o = acc_sc[0] / l_sc[0] - lam * (acc_sc[1] / l_sc[1])
        a = _rms(o) * subg_ref[...] * (1.0 - LAM_INIT)
        o_ref[...] = a.astype(o_ref.dtype)


def _attn_prompt(qa, kab, vab, lamv, subg, *, batch, seq, tq):
    nq = seq // tq
    steps = [(i, j) for i in range(nq) for j in range(i + 1)]
    qi_tab = jnp.asarray([s[0] for s in steps], I32)
    ki_tab = jnp.asarray([s[1] for s in steps], I32)
    slopes = _alibi_slopes()
    dist = (np.arange(tq)[:, None] - np.arange(tq)[None, :]).astype(np.float32)
    rel = -slopes[:, None, None] * dist[None]
    relm = np.where(dist[None] >= 0, rel, np.float32(NEG_INF)).astype(np.float32)
    blk = np.asarray([(s[0] - s[1]) * tq for s in steps], np.float32)
    off = -slopes[:, None] * blk[None, :]
    n = batch * seq
    qspec = pl.BlockSpec((tq, DA_VD), lambda b, h, t, qt, kt: (b * nq + qt[t], h))
    kspec = pl.BlockSpec((tq, DA_VD), lambda b, h, t, qt, kt: (b * nq + kt[t], h))
    relspec = pl.BlockSpec((1, tq, tq), lambda b, h, t, qt, kt: (h, 0, 0))
    grid_spec = pltpu.PrefetchScalarGridSpec(
        num_scalar_prefetch=2,
        grid=(batch, DA_HEADS, len(steps)),
        in_specs=[qspec, kspec, kspec, relspec, relspec,
                  pl.BlockSpec(memory_space=pltpu.SMEM),
                  pl.BlockSpec((4, DA_HD), lambda b, h, t, qt, kt: (0, 0)),
                  pl.BlockSpec((1, DA_VD), lambda b, h, t, qt, kt: (0, 0))],
        out_specs=qspec,
        scratch_shapes=[pltpu.VMEM((2, tq, 1), F32), pltpu.VMEM((2, tq, 1), F32),
                        pltpu.VMEM((2, tq, DA_VD), F32)],
    )
    return pl.pallas_call(
        _attn_prompt_kernel,
        grid_spec=grid_spec,
        out_shape=jax.ShapeDtypeStruct((n, DA_WIDTH), BF16),
        compiler_params=pltpu.CompilerParams(
            dimension_semantics=("parallel", "parallel", "arbitrary"),
            vmem_limit_bytes=VMEM_LIMIT),
        name="attn_prompt",
    )(qi_tab, ki_tab, qa, kab, vab, jnp.asarray(rel), jnp.asarray(relm), jnp.asarray(off),
      lamv, subg)


def _ret_prompt_kernel(q_ref, k_ref, v_ref, g_ref, dec_ref, qdec_ref, kdec_ref, sdec_ref,
                       r_ref, s_ref, *, batch):
    c = pl.program_id(0)

    @pl.when(c == 0)
    def _():
        s_ref[...] = jnp.zeros(s_ref.shape, F32)

    qdec = qdec_ref[...]
    kdec = kdec_ref[...]
    for b in range(batch):
        q = q_ref[b].astype(F32)
        k = k_ref[b].astype(F32)
        qd = (q * qdec).astype(BF16)
        kd = (k * kdec).astype(BF16)
        qb = q_ref[b]
        kb = k_ref[b]
        for h in range(RET_HEADS):
            ks = slice(h * RET_DK, (h + 1) * RET_DK)
            vs = slice(h * RET_DV, (h + 1) * RET_DV)
            vh = v_ref[b, :, vs]
            qk = lax.dot_general(qb[:, ks], kb[:, ks], (((1,), (1,)), ((), ())),
                                 preferred_element_type=F32) * dec_ref[h]
            s_old = s_ref[b, h]
            o = jnp.dot(qk.astype(BF16), vh, preferred_element_type=F32)
            o = o + jnp.dot(qd[:, ks], s_old.astype(BF16), preferred_element_type=F32)
            s_ref[b, h] = sdec_ref[h] * s_old + lax.dot_general(
                kd[:, ks], vh, (((0,), (0,)), ((), ())), preferred_element_type=F32)
            gate = _silu(g_ref[b, :, vs].astype(F32))
            r_ref[b, :, vs] = (_rms(o) * gate).astype(r_ref.dtype)


def _ret_tables(length):
    log_g = jnp.log(1.0 - 2.0 ** (-5.0 - jnp.arange(RET_HEADS, dtype=F32)))
    idx = jnp.arange(length, dtype=F32)
    diff = idx[:, None] - idx[None, :]
    dec = jnp.where(diff >= 0, jnp.exp(jnp.maximum(diff, 0.0)[None] * log_g[:, None, None]), 0.0)
    qdec = jnp.exp((idx + 1.0)[:, None] * log_g[None, :])
    kdec = jnp.exp((length - 1.0 - idx)[:, None] * log_g[None, :])
    sdec = jnp.exp(length * log_g)
    rep = lambda a: jnp.repeat(a, RET_DK, axis=1)
    sdec_b = jnp.broadcast_to(sdec[:, None, None], (RET_HEADS, RET_DK, RET_DV))
    return dec, rep(qdec), rep(kdec), sdec_b


def _ret_prompt(qr, kr, vr, gr, *, batch, seq):
    nc = seq // RET_CHUNK
    dec, qdec, kdec, sdec = _ret_tables(RET_CHUNK)
    q3 = qr.reshape(batch, seq, RET_QK_WIDTH)
    k3 = kr.reshape(batch, seq, RET_QK_WIDTH)
    v3 = vr.reshape(batch, seq, RET_WIDTH)
    g3 = gr.reshape(batch, seq, RET_WIDTH)
    blk = lambda w: pl.BlockSpec((batch, RET_CHUNK, w), lambda c: (0, c, 0))
    const = lambda shape: pl.BlockSpec(shape, lambda c: (0,) * len(shape))
    r, s = pl.pallas_call(
        functools.partial(_ret_prompt_kernel, batch=batch),
        grid=(nc,),
        in_specs=[blk(RET_QK_WIDTH), blk(RET_QK_WIDTH), blk(RET_WIDTH), blk(RET_WIDTH),
                  const((RET_HEADS, RET_CHUNK, RET_CHUNK)),
                  const((RET_CHUNK, RET_QK_WIDTH)), const((RET_CHUNK, RET_QK_WIDTH)),
                  const((RET_HEADS, RET_DK, RET_DV))],
        out_specs=(blk(RET_WIDTH), const((batch, RET_HEADS, RET_DK, RET_DV))),
        out_shape=(jax.ShapeDtypeStruct((batch, seq, RET_WIDTH), BF16),
                   jax.ShapeDtypeStruct((batch, RET_HEADS, RET_DK, RET_DV), F32)),
        compiler_params=pltpu.CompilerParams(dimension_semantics=("arbitrary",),
                                             vmem_limit_bytes=VMEM_LIMIT),
        name="ret_prompt",
    )(q3, k3, v3, g3, dec, qdec, kdec, sdec)
    return r.reshape(batch * seq, RET_WIDTH), s


def _attn_sample_kernel(pt_ref, q_ref, kn_ref, vn_ref, rel_ref, slope_ref, lamv_ref, subg_ref,
                        *rest, pages_per_step, past_len, n_steps):
    g_pages = pages_per_step
    k_refs = rest[:g_pages]
    v_refs = rest[g_pages:2 * g_pages]
    o_ref = rest[2 * g_pages]
    s_sc, wnew_sc, acc_sc = rest[2 * g_pages + 1:]
    j = pl.program_id(1)
    rows = 2 * DA_HEADS

    def q_rows():
        col_group = lax.broadcasted_iota(I32, (rows, DA_WIDTH), 1) // DA_HD
        row_id = lax.broadcasted_iota(I32, (rows, DA_WIDTH), 0)
        want = 2 * (row_id % DA_HEADS) + row_id // DA_HEADS
        qb = jnp.broadcast_to(_bf(q_ref[0]), (rows, DA_WIDTH))
        return jnp.where(col_group == want, qb, 0.0)

    @pl.when(j < n_steps)
    def _():
        qr = q_rows().astype(BF16)
        for g in range(g_pages):
            page = j * g_pages + g
            s = jnp.dot(qr, k_refs[g][0].astype(BF16), preferred_element_type=F32)
            base = (page * LANES - past_len).astype(F32)
            s_sc[page] = s + (rel_ref[...] + slope_ref[...] * base)

    @pl.when(j == n_steps - 1)
    def _():
        s_all = s_sc[...]
        s_new = jnp.sum(q_rows() * _bf(kn_ref[0]), axis=-1, keepdims=True)
        m = jnp.maximum(jnp.max(jnp.max(s_all, axis=0), axis=-1, keepdims=True), s_new)
        p = jnp.exp(s_all - m)
        p_new = jnp.exp(s_new - m)
        l = jnp.sum(jnp.sum(p, axis=0), axis=-1, keepdims=True) + p_new
        wn = p / l
        wn_new = p_new / l
        lam = _lambda_value(lamv_ref)
        s_sc[:, 0:DA_HEADS, :] = wn[:, 0:DA_HEADS, :] - lam * wn[:, DA_HEADS:rows, :]
        s_sc[:, DA_HEADS:rows, :] = jnp.zeros((s_sc.shape[0], DA_HEADS, LANES), F32)
        wnew_sc[...] = wn_new[0:DA_HEADS] - lam * wn_new[DA_HEADS:rows]
        acc_sc[...] = jnp.zeros(acc_sc.shape, F32)

    @pl.when(j >= n_steps)
    def _():
        lane = lax.broadcasted_iota(I32, (rows, LANES), 1)
        own = lane % DA_HEADS == lax.broadcasted_iota(I32, (rows, LANES), 0)
        acc = acc_sc[...]
        for g in range(g_pages):
            page = (j - n_steps) * g_pages + g
            w = s_sc[page]
            w_rows = jnp.concatenate(
                [jnp.where(own, jnp.take_along_axis(w, (LANES // DA_HEADS) * c + lane // DA_HEADS,
                                                    axis=1), 0.0)
                 for c in range(DA_HEADS)], axis=-1)
            acc = acc + jnp.dot(w_rows.astype(BF16), v_refs[g][0].astype(BF16),
                                preferred_element_type=F32)
        acc_sc[...] = acc

    @pl.when(j == 2 * n_steps - 1)
    def _():
        acc = acc_sc[...]
        w_new = _bf(wnew_sc[...])
        v_new = _bf(vn_ref[0])
        outs = []
        for h in range(DA_HEADS):
            vs = slice(h * DA_VD, (h + 1) * DA_VD)
            o = acc[h:h + 1, :] + w_new[h:h + 1, :] * v_new[:, vs]
            outs.append(_rms(o) * subg_ref[...] * (1.0 - LAM_INIT))
        o_ref[0] = jnp.concatenate(outs, axis=-1)


def _attn_sample(q, k_new, v_new, cache_kt, cache_vr, page_table, lamv, subg, *, pages_per_step):
    nb, n_pages = page_table.shape
    page = cache_kt.shape[2]
    assert page == LANES
    past_len = n_pages * page
    g_pages = pages_per_step
    n_steps = n_pages // g_pages
    rows = 2 * DA_HEADS
    slope_rows = np.tile(_alibi_slopes(), 2)[:, None]
    rel = slope_rows * np.arange(page, dtype=np.float32)[None, :]
    pt = page_table.reshape(-1).astype(I32)
    row3 = lambda: pl.BlockSpec((1, 1, DA_WIDTH), lambda b, j, pt: (b, 0, 0))
    const2 = lambda shape: pl.BlockSpec(shape, lambda b, j, pt: (0, 0))

    def k_spec(g):
        return pl.BlockSpec(
            (1, DA_WIDTH, page),
            lambda b, j, pt: (pt[b * n_pages + jnp.minimum(j, n_steps - 1) * g_pages + g], 0, 0))

    def v_spec(g):
        return pl.BlockSpec(
            (1, page * DA_HEADS, DA_VD),
            lambda b, j, pt: (pt[b * n_pages + jnp.maximum(j - n_steps, 0) * g_pages + g], 0, 0))

    grid_spec = pltpu.PrefetchScalarGridSpec(
        num_scalar_prefetch=1,
        grid=(nb, 2 * n_steps),
        in_specs=[row3(), row3(), row3(), const2((rows, page)), const2((rows, 1)),
                  const2((4, DA_HD)), const2((1, DA_VD))]
                 + [k_spec(g) for g in range(g_pages)] + [v_spec(g) for g in range(g_pages)],
        out_specs=row3(),
        scratch_shapes=[pltpu.VMEM((n_pages, rows, page), F32), pltpu.VMEM((DA_HEADS, 1), F32),
                        pltpu.VMEM((rows, DA_VD), F32)],
    )
    out = pl.pallas_call(
        functools.partial(_attn_sample_kernel, pages_per_step=g_pages, past_len=past_len,
                          n_steps=n_steps),
        grid_spec=grid_spec,
        out_shape=jax.ShapeDtypeStruct((nb, 1, DA_WIDTH), F32),
        compiler_params=pltpu.CompilerParams(dimension_semantics=("parallel", "arbitrary"),
                                             vmem_limit_bytes=VMEM_LIMIT),
        name="attn_sample",
    )(pt, q.reshape(nb, 1, DA_WIDTH), k_new.reshape(nb, 1, DA_WIDTH), v_new.reshape(nb, 1, DA_WIDTH),
      jnp.asarray(rel), jnp.asarray(slope_rows), lamv, subg,
      *([cache_kt] * g_pages), *([cache_vr] * g_pages))
    return out.reshape(nb, DA_WIDTH)


def _ret_sample_kernel(q_ref, k_ref, qc_ref, kc_ref, v_ref, g_ref, s_ref, qdec_ref, sdec_ref,
                       r_ref, so_ref):
    q = _bf(q_ref[0])
    k = _bf(k_ref[0])
    qd_col = _bf(qc_ref[0] * qdec_ref[...])
    k_col = _bf(kc_ref[0])
    outs = []
    for h in range(RET_HEADS):
        ks = slice(h * RET_DK, (h + 1) * RET_DK)
        vs = slice(h * RET_DV, (h + 1) * RET_DV)
        vh = _bf(v_ref[0, :, vs])
        s_old = s_ref[0, h]
        qk = jnp.sum(q[:, ks] * k[:, ks], axis=-1, keepdims=True)
        o = _bf(qk) * vh + jnp.sum(qd_col[ks, :] * _bf(s_old), axis=0, keepdims=True)
        so_ref[0, h] = sdec_ref[h] * s_old + k_col[ks, :] * vh
        outs.append(_rms(o) * _silu(g_ref[0, :, vs]))
    r_ref[0] = jnp.concatenate(outs, axis=-1)


def _ret_sample(qr, kr, vr, gr, state):
    nb = qr.shape[0]
    _, qdec, _, sdec = _ret_tables(1)
    row3 = lambda w: pl.BlockSpec((1, 1, w), lambda b: (b, 0, 0))
    col3 = pl.BlockSpec((1, RET_QK_WIDTH, 1), lambda b: (b, 0, 0))
    st = pl.BlockSpec((1, RET_HEADS, RET_DK, RET_DV), lambda b: (b, 0, 0, 0))
    r, s_new = pl.pallas_call(
        _ret_sample_kernel,
        grid=(nb,),
        in_specs=[row3(RET_QK_WIDTH), row3(RET_QK_WIDTH), col3, col3, row3(RET_WIDTH),
                  row3(RET_WIDTH), st,
                  pl.BlockSpec((RET_QK_WIDTH, 1), lambda b: (0, 0)),
                  pl.BlockSpec((RET_HEADS, RET_DK, RET_DV), lambda b: (0, 0, 0))],
        out_specs=(row3(RET_WIDTH), st),
        out_shape=(jax.ShapeDtypeStruct((nb, 1, RET_WIDTH), F32),
                   jax.ShapeDtypeStruct((nb, RET_HEADS, RET_DK, RET_DV), F32)),
        compiler_params=pltpu.CompilerParams(dimension_semantics=("parallel",)),
        name="ret_sample",
    )(qr.reshape(nb, 1, -1), kr.reshape(nb, 1, -1), qr.reshape(nb, -1, 1), kr.reshape(nb, -1, 1),
      vr.reshape(nb, 1, -1), gr.reshape(nb, 1, -1), state, qdec.reshape(-1, 1), sdec)
    return r.reshape(nb, RET_WIDTH), s_new


def _outproj_kernel(a_ref, r_ref, x_ref, wo_ref, g_ref, wr_ref, br_ref,
                    x1_ref, hfp_ref, ti_ref, tg_ref):
    mix = _mm(a_ref[...], wo_ref[:DA_WIDTH, :]) + _mm(r_ref[...], wo_ref[DA_WIDTH:, :])
    x1 = x_ref[...] + mix
    x1_ref[...] = x1
    hf = _rms(x1) * g_ref[...]
    hfp_ref[0] = _pack_pairs(hf[:, :HALF])
    hfp_ref[1] = _pack_pairs(hf[:, HALF:])
    logits = _mm(hf, wr_ref[...]) + br_ref[...]
    lane = lax.broadcasted_iota(I32, logits.shape, 1)
    work = logits
    top_v, top_i = [], []
    for _ in range(TOP_K):
        m = jnp.max(work, axis=-1, keepdims=True)
        idx = jnp.min(jnp.where(work == m, lane, LANES), axis=-1, keepdims=True)
        top_v.append(m)
        top_i.append(idx)
        work = jnp.where(lane == idx, -jnp.inf, work)
    es = [jnp.exp(v - top_v[0]) for v in top_v]
    denom = es[0] + es[1] + es[2] + es[3]
    ti = jnp.zeros(logits.shape, I32)
    tg = jnp.zeros(logits.shape, F32)
    for kk in range(TOP_K):
        ti = jnp.where(lane == kk, top_i[kk], ti)
        tg = jnp.where(lane == kk, es[kk] / denom, tg)
    ti_ref[...] = ti
    tg_ref[...] = tg


def _outproj(a, r, x, w_out, g, w_router, b_router, *, tm):
    n = x.shape[0]
    row = lambda w: pl.BlockSpec((tm, w), lambda i: (i, 0))
    const = lambda shape: pl.BlockSpec(shape, lambda i: (0, 0))
    return pl.pallas_call(
        _outproj_kernel,
        grid=(n // tm,),
        in_specs=[row(DA_WIDTH), row(RET_WIDTH), row(D_MODEL), const((D_MODEL, D_MODEL)),
                  const((1, D_MODEL)), const((D_MODEL, LANES)), const((1, LANES))],
        out_specs=(row(D_MODEL), pl.BlockSpec((2, tm, QUARTER), lambda i: (0, i, 0)),
                   row(LANES), row(LANES)),
        out_shape=(jax.ShapeDtypeStruct((n, D_MODEL), F32),
                   jax.ShapeDtypeStruct((2, n, QUARTER), I32),
                   jax.ShapeDtypeStruct((n, LANES), I32),
                   jax.ShapeDtypeStruct((n, LANES), F32)),
        compiler_params=pltpu.CompilerParams(dimension_semantics=("parallel",),
                                             vmem_limit_bytes=VMEM_LIMIT),
        name="outproj",
    )(a, r, x, w_out, g, w_router, b_router)


def _sc_gather(x, idx):
    m = idx.shape[0]
    cols = x.shape[1]
    mesh = plsc.VectorSubcoreMesh(core_axis_name="c", subcore_axis_name="s")

    @pl.kernel(out_type=jax.ShapeDtypeStruct((m, cols), x.dtype), mesh=mesh)
    def gather_kernel(x_hbm, i_hbm, o_hbm):
        def body(i_vmem, o_vmem):
            pltpu.sync_copy(x_hbm.at[i_vmem.at[0]], o_vmem)

        pltpu.emit_pipeline(
            body,
            grid=(m // SC_WINDOW,),
            in_specs=[pl.BlockSpec((1, SC_WINDOW), lambda i: (0, i))],
            out_specs=[pl.BlockSpec((SC_WINDOW, cols), lambda i: (i, 0))],
            core_axis_name=("c", "s"),
            dimension_semantics=(pltpu.PARALLEL,),
        )(i_hbm, o_hbm)

    return gather_kernel(x, idx.reshape(1, m))


def _filler(n, modulus):
    return jnp.arange(n, dtype=I32) % modulus


def _prep_expert_weights(wup_ref, wdn_ref, wupb, wdnb):
    lane = lax.broadcasted_iota(I32, (PREP_ROWS, LANES), 1)
    first_half = lane < LANES // 2
    idx_even = (2 * lane) % LANES
    idx_odd = (2 * lane + 1) % LANES

    def body(r, carry):
        rows = pl.ds(pl.multiple_of(r * PREP_ROWS, PREP_ROWS), PREP_ROWS)
        for ct in range(D_FF // LANES):
            a = wup_ref[0, rows, 2 * ct * LANES:(2 * ct + 1) * LANES]
            b = wup_ref[0, rows, (2 * ct + 1) * LANES:(2 * ct + 2) * LANES]
            even = jnp.where(first_half, jnp.take_along_axis(a, idx_even, axis=1),
                             jnp.take_along_axis(b, idx_even, axis=1))
            odd = jnp.where(first_half, jnp.take_along_axis(a, idx_odd, axis=1),
                            jnp.take_along_axis(b, idx_odd, axis=1))
            wupb[rows, ct * LANES:(ct + 1) * LANES] = even.astype(BF16)
            wupb[rows, D_FF + ct * LANES:D_FF + (ct + 1) * LANES] = odd.astype(BF16)
        wdnb[rows, :] = wdn_ref[0, rows, :].astype(BF16)
        return carry

    lax.fori_loop(0, D_MODEL // PREP_ROWS, body, 0)


def _moe_kernel(be_ref, na_ref, xs_ref, wup_ref, bup_ref, wdn_ref, bdn_ref, out_ref, wupb, wdnb):
    i = pl.program_id(0)
    n_act = na_ref[0]
    e = be_ref[jnp.minimum(i, n_act - 1)]
    e_prev = be_ref[jnp.maximum(i - 1, 0)]

    @pl.when(jnp.logical_and(i < n_act, jnp.logical_or(i == 0, e != e_prev)))
    def _():
        _prep_expert_weights(wup_ref, wdn_ref, wupb, wdnb)

    @pl.when(i < n_act)
    def _():
        lo0, hi0 = _unpack_pairs(xs_ref[0])
        lo1, hi1 = _unpack_pairs(xs_ref[1])
        x = jnp.concatenate([lo0, hi0, lo1, hi1], axis=-1).astype(BF16)
        u = jnp.dot(x, wupb[...], preferred_element_type=F32) + bup_ref[0]
        g = jnp.minimum(u[:, :D_FF], SWIGLU_LIMIT)
        lin = jnp.clip(u[:, D_FF:], -SWIGLU_LIMIT, SWIGLU_LIMIT)
        y = g * (1.0 / (1.0 + jnp.exp(-SWIGLU_ALPHA * g))) * (lin + 1.0)
        o = jnp.dot(y.astype(BF16), wdnb[...], preferred_element_type=F32) + bdn_ref[0]
        out_ref[0] = _pack_pairs(o[:, :HALF])
        out_ref[1] = _pack_pairs(o[:, HALF:])

    @pl.when(i >= n_act)
    def _():
        out_ref[...] = jnp.zeros(out_ref.shape, I32)


def _moe(block_e, n_active, xs, w_up, b_up_d, w_down, b_down):
    n_blocks = block_e.shape[0]
    n_slots = n_blocks * MOE_TM

    def act(i, na):
        return jnp.minimum(i, na[0] - 1)

    grid_spec = pltpu.PrefetchScalarGridSpec(
        num_scalar_prefetch=2,
        grid=(n_blocks,),
        in_specs=[
            pl.BlockSpec((2, MOE_TM, QUARTER), lambda i, be, na: (0, act(i, na), 0)),
            pl.BlockSpec((1, D_MODEL, 2 * D_FF), lambda i, be, na: (be[act(i, na)], 0, 0)),
            pl.BlockSpec((1, 1, 2 * D_FF), lambda i, be, na: (be[act(i, na)], 0, 0)),
            pl.BlockSpec((1, D_FF, D_MODEL), lambda i, be, na: (be[act(i, na)], 0, 0)),
            pl.BlockSpec((1, 1, D_MODEL), lambda i, be, na: (be[act(i, na)], 0, 0)),
        ],
        out_specs=pl.BlockSpec((2, MOE_TM, QUARTER), lambda i, be, na: (0, i, 0)),
        scratch_shapes=[pltpu.VMEM((D_MODEL, 2 * D_FF), BF16), pltpu.VMEM((D_FF, D_MODEL), BF16)],
    )
    return pl.pallas_call(
        _moe_kernel,
        grid_spec=grid_spec,
        out_shape=jax.ShapeDtypeStruct((2, n_slots, QUARTER), I32),
        compiler_params=pltpu.CompilerParams(dimension_semantics=("arbitrary",),
                                             vmem_limit_bytes=VMEM_LIMIT),
        name="moe_ffn",
    )(block_e, n_active, xs, w_up, b_up_d, w_down, b_down)


def _route(top_i, n_tokens):
    n_assign = n_tokens * TOP_K
    n_blocks = -(-n_assign // MOE_TM) + N_EXPERTS
    n_slots = n_blocks * MOE_TM
    flat_e = top_i.reshape(-1)
    onehot = (flat_e[:, None] == jnp.arange(N_EXPERTS, dtype=I32)[None, :]).astype(I32)
    csum = jnp.cumsum(onehot, axis=0)
    rank = jnp.sum(onehot * csum, axis=1) - 1
    counts = csum[-1]
    padded = (counts + MOE_TM - 1) // MOE_TM * MOE_TM
    pad_end = jnp.cumsum(padded)
    pad_start = pad_end - padded
    dest = (jnp.sum(onehot * pad_start[None, :], axis=1) + rank).astype(I32)
    flat_tok = (jnp.arange(n_assign, dtype=I32) // TOP_K)
    slot_tok = _filler(n_slots, n_tokens).at[dest].set(flat_tok)
    block_e = jnp.minimum(
        jnp.searchsorted(pad_end, jnp.arange(n_blocks, dtype=I32) * MOE_TM, side="right"),
        N_EXPERTS - 1).astype(I32)
    n_active = (pad_end[-1] // MOE_TM).astype(I32).reshape(1)
    return dest.reshape(n_tokens, TOP_K), slot_tok, block_e, n_active


def _combine_kernel(yg_ref, tg_ref, x1_ref, g_ref, o_ref):
    tg = tg_ref[...]
    parts = [jnp.zeros((x1_ref.shape[0], QUARTER), F32) for _ in range(4)]
    for kk in range(TOP_K):
        gate = tg[:, kk:kk + 1]
        lo0, hi0 = _unpack_pairs(yg_ref[0, kk])
        lo1, hi1 = _unpack_pairs(yg_ref[1, kk])
        for j, piece in enumerate((lo0, hi0, lo1, hi1)):
            parts[j] = parts[j] + gate * piece
    x2 = x1_ref[...] + jnp.concatenate(parts, axis=-1)
    o_ref[...] = _rms(x2) * g_ref[...]


def _combine(yg, tg, x1, g, *, tm, row_offset):
    n = x1.shape[0]
    blk0 = row_offset // tm
    return pl.pallas_call(
        _combine_kernel,
        grid=(n // tm,),
        in_specs=[pl.BlockSpec((2, TOP_K, tm, QUARTER), lambda i: (0, 0, i + blk0, 0)),
                  pl.BlockSpec((tm, LANES), lambda i: (i, 0)),
                  pl.BlockSpec((tm, D_MODEL), lambda i: (i, 0)),
                  pl.BlockSpec((1, D_MODEL), lambda i: (0, 0))],
        out_specs=pl.BlockSpec((tm, D_MODEL), lambda i: (i, 0)),
        out_shape=jax.ShapeDtypeStruct((n, D_MODEL), F32),
        compiler_params=pltpu.CompilerParams(dimension_semantics=("parallel",),
                                             vmem_limit_bytes=VMEM_LIMIT),
        name="combine",
    )(yg, tg, x1, g)


def _extend_w_in(w):
    swap = np.arange(RET_QK_WIDTH).reshape(RET_HEADS, 2, RET_DK // 2)[:, ::-1, :].reshape(-1)
    qa, ka, va = w[:, 0:512], w[:, 512:1024], w[:, 1024:1536]
    qr, kr = w[:, 1536:1792], w[:, 1792:2048]
    vr, gr = w[:, 2048:2560], w[:, 2560:3072]
    sa = DA_HD ** -0.5
    sr = RET_DK ** -0.5
    return jnp.concatenate([qa * sa, ka, va, qr, qr[:, swap], kr * sr, kr[:, swap] * sr, vr, gr], axis=1)


def _rotary_tables(pos):
    half = RET_DK // 2
    inv = ROPE_BASE ** (-jnp.arange(half, dtype=F32) / half)
    ang = pos.astype(F32)[:, None] * inv[None, :]
    cos = jnp.cos(ang)
    sin = jnp.sin(ang)
    cos_t = jnp.tile(jnp.concatenate([cos, cos], axis=1), (1, RET_HEADS))
    sin_t = jnp.tile(jnp.concatenate([-sin, sin], axis=1), (1, RET_HEADS))
    return cos_t, sin_t


def kernel(x_prompt, x_sample, cache_k, cache_v, state_ret, page_table, norm_attn_g, w_in, lam_q1,
           lam_k1, lam_q2, lam_k2, da_subln_g, w_out, norm_ffn_g, w_router, b_router, w_up, b_up,
           w_down, b_down, norm_final_g):
    return _forward(x_prompt, x_sample, cache_k, cache_v, state_ret, page_table, norm_attn_g, w_in,
                    lam_q1, lam_k1, lam_q2, lam_k2, da_subln_g, w_out, norm_ffn_g, w_router,
                    b_router, w_up, b_up, w_down, b_down, norm_final_g,
                    tm=512, tq=512, pages_per_step=8, gather=_sc_gather)


def _forward(x_prompt, x_sample, cache_k, cache_v, state_ret, page_table, norm_attn_g, w_in, lam_q1,
             lam_k1, lam_q2, lam_k2, da_subln_g, w_out, norm_ffn_g, w_router, b_router, w_up, b_up,
             w_down, b_down, norm_final_g, *, tm, tq, pages_per_step, gather):
    batch, seq, _ = x_prompt.shape
    nb = x_sample.shape[0]
    n_pool, page = cache_k.shape[1], cache_k.shape[2]
    past_len = page_table.shape[1] * page
    n_prompt = batch * seq

    w_ext_b = _extend_w_in(w_in[0]).astype(BF16)
    g_attn = norm_attn_g[0].reshape(1, D_MODEL)
    g_ffn = norm_ffn_g[0].reshape(1, D_MODEL)
    g_fin = norm_final_g.reshape(1, D_MODEL)
    lamv = jnp.stack([lam_q1[0], lam_k1[0], lam_q2[0], lam_k2[0]]).astype(F32)
    subg = da_subln_g[0].reshape(1, DA_VD)
    w_o_b = w_out[0].astype(BF16)
    w_r = jnp.pad(w_router[0], ((0, 0), (0, LANES - N_EXPERTS))).astype(BF16)
    b_r = jnp.pad(b_router[0], (0, LANES - N_EXPERTS), constant_values=NEG_INF).reshape(1, LANES)
    assert cache_k.shape[0] == 1 and w_up.shape[0] == 1
    w_up_e = w_up.reshape(N_EXPERTS, D_MODEL, 2 * D_FF)
    b_up_d = jnp.concatenate([b_up[0][:, 0::2], b_up[0][:, 1::2]], axis=-1).reshape(N_EXPERTS, 1, 2 * D_FF)
    w_down_e = w_down.reshape(N_EXPERTS, D_FF, D_MODEL)
    b_dn = b_down.reshape(N_EXPERTS, 1, D_MODEL)
    cos_p, sin_p = _rotary_tables(jnp.arange(seq))
    cos_s, sin_s = _rotary_tables(past_len + jnp.zeros((nb,), I32))

    xp = x_prompt.reshape(n_prompt, D_MODEL)
    qa, ka, va, kab, vab, qr, kr, vr, gr = _inproj(xp, g_attn, w_ext_b, cos_p, sin_p,
                                                   tm=tm, act=BF16)
    a_p = _attn_prompt(qa, kab, vab, lamv, subg, batch=batch, seq=seq, tq=tq)
    r_p, ret_p = _ret_prompt(qr, kr, vr, gr, batch=batch, seq=seq)
    x1_p, hfp_p, ti_p, tg_p = _outproj(a_p, r_p, xp, w_o_b, g_ffn, w_r, b_r, tm=tm)

    xs_ = x_sample.reshape(nb, D_MODEL)
    qa_s, ka_s, va_s, _, _, qr_s, kr_s, vr_s, gr_s = _inproj(xs_, g_attn, w_ext_b, cos_s, sin_s,
                                                             tm=nb, act=F32)
    ckt = jnp.transpose(cache_k.reshape(n_pool, page, DA_WIDTH), (0, 2, 1))
    cvr = cache_v.reshape(n_pool, page * DA_HEADS, DA_VD)
    a_s = _attn_sample(qa_s, ka_s, va_s, ckt, cvr, page_table, lamv, subg,
                       pages_per_step=pages_per_step)
    r_s, ret_s = _ret_sample(qr_s, kr_s, vr_s, gr_s,
                             state_ret.reshape(nb, RET_HEADS, RET_DK, RET_DV))
    x1_s, hfp_s, ti_s, tg_s = _outproj(a_s, r_s, xs_, w_o_b, g_ffn, w_r, b_r, tm=nb)

    n_tok = n_prompt + nb
    hfp = jnp.concatenate([hfp_p, hfp_s], axis=1)
    top_i = jnp.concatenate([ti_p[:, :TOP_K], ti_s[:, :TOP_K]], axis=0)
    dest, slot_tok, block_e, n_active = _route(top_i, n_tok)
    n_slots = block_e.shape[0] * MOE_TM
    half_unit = SC_UNIT // 2
    slots_pad = -(-n_slots // half_unit) * half_unit
    tok_pad = jnp.concatenate([slot_tok, _filler(slots_pad - n_slots, n_tok)])
    xs_sorted = gather(hfp.reshape(2 * n_tok, QUARTER), jnp.concatenate([tok_pad, n_tok + tok_pad]))
    out_sorted = _moe(block_e, n_active, xs_sorted.reshape(2, slots_pad, QUARTER),
                      w_up_e, b_up_d, w_down_e, b_dn)
    rows_pad = -(-n_tok // half_unit) * half_unit
    fill = jnp.broadcast_to(_filler(rows_pad - n_tok, n_slots)[None, :], (TOP_K, rows_pad - n_tok))
    dest_k = jnp.concatenate([dest.T, fill], axis=1)
    dest_idx = jnp.concatenate([dest_k, n_slots + dest_k]).reshape(-1)
    yg = gather(out_sorted.reshape(2 * n_slots, QUARTER), dest_idx)
    yg = yg.reshape(2, TOP_K, rows_pad, QUARTER)
    y_p = _combine(yg, tg_p, x1_p, g_fin, tm=tm, row_offset=0)
    y_s = _combine(yg, tg_s, x1_s, g_fin, tm=nb, row_offset=n_prompt)

    depth = 1
    return (y_p.reshape(batch, seq, D_MODEL),
            y_s.reshape(nb, 1, D_MODEL),
            ka.reshape(depth, batch, seq, DA_HEADS, 2, DA_HD),
            va.reshape(depth, batch, seq, DA_HEADS, DA_VD),
            ret_p.reshape(depth, batch, RET_HEADS, RET_DK, RET_DV),
            ka_s.reshape(depth, nb, 1, DA_HEADS, 2, DA_HD),
            va_s.reshape(depth, nb, 1, DA_HEADS, DA_VD),
            ret_s.reshape(depth, nb, RET_HEADS, RET_DK, RET_DV))
```

```python
import functools
import math

import numpy as np
import jax
import jax.numpy as jnp
from jax import lax
from jax.experimental import pallas as pl
from jax.experimental.pallas import tpu as pltpu
from jax.experimental.pallas import tpu_sc as plsc

F32 = jnp.float32
BF16 = jnp.bfloat16
I32 = jnp.int32

D_MODEL = 1024
DA_HEADS = 4
DA_VD = 128
DA_HD = 64
DA_WIDTH = DA_HEADS * DA_VD
RET_HEADS = 4
RET_DV = 128
RET_DK = 64
RET_WIDTH = RET_HEADS * RET_DV
RET_QK_WIDTH = RET_HEADS * RET_DK
RET_CHUNK = 128
ROPE_BASE = 10000.0
N_EXPERTS = 32
TOP_K = 4
D_FF = 1024
SWIGLU_LIMIT = 7.0
SWIGLU_ALPHA = 1.702
NORM_EPS = 1e-5
NEG_INF = -1e30
LAM_INIT = 0.8 - 0.6 * math.exp(-0.3 * 0)
LANES = 128
HALF = D_MODEL // 2
QUARTER = D_MODEL // 4
MOE_TM = 256
PREP_ROWS = 64
ROUTE_TILE = 512
SC_WINDOW = 128
SC_SUBCORES = 32
SC_UNIT = SC_WINDOW * SC_SUBCORES
VMEM_LIMIT = 56 * 1024 * 1024

_C_QA, _C_KA, _C_VA = 0, 512, 1024
_C_QR, _C_QRS, _C_KR, _C_KRS = 1536, 1792, 2048, 2304
_C_VR, _C_GR, _C_END = 2560, 3072, 3584


def _mm(a, b):
    return jnp.dot(a.astype(BF16), b.astype(BF16), preferred_element_type=F32)


def _bf(x):
    return x.astype(BF16).astype(F32)


def _rms(x):
    return x * lax.rsqrt(jnp.mean(x * x, axis=-1, keepdims=True) + NORM_EPS)


def _silu(g):
    return g / (1.0 + jnp.exp(-g))


def _pack_pairs(x):
    w = x.shape[-1] // 2
    lo = lax.bitcast_convert_type(x[:, :w].astype(BF16).astype(F32), jnp.uint32)
    hi = lax.bitcast_convert_type(x[:, w:].astype(BF16).astype(F32), jnp.uint32)
    return lax.bitcast_convert_type(hi | (lo >> 16), I32)


def _unpack_pairs(p):
    u = lax.bitcast_convert_type(p, jnp.uint32)
    lo = lax.bitcast_convert_type(u << 16, F32)
    hi = lax.bitcast_convert_type(u & jnp.uint32(0xFFFF0000), F32)
    return lo, hi


def _alibi_slopes():
    return np.asarray([2.0 ** (-8.0 * (h + 1) / DA_HEADS) for h in range(DA_HEADS)], np.float32)


def _inproj_kernel(x_ref, g_ref, w_ref, cos_ref, sin_ref,
                   qa_ref, ka_ref, va_ref, kab_ref, vab_ref,
                   qr_ref, kr_ref, vr_ref, gr_ref):
    hm = (_rms(x_ref[...]) * g_ref[...]).astype(BF16)

    def mm(lo, hi):
        return _mm(hm, w_ref[:, lo:hi])

    qa_ref[...] = mm(_C_QA, _C_KA).astype(qa_ref.dtype)
    ka = mm(_C_KA, _C_VA)
    ka_ref[...] = ka
    kab_ref[...] = ka.astype(BF16)
    va = mm(_C_VA, _C_QR)
    va_ref[...] = va
    vab_ref[...] = va.astype(BF16)
    cos = cos_ref[...]
    sin = sin_ref[...]
    qr_ref[...] = (mm(_C_QR, _C_QRS) * cos + mm(_C_QRS, _C_KR) * sin).astype(qr_ref.dtype)
    kr_ref[...] = (mm(_C_KR, _C_KRS) * cos + mm(_C_KRS, _C_VR) * sin).astype(kr_ref.dtype)
    vr_ref[...] = mm(_C_VR, _C_GR).astype(vr_ref.dtype)
    gr_ref[...] = mm(_C_GR, _C_END).astype(gr_ref.dtype)


def _inproj(x, g, w_ext, cos_t, sin_t, *, tm, act):
    n = x.shape[0]
    n_tab = cos_t.shape[0] // tm
    row = lambda w: pl.BlockSpec((tm, w), lambda i: (i, 0))
    tab = pl.BlockSpec((tm, RET_QK_WIDTH), lambda i: (i % n_tab, 0))
    out_shape = (
        jax.ShapeDtypeStruct((n, DA_WIDTH), act),
        jax.ShapeDtypeStruct((n, DA_WIDTH), F32),
        jax.ShapeDtypeStruct((n, DA_WIDTH), F32),
        jax.ShapeDtypeStruct((n, DA_WIDTH), BF16),
        jax.ShapeDtypeStruct((n, DA_WIDTH), BF16),
        jax.ShapeDtypeStruct((n, RET_QK_WIDTH), act),
        jax.ShapeDtypeStruct((n, RET_QK_WIDTH), act),
        jax.ShapeDtypeStruct((n, RET_WIDTH), act),
        jax.ShapeDtypeStruct((n, RET_WIDTH), act),
    )
    return pl.pallas_call(
        _inproj_kernel,
        grid=(n // tm,),
        in_specs=[row(D_MODEL),
                  pl.BlockSpec((1, D_MODEL), lambda i: (0, 0)),
                  pl.BlockSpec((D_MODEL, _C_END), lambda i: (0, 0)),
                  tab, tab],
        out_specs=(row(DA_WIDTH), row(DA_WIDTH), row(DA_WIDTH), row(DA_WIDTH), row(DA_WIDTH),
                   row(RET_QK_WIDTH), row(RET_QK_WIDTH), row(RET_WIDTH), row(RET_WIDTH)),
        out_shape=out_shape,
        compiler_params=pltpu.CompilerParams(dimension_semantics=("parallel",),
                                             vmem_limit_bytes=VMEM_LIMIT),
        name="inproj",
    )(x, g, w_ext, cos_t, sin_t)


def _lambda_value(lamv_ref):
    lv = lamv_ref[...]
    s1 = jnp.sum(lv[0:1] * lv[1:2], axis=-1, keepdims=True)
    s2 = jnp.sum(lv[2:3] * lv[3:4], axis=-1, keepdims=True)
    return jnp.exp(s1) - jnp.exp(s2) + LAM_INIT


def _attn_prompt_kernel(qi_tab, ki_tab, q_ref, k_ref, v_ref, rel_ref, relm_ref, off_ref,
                        lamv_ref, subg_ref, o_ref, m_sc, acc_sc):
    hd = pl.program_id(1)
    t = pl.program_id(2)
    qi = qi_tab[t]
    ki = ki_tab[t]

    @pl.when(ki == 0)
    def _():
        m_sc[...] = jnp.full(m_sc.shape, NEG_INF, F32)
        acc_sc[...] = jnp.zeros(acc_sc.shape, F32)

    off = off_ref[hd, t]
    tk = k_ref.shape[0]

    def update(rel):
        q = q_ref[...]
        k = k_ref[...]
        v_ext = jnp.concatenate([v_ref[...], jnp.ones((tk, LANES), BF16)], axis=-1)
        for c in range(2):
            s = lax.dot_general(q[:, c * DA_HD:(c + 1) * DA_HD], k[:, c * DA_HD:(c + 1) * DA_HD],
                                (((1,), (1,)), ((), ())), preferred_element_type=F32)
            tt = s + rel
            mx = tt[:, 0:LANES]
            for jj in range(1, tk // LANES):
                mx = jnp.maximum(mx, tt[:, jj * LANES:(jj + 1) * LANES])
            m_old = m_sc[c]
            m_new = jnp.maximum(m_old, jnp.max(mx, axis=-1, keepdims=True) + off)
            alpha = jnp.exp(m_old - m_new)
            p = jnp.exp(tt + pltpu.repeat(off - m_new, tk // LANES, axis=1))
            acc_sc[c] = (pltpu.repeat(alpha, 2, axis=1) * acc_sc[c]
                         + jnp.dot(p.astype(BF16), v_ext, preferred_element_type=F32))
            m_sc[c] = m_new

    @pl.when(ki < qi)
    def _():
        update(rel_ref[0])

    @pl.when(ki == qi)
    def _():
        update(relm_ref[0])
        lam = _lambda_value(lamv_ref)
        o = (acc_sc[0, :, :DA_VD] / acc_sc[0, :, DA_VD:]
             - lam * (acc_sc[1, :, :DA_VD] / acc_sc[1, :, DA_VD:]))
        a = _rms(o) * subg_ref[...] * (1.0 - LAM_INIT)
        o_ref[...] = a.astype(o_ref.dtype)


def _attn_prompt(qa, kab, vab, lamv, subg, *, batch, seq, tq):
    nq = seq // tq
    steps = [(i, j) for i in range(nq) for j in range(i + 1)]
    qi_tab = jnp.asarray([s[0] for s in steps], I32)
    ki_tab = jnp.asarray([s[1] for s in steps], I32)
    slopes = _alibi_slopes()
    dist = (np.arange(tq)[:, None] - np.arange(tq)[None, :]).astype(np.float32)
    rel = -slopes[:, None, None] * dist[None]
    relm = np.where(dist[None] >= 0, rel, np.float32(NEG_INF)).astype(np.float32)
    blk = np.asarray([(s[0] - s[1]) * tq for s in steps], np.float32)
    off = -slopes[:, None] * blk[None, :]
    n = batch * seq
    qspec = pl.BlockSpec((tq, DA_VD), lambda b, h, t, qt, kt: (b * nq + qt[t], h))
    kspec = pl.BlockSpec((tq, DA_VD), lambda b, h, t, qt, kt: (b * nq + kt[t], h))
    relspec = pl.BlockSpec((1, tq, tq), lambda b, h, t, qt, kt: (h, 0, 0))
    grid_spec = pltpu.PrefetchScalarGridSpec(
        num_scalar_prefetch=2,
        grid=(batch, DA_HEADS, len(steps)),
        in_specs=[qspec, kspec, kspec, relspec, relspec,
                  pl.BlockSpec(memory_space=pltpu.SMEM),
                  pl.BlockSpec((4, DA_HD), lambda b, h, t, qt, kt: (0, 0)),
                  pl.BlockSpec((1, DA_VD), lambda b, h, t, qt, kt: (0, 0))],
        out_specs=qspec,
        scratch_shapes=[pltpu.VMEM((2, tq, LANES), F32), pltpu.VMEM((2, tq, DA_VD + LANES), F32)],
    )
    return pl.pallas_call(
        _attn_prompt_kernel,
        grid_spec=grid_spec,
        out_shape=jax.ShapeDtypeStruct((n, DA_WIDTH), BF16),
        compiler_params=pltpu.CompilerParams(
            dimension_semantics=("parallel", "parallel", "arbitrary"),
            vmem_limit_bytes=VMEM_LIMIT),
        name="attn_prompt",
    )(qi_tab, ki_tab, qa, kab, vab, jnp.asarray(rel), jnp.asarray(relm), jnp.asarray(off),
      lamv, subg)


def _ret_prompt_kernel(q_ref, k_ref, v_ref, g_ref, dec_ref, qdec_ref, kdec_ref, sdec_ref,
                       r_ref, s_ref, *, batch):
    c = pl.program_id(0)

    @pl.when(c == 0)
    def _():
        s_ref[...] = jnp.zeros(s_ref.shape, F32)

    qdec = qdec_ref[...]
    kdec = kdec_ref[...]
    for b in range(batch):
        q = q_ref[b].astype(F32)
        k = k_ref[b].astype(F32)
        qd = (q * qdec).astype(BF16)
        kd = (k * kdec).astype(BF16)
        qb = q_ref[b]
        kb = k_ref[b]
        for h in range(RET_HEADS):
            ks = slice(h * RET_DK, (h + 1) * RET_DK)
            vs = slice(h * RET_DV, (h + 1) * RET_DV)
            vh = v_ref[b, :, vs]
            qk = lax.dot_general(qb[:, ks], kb[:, ks], (((1,), (1,)), ((), ())),
                                 preferred_element_type=F32) * dec_ref[h]
            s_old = s_ref[b, h]
            o = jnp.dot(qk.astype(BF16), vh, preferred_element_type=F32)
            o = o + jnp.dot(qd[:, ks], s_old.astype(BF16), preferred_element_type=F32)
            s_ref[b, h] = sdec_ref[h] * s_old + lax.dot_general(
                kd[:, ks], vh, (((0,), (0,)), ((), ())), preferred_element_type=F32)
            gate = _silu(g_ref[b, :, vs].astype(F32))
            r_ref[b, :, vs] = (_rms(o) * gate).astype(r_ref.dtype)


def _ret_tables(length):
    log_g = jnp.log(1.0 - 2.0 ** (-5.0 - jnp.arange(RET_HEADS, dtype=F32)))
    idx = jnp.arange(length, dtype=F32)
    diff = idx[:, None] - idx[None, :]
    dec = jnp.where(diff >= 0, jnp.exp(jnp.maximum(diff, 0.0)[None] * log_g[:, None, None]), 0.0)
    qdec = jnp.exp((idx + 1.0)[:, None] * log_g[None, :])
    kdec = jnp.exp((length - 1.0 - idx)[:, None] * log_g[None, :])
    sdec = jnp.exp(length * log_g)
    rep = lambda a: jnp.repeat(a, RET_DK, axis=1)
    sdec_b = jnp.broadcast_to(sdec[:, None, None], (RET_HEADS, RET_DK, RET_DV))
    return dec, rep(qdec), rep(kdec), sdec_b


def _ret_prompt(qr, kr, vr, gr, *, batch, seq):
    nc = seq // RET_CHUNK
    dec, qdec, kdec, sdec = _ret_tables(RET_CHUNK)
    q3 = qr.reshape(batch, seq, RET_QK_WIDTH)
    k3 = kr.reshape(batch, seq, RET_QK_WIDTH)
    v3 = vr.reshape(batch, seq, RET_WIDTH)
    g3 = gr.reshape(batch, seq, RET_WIDTH)
    blk = lambda w: pl.BlockSpec((batch, RET_CHUNK, w), lambda c: (0, c, 0))
    const = lambda shape: pl.BlockSpec(shape, lambda c: (0,) * len(shape))
    r, s = pl.pallas_call(
        functools.partial(_ret_prompt_kernel, batch=batch),
        grid=(nc,),
        in_specs=[blk(RET_QK_WIDTH), blk(RET_QK_WIDTH), blk(RET_WIDTH), blk(RET_WIDTH),
                  const((RET_HEADS, RET_CHUNK, RET_CHUNK)),
                  const((RET_CHUNK, RET_QK_WIDTH)), const((RET_CHUNK, RET_QK_WIDTH)),
                  const((RET_HEADS, RET_DK, RET_DV))],
        out_specs=(blk(RET_WIDTH), const((batch, RET_HEADS, RET_DK, RET_DV))),
        out_shape=(jax.ShapeDtypeStruct((batch, seq, RET_WIDTH), BF16),
                   jax.ShapeDtypeStruct((batch, RET_HEADS, RET_DK, RET_DV), F32)),
        compiler_params=pltpu.CompilerParams(dimension_semantics=("arbitrary",),
                                             vmem_limit_bytes=VMEM_LIMIT),
        name="ret_prompt",
    )(q3, k3, v3, g3, dec, qdec, kdec, sdec)
    return r.reshape(batch * seq, RET_WIDTH), s


def _attn_sample_kernel(pt_ref, q_ref, kn_ref, vn_ref, rel_ref, slope_ref, lamv_ref, subg_ref,
                        *rest, pages_per_step, past_len, n_steps):
    g_pages = pages_per_step
    k_refs = rest[:g_pages]
    v_refs = rest[g_pages:2 * g_pages]
    o_ref = rest[2 * g_pages]
    s_sc, wnew_sc, acc_sc = rest[2 * g_pages + 1:]
    j = pl.program_id(1)
    rows = 2 * DA_HEADS

    def q_rows():
        col_group = lax.broadcasted_iota(I32, (rows, DA_WIDTH), 1) // DA_HD
        row_id = lax.broadcasted_iota(I32, (rows, DA_WIDTH), 0)
        want = 2 * (row_id % DA_HEADS) + row_id // DA_HEADS
        qb = jnp.broadcast_to(q_ref[0], (rows, DA_WIDTH))
        return jnp.where(col_group == want, qb, 0.0)

    @pl.when(j < n_steps)
    def _():
        qr = q_rows().astype(BF16)
        for g in range(g_pages):
            page = j * g_pages + g
            s = jnp.dot(qr, k_refs[g][0].astype(BF16), preferred_element_type=F32)
            base = (page * LANES - past_len).astype(F32)
            s_sc[page] = s + (rel_ref[...] + slope_ref[...] * base)

    @pl.when(j == n_steps - 1)
    def _():
        s_all = s_sc[...]
        s_new = jnp.sum(q_rows() * kn_ref[0], axis=-1, keepdims=True)
        m = jnp.maximum(jnp.max(jnp.max(s_all, axis=0), axis=-1, keepdims=True), s_new)
        p = jnp.exp(s_all - m)
        p_new = jnp.exp(s_new - m)
        l = jnp.sum(jnp.sum(p, axis=0), axis=-1, keepdims=True) + p_new
        wn = p / l
        wn_new = p_new / l
        lam = _lambda_value(lamv_ref)
        s_sc[:, 0:DA_HEADS, :] = wn[:, 0:DA_HEADS, :] - lam * wn[:, DA_HEADS:rows, :]
        s_sc[:, DA_HEADS:rows, :] = jnp.zeros((s_sc.shape[0], DA_HEADS, LANES), F32)
        wnew_sc[...] = wn_new[0:DA_HEADS] - lam * wn_new[DA_HEADS:rows]
        acc_sc[...] = jnp.zeros(acc_sc.shape, F32)

    @pl.when(j >= n_steps)
    def _():
        lane = lax.broadcasted_iota(I32, (rows, LANES), 1)
        own = lane % DA_HEADS == lax.broadcasted_iota(I32, (rows, LANES), 0)
        acc = acc_sc[...]
        for g in range(g_pages):
            page = (j - n_steps) * g_pages + g
            w = s_sc[page]
            w_rows = jnp.concatenate(
                [jnp.where(own, jnp.take_along_axis(w, (LANES // DA_HEADS) * c + lane // DA_HEADS,
                                                    axis=1), 0.0)
                 for c in range(DA_HEADS)], axis=-1)
            acc = acc + jnp.dot(w_rows.astype(BF16), v_refs[g][0].astype(BF16),
                                preferred_element_type=F32)
        acc_sc[...] = acc

    @pl.when(j == 2 * n_steps - 1)
    def _():
        acc = acc_sc[...]
        w_new = wnew_sc[...]
        v_new = vn_ref[0]
        outs = []
        for h in range(DA_HEADS):
            vs = slice(h * DA_VD, (h + 1) * DA_VD)
            o = acc[h:h + 1, :] + w_new[h:h + 1, :] * v_new[:, vs]
            outs.append(_rms(o) * subg_ref[...] * (1.0 - LAM_INIT))
        o_ref[0] = jnp.concatenate(outs, axis=-1)


def _attn_sample(q, k_new, v_new, cache_kt, cache_vr, page_table, lamv, subg, *, pages_per_step):
    nb, n_pages = page_table.shape
    page = cache_kt.shape[2]
    assert page == LANES
    past_len = n_pages * page
    g_pages = pages_per_step
    n_steps = n_pages // g_pages
    rows = 2 * DA_HEADS
    slope_rows = np.tile(_alibi_slopes(), 2)[:, None]
    rel = slope_rows * np.arange(page, dtype=np.float32)[None, :]
    pt = page_table.reshape(-1).astype(I32)
    row3 = lambda: pl.BlockSpec((1, 1, DA_WIDTH), lambda b, j, pt: (b, 0, 0))
    const2 = lambda shape: pl.BlockSpec(shape, lambda b, j, pt: (0, 0))

    def k_spec(g):
        return pl.BlockSpec(
            (1, DA_WIDTH, page),
            lambda b, j, pt: (pt[b * n_pages + jnp.minimum(j, n_steps - 1) * g_pages + g], 0, 0))

    def v_spec(g):
        return pl.BlockSpec(
            (1, page * DA_HEADS, DA_VD),
            lambda b, j, pt: (pt[b * n_pages + jnp.maximum(j - n_steps, 0) * g_pages + g], 0, 0))

    grid_spec = pltpu.PrefetchScalarGridSpec(
        num_scalar_prefetch=1,
        grid=(nb, 2 * n_steps),
        in_specs=[row3(), row3(), row3(), const2((rows, page)), const2((rows, 1)),
                  const2((4, DA_HD)), const2((1, DA_VD))]
                 + [k_spec(g) for g in range(g_pages)] + [v_spec(g) for g in range(g_pages)],
        out_specs=row3(),
        scratch_shapes=[pltpu.VMEM((n_pages, rows, page), F32), pltpu.VMEM((DA_HEADS, 1), F32),
                        pltpu.VMEM((rows, DA_VD), F32)],
    )
    out = pl.pallas_call(
        functools.partial(_attn_sample_kernel, pages_per_step=g_pages, past_len=past_len,
                          n_steps=n_steps),
        grid_spec=grid_spec,
        out_shape=jax.ShapeDtypeStruct((nb, 1, DA_WIDTH), F32),
        compiler_params=pltpu.CompilerParams(dimension_semantics=("parallel", "arbitrary"),
                                             vmem_limit_bytes=VMEM_LIMIT),
        name="attn_sample",
    )(pt, q.reshape(nb, 1, DA_WIDTH), k_new.reshape(nb, 1, DA_WIDTH), v_new.reshape(nb, 1, DA_WIDTH),
      jnp.asarray(rel), jnp.asarray(slope_rows), lamv, subg,
      *([cache_kt] * g_pages), *([cache_vr] * g_pages))
    return out.reshape(nb, DA_WIDTH)


def _ret_sample_kernel(q_ref, k_ref, qc_ref, kc_ref, v_ref, g_ref, s_ref, qdec_ref, sdec_ref,
                       r_ref, so_ref):
    q = q_ref[0]
    k = k_ref[0]
    qd_col = _bf(qc_ref[0] * qdec_ref[...])
    k_col = kc_ref[0]
    outs = []
    for h in range(RET_HEADS):
        ks = slice(h * RET_DK, (h + 1) * RET_DK)
        vs = slice(h * RET_DV, (h + 1) * RET_DV)
        vh = v_ref[0, :, vs]
        s_old = s_ref[0, h]
        qk = jnp.sum(q[:, ks] * k[:, ks], axis=-1, keepdims=True)
        o = qk * vh + jnp.sum(qd_col[ks, :] * _bf(s_old), axis=0, keepdims=True)
        so_ref[0, h] = sdec_ref[h] * s_old + k_col[ks, :] * vh
        outs.append(_rms(o) * _silu(g_ref[0, :, vs]))
    r_ref[0] = jnp.concatenate(outs, axis=-1)


def _ret_sample(qr, kr, vr, gr, state):
    nb = qr.shape[0]
    _, qdec, _, sdec = _ret_tables(1)
    row3 = lambda w: pl.BlockSpec((1, 1, w), lambda b: (b, 0, 0))
    col3 = pl.BlockSpec((1, RET_QK_WIDTH, 1), lambda b: (b, 0, 0))
    st = pl.BlockSpec((1, RET_HEADS, RET_DK, RET_DV), lambda b: (b, 0, 0, 0))
    r, s_new = pl.pallas_call(
        _ret_sample_kernel,
        grid=(nb,),
        in_specs=[row3(RET_QK_WIDTH), row3(RET_QK_WIDTH), col3, col3, row3(RET_WIDTH),
                  row3(RET_WIDTH), st,
                  pl.BlockSpec((RET_QK_WIDTH, 1), lambda b: (0, 0)),
                  pl.BlockSpec((RET_HEADS, RET_DK, RET_DV), lambda b: (0, 0, 0))],
        out_specs=(row3(RET_WIDTH), st),
        out_shape=(jax.ShapeDtypeStruct((nb, 1, RET_WIDTH), F32),
                   jax.ShapeDtypeStruct((nb, RET_HEADS, RET_DK, RET_DV), F32)),
        compiler_params=pltpu.CompilerParams(dimension_semantics=("parallel",)),
        name="ret_sample",
    )(qr.reshape(nb, 1, -1), kr.reshape(nb, 1, -1), qr.reshape(nb, -1, 1), kr.reshape(nb, -1, 1),
      vr.reshape(nb, 1, -1), gr.reshape(nb, 1, -1), state, qdec.reshape(-1, 1), sdec)
    return r.reshape(nb, RET_WIDTH), s_new


def _outproj_kernel(a_ref, r_ref, x_ref, wo_ref, g_ref, wr_ref, br_ref,
                    x1_ref, hfp_ref, ti_ref, tg_ref):
    mix = _mm(a_ref[...], wo_ref[:DA_WIDTH, :]) + _mm(r_ref[...], wo_ref[DA_WIDTH:, :])
    x1 = x_ref[...] + mix
    x1_ref[...] = x1
    hf = _rms(x1) * g_ref[...]
    hfp_ref[0] = _pack_pairs(hf[:, :HALF])
    hfp_ref[1] = _pack_pairs(hf[:, HALF:])
    logits = _mm(hf, wr_ref[...]) + br_ref[...]
    lane = lax.broadcasted_iota(I32, logits.shape, 1)
    work = logits
    top_v, top_i = [], []
    for _ in range(TOP_K):
        m = jnp.max(work, axis=-1, keepdims=True)
        idx = jnp.min(jnp.where(work == m, lane, LANES), axis=-1, keepdims=True)
        top_v.append(m)
        top_i.append(idx)
        work = jnp.where(lane == idx, -jnp.inf, work)
    es = [jnp.exp(v - top_v[0]) for v in top_v]
    denom = es[0] + es[1] + es[2] + es[3]
    ti = jnp.zeros(logits.shape, I32)
    tg = jnp.zeros(logits.shape, F32)
    for kk in range(TOP_K):
        ti = jnp.where(lane == kk, top_i[kk], ti)
        tg = jnp.where(lane == kk, es[kk] / denom, tg)
    ti_ref[...] = ti
    tg_ref[...] = tg


def _outproj(a, r, x, w_out, g, w_router, b_router, *, tm):
    n = x.shape[0]
    row = lambda w: pl.BlockSpec((tm, w), lambda i: (i, 0))
    const = lambda shape: pl.BlockSpec(shape, lambda i: (0, 0))
    return pl.pallas_call(
        _outproj_kernel,
        grid=(n // tm,),
        in_specs=[row(DA_WIDTH), row(RET_WIDTH), row(D_MODEL), const((D_MODEL, D_MODEL)),
                  const((1, D_MODEL)), const((D_MODEL, LANES)), const((1, LANES))],
        out_specs=(row(D_MODEL), pl.BlockSpec((2, tm, QUARTER), lambda i: (0, i, 0)),
                   row(LANES), row(LANES)),
        out_shape=(jax.ShapeDtypeStruct((n, D_MODEL), F32),
                   jax.ShapeDtypeStruct((2, n, QUARTER), I32),
                   jax.ShapeDtypeStruct((n, LANES), I32),
                   jax.ShapeDtypeStruct((n, LANES), F32)),
        compiler_params=pltpu.CompilerParams(dimension_semantics=("parallel",),
                                             vmem_limit_bytes=VMEM_LIMIT),
        name="outproj",
    )(a, r, x, w_out, g, w_router, b_router)


def _sc_gather(x, idx):
    m = idx.shape[0]
    cols = x.shape[1]
    mesh = plsc.VectorSubcoreMesh(core_axis_name="c", subcore_axis_name="s")

    @pl.kernel(out_type=jax.ShapeDtypeStruct((m, cols), x.dtype), mesh=mesh)
    def gather_kernel(x_hbm, i_hbm, o_hbm):
        def body(i_vmem, o_vmem):
            pltpu.sync_copy(x_hbm.at[i_vmem.at[0]], o_vmem)

        pltpu.emit_pipeline(
            body,
            grid=(m // SC_WINDOW,),
            in_specs=[pl.BlockSpec((1, SC_WINDOW), lambda i: (0, i))],
            out_specs=[pl.BlockSpec((SC_WINDOW, cols), lambda i: (i, 0))],
            core_axis_name=("c", "s"),
            dimension_semantics=(pltpu.PARALLEL,),
        )(i_hbm, o_hbm)

    return gather_kernel(x, idx.reshape(1, m))


def _filler(n, modulus):
    return jnp.arange(n, dtype=I32) % modulus


def _sc_scatter(x, idx, out_rows):
    m = idx.shape[0]
    cols = x.shape[1]
    n_blk = x.shape[0] // 2 // SC_WINDOW
    mesh = plsc.VectorSubcoreMesh(core_axis_name="c", subcore_axis_name="s")

    @pl.kernel(out_type=jax.ShapeDtypeStruct((out_rows, cols), x.dtype), mesh=mesh)
    def scatter_kernel(x_hbm, i_hbm, o_hbm):
        def body(x_vmem, i_vmem):
            pltpu.sync_copy(x_vmem, o_hbm.at[i_vmem.at[0]])

        pltpu.emit_pipeline(
            body,
            grid=(m // SC_WINDOW,),
            in_specs=[pl.BlockSpec((SC_WINDOW, cols),
                                   lambda i: ((i // (TOP_K * n_blk)) * n_blk + i % n_blk, 0)),
                      pl.BlockSpec((1, SC_WINDOW), lambda i: (0, i))],
            out_specs=[],
            core_axis_name=("c", "s"),
            dimension_semantics=(pltpu.PARALLEL,),
        )(x_hbm, i_hbm)

    return scatter_kernel(x, idx.reshape(1, m))


def _rank_kernel(ti_ref, tri_ref, rank_ref, cnt_ref, base_sc):
    @pl.when(pl.program_id(0) == 0)
    def _():
        base_sc[...] = jnp.zeros(base_sc.shape, F32)

    ti = ti_ref[...]
    lane = lax.broadcasted_iota(I32, ti.shape, 1)
    picks = [lane == ti[:, k:k + 1] for k in range(TOP_K)]
    onehot = jnp.zeros(ti.shape, F32)
    for pk in picks:
        onehot = onehot + jnp.where(pk, 1.0, 0.0)
    pos = jnp.dot(tri_ref[...], onehot.astype(BF16), preferred_element_type=F32) + base_sc[...]
    rank = jnp.zeros(ti.shape, I32)
    for k, pk in enumerate(picks):
        r_k = jnp.sum(jnp.where(pk, pos, 0.0), axis=-1, keepdims=True)
        rank = jnp.where(lane == k, r_k.astype(I32), rank)
    rank_ref[...] = rank
    base_sc[...] = base_sc[...] + jnp.sum(onehot, axis=0, keepdims=True)
    cnt_ref[...] = base_sc[...]


def _rank(ti, *, tr):
    n = ti.shape[0]
    tri = jnp.asarray(np.tril(np.ones((tr, tr), np.float32), -1), BF16)
    return pl.pallas_call(
        _rank_kernel,
        grid=(n // tr,),
        in_specs=[pl.BlockSpec((tr, LANES), lambda i: (i, 0)),
                  pl.BlockSpec((tr, tr), lambda i: (0, 0))],
        out_specs=(pl.BlockSpec((tr, LANES), lambda i: (i, 0)),
                   pl.BlockSpec((1, LANES), lambda i: (0, 0))),
        out_shape=(jax.ShapeDtypeStruct((n, LANES), I32), jax.ShapeDtypeStruct((1, LANES), F32)),
        scratch_shapes=[pltpu.VMEM((1, LANES), F32)],
        compiler_params=pltpu.CompilerParams(dimension_semantics=("arbitrary",)),
        name="route_rank",
    )(ti, tri)


def _prep_expert_weights(wup_ref, wdn_ref, wupb, wdnb):
    lane = lax.broadcasted_iota(I32, (PREP_ROWS, LANES), 1)
    first_half = lane < LANES // 2
    idx_even = (2 * lane) % LANES
    idx_odd = (2 * lane + 1) % LANES

    def body(r, carry):
        rows = pl.ds(pl.multiple_of(r * PREP_ROWS, PREP_ROWS), PREP_ROWS)
        for ct in range(D_FF // LANES):
            a = wup_ref[0, rows, 2 * ct * LANES:(2 * ct + 1) * LANES]
            b = wup_ref[0, rows, (2 * ct + 1) * LANES:(2 * ct + 2) * LANES]
            even = jnp.where(first_half, jnp.take_along_axis(a, idx_even, axis=1),
                             jnp.take_along_axis(b, idx_even, axis=1))
            odd = jnp.where(first_half, jnp.take_along_axis(a, idx_odd, axis=1),
                            jnp.take_along_axis(b, idx_odd, axis=1))
            wupb[rows, ct * LANES:(ct + 1) * LANES] = even.astype(BF16)
            wupb[rows, D_FF + ct * LANES:D_FF + (ct + 1) * LANES] = odd.astype(BF16)
        wdnb[rows, :] = wdn_ref[0, rows, :].astype(BF16)
        return carry

    lax.fori_loop(0, D_MODEL // PREP_ROWS, body, 0)


def _moe_kernel(be_ref, na_ref, nv_ref, xs_ref, wup_ref, bup_ref, wdn_ref, bdn_ref, out_ref,
                wupb, wdnb):
    i = pl.program_id(0)
    n_act = na_ref[0]
    e = be_ref[jnp.minimum(i, n_act - 1)]
    e_prev = be_ref[jnp.maximum(i - 1, 0)]

    @pl.when(jnp.logical_and(i < n_act, jnp.logical_or(i == 0, e != e_prev)))
    def _():
        _prep_expert_weights(wup_ref, wdn_ref, wupb, wdnb)

    @pl.when(i < n_act)
    def _():
        lo0, hi0 = _unpack_pairs(xs_ref[0])
        lo1, hi1 = _unpack_pairs(xs_ref[1])
        x = jnp.concatenate([lo0, hi0, lo1, hi1], axis=-1)
        live = lax.broadcasted_iota(I32, x.shape, 0) < nv_ref[i]
        x = jnp.where(live, x, 0.0).astype(BF16)
        u = jnp.dot(x, wupb[...], preferred_element_type=F32) + bup_ref[0]
        g = jnp.minimum(u[:, :D_FF], SWIGLU_LIMIT)
        lin = jnp.clip(u[:, D_FF:], -SWIGLU_LIMIT, SWIGLU_LIMIT)
        y = g * (1.0 / (1.0 + jnp.exp(-SWIGLU_ALPHA * g))) * (lin + 1.0)
        o = jnp.dot(y.astype(BF16), wdnb[...], preferred_element_type=F32) + bdn_ref[0]
        out_ref[0] = _pack_pairs(o[:, :HALF])
        out_ref[1] = _pack_pairs(o[:, HALF:])

    @pl.when(i >= n_act)
    def _():
        out_ref[...] = jnp.zeros(out_ref.shape, I32)


def _moe(block_e, n_active, n_valid, xs, w_up, b_up_d, w_down, b_down):
    n_blocks = block_e.shape[0]
    n_slots = n_blocks * MOE_TM

    def act(i, na):
        return jnp.minimum(i, na[0] - 1)

    grid_spec = pltpu.PrefetchScalarGridSpec(
        num_scalar_prefetch=3,
        grid=(n_blocks,),
        in_specs=[
            pl.BlockSpec((2, MOE_TM, QUARTER), lambda i, be, na, nv: (0, act(i, na), 0)),
            pl.BlockSpec((1, D_MODEL, 2 * D_FF), lambda i, be, na, nv: (be[act(i, na)], 0, 0)),
            pl.BlockSpec((1, 1, 2 * D_FF), lambda i, be, na, nv: (be[act(i, na)], 0, 0)),
            pl.BlockSpec((1, D_FF, D_MODEL), lambda i, be, na, nv: (be[act(i, na)], 0, 0)),
            pl.BlockSpec((1, 1, D_MODEL), lambda i, be, na, nv: (be[act(i, na)], 0, 0)),
        ],
        out_specs=pl.BlockSpec((2, MOE_TM, QUARTER), lambda i, be, na, nv: (0, i, 0)),
        scratch_shapes=[pltpu.VMEM((D_MODEL, 2 * D_FF), BF16), pltpu.VMEM((D_FF, D_MODEL), BF16)],
    )
    return pl.pallas_call(
        _moe_kernel,
        grid_spec=grid_spec,
        out_shape=jax.ShapeDtypeStruct((2, n_slots, QUARTER), I32),
        compiler_params=pltpu.CompilerParams(dimension_semantics=("arbitrary",),
                                             vmem_limit_bytes=VMEM_LIMIT),
        name="moe_ffn",
    )(block_e, n_active, n_valid, xs, w_up, b_up_d, w_down, b_down)


def _slot_layout(counts, top_i, rank, n_blocks):
    experts = jnp.arange(N_EXPERTS, dtype=I32)
    padded = (counts + MOE_TM - 1) // MOE_TM * MOE_TM
    pad_end = jnp.cumsum(padded)
    pad_start = pad_end - padded
    pick = top_i[:, :, None] == experts[None, None, :]
    dest = rank + jnp.sum(jnp.where(pick, pad_start[None, None, :], 0), axis=-1)
    blk_start = jnp.arange(n_blocks, dtype=I32) * MOE_TM
    block_e = jnp.minimum(jnp.sum((blk_start[:, None] >= pad_end[None, :]).astype(I32), axis=1),
                          N_EXPERTS - 1)
    live = jnp.clip(counts[block_e] - (blk_start - pad_start[block_e]), 0, MOE_TM)
    n_active = (pad_end[-1] // MOE_TM).astype(I32).reshape(1)
    return dest.astype(I32), block_e.astype(I32), live.astype(I32), n_active


def _combine_kernel(yg_ref, tg_ref, x1_ref, g_ref, o_ref):
    tg = tg_ref[...]
    parts = [jnp.zeros((x1_ref.shape[0], QUARTER), F32) for _ in range(4)]
    for kk in range(TOP_K):
        gate = tg[:, kk:kk + 1]
        lo0, hi0 = _unpack_pairs(yg_ref[0, kk])
        lo1, hi1 = _unpack_pairs(yg_ref[1, kk])
        for j, piece in enumerate((lo0, hi0, lo1, hi1)):
            parts[j] = parts[j] + gate * piece
    x2 = x1_ref[...] + jnp.concatenate(parts, axis=-1)
    o_ref[...] = _rms(x2) * g_ref[...]


def _combine(yg, tg, x1, g, *, tm, row_offset):
    n = x1.shape[0]
    blk0 = row_offset // tm
    return pl.pallas_call(
        _combine_kernel,
        grid=(n // tm,),
        in_specs=[pl.BlockSpec((2, TOP_K, tm, QUARTER), lambda i: (0, 0, i + blk0, 0)),
                  pl.BlockSpec((tm, LANES), lambda i: (i, 0)),
                  pl.BlockSpec((tm, D_MODEL), lambda i: (i, 0)),
                  pl.BlockSpec((1, D_MODEL), lambda i: (0, 0))],
        out_specs=pl.BlockSpec((tm, D_MODEL), lambda i: (i, 0)),
        out_shape=jax.ShapeDtypeStruct((n, D_MODEL), F32),
        compiler_params=pltpu.CompilerParams(dimension_semantics=("parallel",),
                                             vmem_limit_bytes=VMEM_LIMIT),
        name="combine",
    )(yg, tg, x1, g)


def _extend_w_in(w):
    swap = np.arange(RET_QK_WIDTH).reshape(RET_HEADS, 2, RET_DK // 2)[:, ::-1, :].reshape(-1)
    qa, ka, va = w[:, 0:512], w[:, 512:1024], w[:, 1024:1536]
    qr, kr = w[:, 1536:1792], w[:, 1792:2048]
    vr, gr = w[:, 2048:2560], w[:, 2560:3072]
    sa = DA_HD ** -0.5
    sr = RET_DK ** -0.5
    return jnp.concatenate([qa * sa, ka, va, qr, qr[:, swap], kr * sr, kr[:, swap] * sr, vr, gr], axis=1)


def _rotary_tables(pos):
    half = RET_DK // 2
    inv = ROPE_BASE ** (-jnp.arange(half, dtype=F32) / half)
    ang = pos.astype(F32)[:, None] * inv[None, :]
    cos = jnp.cos(ang)
    sin = jnp.sin(ang)
    cos_t = jnp.tile(jnp.concatenate([cos, cos], axis=1), (1, RET_HEADS))
    sin_t = jnp.tile(jnp.concatenate([-sin, sin], axis=1), (1, RET_HEADS))
    return cos_t, sin_t


def kernel(x_prompt, x_sample, cache_k, cache_v, state_ret, page_table, norm_attn_g, w_in, lam_q1,
           lam_k1, lam_q2, lam_k2, da_subln_g, w_out, norm_ffn_g, w_router, b_router, w_up, b_up,
           w_down, b_down, norm_final_g):
    return _forward(x_prompt, x_sample, cache_k, cache_v, state_ret, page_table, norm_attn_g, w_in,
                    lam_q1, lam_k1, lam_q2, lam_k2, da_subln_g, w_out, norm_ffn_g, w_router,
                    b_router, w_up, b_up, w_down, b_down, norm_final_g,
                    tm=512, tq=512, pages_per_step=32, gather=_sc_gather, scatter=_sc_scatter)


def _forward(x_prompt, x_sample, cache_k, cache_v, state_ret, page_table, norm_attn_g, w_in, lam_q1,
             lam_k1, lam_q2, lam_k2, da_subln_g, w_out, norm_ffn_g, w_router, b_router, w_up, b_up,
             w_down, b_down, norm_final_g, *, tm, tq, pages_per_step, gather, scatter):
    batch, seq, _ = x_prompt.shape
    nb = x_sample.shape[0]
    n_pool, page = cache_k.shape[1], cache_k.shape[2]
    past_len = page_table.shape[1] * page
    n_prompt = batch * seq

    w_ext_b = _extend_w_in(w_in[0]).astype(BF16)
    g_attn = norm_attn_g[0].reshape(1, D_MODEL)
    g_ffn = norm_ffn_g[0].reshape(1, D_MODEL)
    g_fin = norm_final_g.reshape(1, D_MODEL)
    lamv = jnp.stack([lam_q1[0], lam_k1[0], lam_q2[0], lam_k2[0]]).astype(F32)
    subg = da_subln_g[0].reshape(1, DA_VD)
    w_o_b = w_out[0].astype(BF16)
    w_r = jnp.pad(w_router[0], ((0, 0), (0, LANES - N_EXPERTS))).astype(BF16)
    b_r = jnp.pad(b_router[0], (0, LANES - N_EXPERTS), constant_values=NEG_INF).reshape(1, LANES)
    assert cache_k.shape[0] == 1 and w_up.shape[0] == 1
    w_up_e = w_up.reshape(N_EXPERTS, D_MODEL, 2 * D_FF)
    b_up_d = jnp.concatenate([b_up[0][:, 0::2], b_up[0][:, 1::2]], axis=-1).reshape(N_EXPERTS, 1, 2 * D_FF)
    w_down_e = w_down.reshape(N_EXPERTS, D_FF, D_MODEL)
    b_dn = b_down.reshape(N_EXPERTS, 1, D_MODEL)
    cos_p, sin_p = _rotary_tables(jnp.arange(seq))
    cos_s, sin_s = _rotary_tables(past_len + jnp.zeros((nb,), I32))

    xp = x_prompt.reshape(n_prompt, D_MODEL)
    qa, ka, va, kab, vab, qr, kr, vr, gr = _inproj(xp, g_attn, w_ext_b, cos_p, sin_p,
                                                   tm=tm, act=BF16)
    a_p = _attn_prompt(qa, kab, vab, lamv, subg, batch=batch, seq=seq, tq=tq)
    r_p, ret_p = _ret_prompt(qr, kr, vr, gr, batch=batch, seq=seq)
    x1_p, hfp_p, ti_p, tg_p = _outproj(a_p, r_p, xp, w_o_b, g_ffn, w_r, b_r, tm=tm)

    xs_ = x_sample.reshape(nb, D_MODEL)
    qa_s, ka_s, va_s, _, _, qr_s, kr_s, vr_s, gr_s = _inproj(xs_, g_attn, w_ext_b, cos_s, sin_s,
                                                             tm=nb, act=F32)
    ckt = jnp.transpose(cache_k.reshape(n_pool, page, DA_WIDTH), (0, 2, 1))
    cvr = cache_v.reshape(n_pool, page * DA_HEADS, DA_VD)
    a_s = _attn_sample(qa_s, ka_s, va_s, ckt, cvr, page_table, lamv, subg,
                       pages_per_step=pages_per_step)
    r_s, ret_s = _ret_sample(qr_s, kr_s, vr_s, gr_s,
                             state_ret.reshape(nb, RET_HEADS, RET_DK, RET_DV))
    x1_s, hfp_s, ti_s, tg_s = _outproj(a_s, r_s, xs_, w_o_b, g_ffn, w_r, b_r, tm=nb)

    n_tok = n_prompt + nb
    tok_pad = -(-n_tok // ROUTE_TILE) * ROUTE_TILE
    pad = tok_pad - n_tok
    hfp = jnp.concatenate([hfp_p, hfp_s, jnp.zeros((2, pad, QUARTER), I32)], axis=1)
    ti_all = jnp.concatenate([ti_p, ti_s, jnp.full((pad, LANES), LANES - 1, I32)], axis=0)
    rank, cnt = _rank(ti_all, tr=ROUTE_TILE)
    counts = cnt[0, :N_EXPERTS].astype(I32)
    n_blocks = -(-n_tok * TOP_K // MOE_TM) + N_EXPERTS
    n_slots = n_blocks * MOE_TM
    dest, block_e, n_live, n_active = _slot_layout(counts, ti_all[:n_tok, :TOP_K],
                                                   rank[:n_tok, :TOP_K], n_blocks)
    slots_pad = -(-(n_slots + 1) // (SC_UNIT // 2)) * (SC_UNIT // 2)
    dump = n_slots + _filler(pad, slots_pad - n_slots)
    dest_sc = jnp.concatenate([dest, jnp.broadcast_to(dump[:, None], (pad, TOP_K))], axis=0).T
    xs_sorted = scatter(hfp.reshape(2 * tok_pad, QUARTER),
                        jnp.concatenate([dest_sc, slots_pad + dest_sc]).reshape(-1), 2 * slots_pad)
    out_sorted = _moe(block_e, n_active, n_live, xs_sorted.reshape(2, slots_pad, QUARTER),
                      w_up_e, b_up_d, w_down_e, b_dn)
    fill = jnp.broadcast_to(_filler(pad, n_slots)[:, None], (pad, TOP_K))
    dest_g = jnp.concatenate([dest, fill], axis=0).T
    yg = gather(out_sorted.reshape(2 * n_slots, QUARTER),
                jnp.concatenate([dest_g, n_slots + dest_g]).reshape(-1))
    yg = yg.reshape(2, TOP_K, tok_pad, QUARTER)
    y_p = _combine(yg, tg_p, x1_p, g_fin, tm=tm, row_offset=0)
    y_s = _combine(yg, tg_s, x1_s, g_fin, tm=nb, row_offset=n_prompt)

    depth = 1
    return (y_p.reshape(batch, seq, D_MODEL),
            y_s.reshape(nb, 1, D_MODEL),
            ka.reshape(depth, batch, seq, DA_HEADS, 2, DA_HD),
            va.reshape(depth, batch, seq, DA_HEADS, DA_VD),
            ret_p.reshape(depth, batch, RET_HEADS, RET_DK, RET_DV),
            ka_s.reshape(depth, nb, 1, DA_HEADS, 2, DA_HD),
            va_s.reshape(depth, nb, 1, DA_HEADS, DA_VD),
            ret_s.reshape(depth, nb, RET_HEADS, RET_DK, RET_DV))
```

```python
import functools
import math

import numpy as np
import jax
import jax.numpy as jnp
from jax import lax
from jax.experimental import pallas as pl
from jax.experimental.pallas import tpu as pltpu
from jax.experimental.pallas import tpu_sc as plsc

F32 = jnp.float32
BF16 = jnp.bfloat16
I32 = jnp.int32

D_MODEL = 1024
DA_HEADS = 4
DA_VD = 128
DA_HD = 64
DA_WIDTH = DA_HEADS * DA_VD
RET_HEADS = 4
RET_DV = 128
RET_DK = 64
RET_WIDTH = RET_HEADS * RET_DV
RET_QK_WIDTH = RET_HEADS * RET_DK
RET_CHUNK = 128
ROPE_BASE = 10000.0
N_EXPERTS = 32
TOP_K = 4
D_FF = 1024
SWIGLU_LIMIT = 7.0
SWIGLU_ALPHA = 1.702
NORM_EPS = 1e-5
NEG_INF = -1e30
LAM_INIT = 0.8 - 0.6 * math.exp(-0.3 * 0)
LANES = 128
HALF = D_MODEL // 2
QUARTER = D_MODEL // 4
MOE_TM = 256
PREP_ROWS = 64
ROUTE_TILE = 512
SC_WINDOW = 128
SC_SUBCORES = 32
SC_UNIT = SC_WINDOW * SC_SUBCORES
VMEM_LIMIT = 56 * 1024 * 1024

_C_QA, _C_KA, _C_VA = 0, 512, 1024
_C_QR, _C_QRS, _C_KR, _C_KRS = 1536, 1792, 2048, 2304
_C_VR, _C_GR, _C_END = 2560, 3072, 3584


def _mm(a, b):
    return jnp.dot(a.astype(BF16), b.astype(BF16), preferred_element_type=F32)


def _bf(x):
    return x.astype(BF16).astype(F32)


def _rms(x):
    return x * lax.rsqrt(jnp.mean(x * x, axis=-1, keepdims=True) + NORM_EPS)


def _silu(g):
    return g / (1.0 + jnp.exp(-g))


def _pack_pairs(x):
    w = x.shape[-1] // 2
    lo = lax.bitcast_convert_type(x[:, :w].astype(BF16).astype(F32), jnp.uint32)
    hi = lax.bitcast_convert_type(x[:, w:].astype(BF16).astype(F32), jnp.uint32)
    return lax.bitcast_convert_type(hi | (lo >> 16), I32)


def _unpack_pairs(p):
    u = lax.bitcast_convert_type(p, jnp.uint32)
    lo = lax.bitcast_convert_type(u << 16, F32)
    hi = lax.bitcast_convert_type(u & jnp.uint32(0xFFFF0000), F32)
    return lo, hi


def _alibi_slopes():
    return np.asarray([2.0 ** (-8.0 * (h + 1) / DA_HEADS) for h in range(DA_HEADS)], np.float32)


def _inproj_kernel(x_ref, g_ref, w_ref, wt_ref, cos_ref, sin_ref, *outs, transposed):
    hm = (_rms(x_ref[...]) * g_ref[...]).astype(BF16)

    def mm(lo, hi):
        return _mm(hm, w_ref[:, lo:hi])

    def mm_t(lo, hi):
        return lax.dot_general(wt_ref[lo:hi, :], hm, (((1,), (1,)), ((), ())),
                               preferred_element_type=F32)

    if transposed:
        qat_ref, kat_ref, kab_ref, va_ref, vat_ref, qr_ref, kr_ref, vr_ref, gr_ref = outs
        qat_ref[0] = mm_t(_C_QA, _C_KA).astype(BF16)
        kat_ref[0] = mm_t(_C_KA, _C_VA)
        kab_ref[...] = mm(_C_KA, _C_VA).astype(BF16)
        va_ref[...] = mm(_C_VA, _C_QR)
        vat_ref[0] = mm_t(_C_VA, _C_QR).astype(BF16)
    else:
        qa_ref, ka_ref, va_ref, qr_ref, kr_ref, vr_ref, gr_ref = outs
        qa_ref[...] = mm(_C_QA, _C_KA)
        ka_ref[...] = mm(_C_KA, _C_VA)
        va_ref[...] = mm(_C_VA, _C_QR)
    cos = cos_ref[...]
    sin = sin_ref[...]
    qr_ref[...] = (mm(_C_QR, _C_QRS) * cos + mm(_C_QRS, _C_KR) * sin).astype(qr_ref.dtype)
    kr_ref[...] = (mm(_C_KR, _C_KRS) * cos + mm(_C_KRS, _C_VR) * sin).astype(kr_ref.dtype)
    vr_ref[...] = mm(_C_VR, _C_GR).astype(vr_ref.dtype)
    gr_ref[...] = mm(_C_GR, _C_END).astype(gr_ref.dtype)


def _inproj(x, g, w_ext, w_ext_t, cos_t, sin_t, *, tm, seq=None):
    n = x.shape[0]
    n_tab = cos_t.shape[0] // tm
    transposed = seq is not None
    act = BF16 if transposed else F32
    row = lambda w: pl.BlockSpec((tm, w), lambda i: (i, 0))
    tab = pl.BlockSpec((tm, RET_QK_WIDTH), lambda i: (i % n_tab, 0))
    sds = jax.ShapeDtypeStruct
    tail_shapes = (sds((n, RET_QK_WIDTH), act), sds((n, RET_QK_WIDTH), act),
                   sds((n, RET_WIDTH), act), sds((n, RET_WIDTH), act))
    tail_specs = (row(RET_QK_WIDTH), row(RET_QK_WIDTH), row(RET_WIDTH), row(RET_WIDTH))
    if transposed:
        n_s = seq // tm
        tr = pl.BlockSpec((1, DA_WIDTH, tm), lambda i: (i // n_s, 0, i % n_s))
        head_shapes = (sds((n // seq, DA_WIDTH, seq), BF16), sds((n // seq, DA_WIDTH, seq), F32),
                       sds((n, DA_WIDTH), BF16), sds((n, DA_WIDTH), F32),
                       sds((n // seq, DA_WIDTH, seq), BF16))
        head_specs = (tr, tr, row(DA_WIDTH), row(DA_WIDTH), tr)
    else:
        head_shapes = (sds((n, DA_WIDTH), F32),) * 3
        head_specs = (row(DA_WIDTH),) * 3
    return pl.pallas_call(
        functools.partial(_inproj_kernel, transposed=transposed),
        grid=(n // tm,),
        in_specs=[row(D_MODEL),
                  pl.BlockSpec((1, D_MODEL), lambda i: (0, 0)),
                  pl.BlockSpec((D_MODEL, _C_END), lambda i: (0, 0)),
                  pl.BlockSpec((_C_QR, D_MODEL), lambda i: (0, 0)),
                  tab, tab],
        out_specs=head_specs + tail_specs,
        out_shape=head_shapes + tail_shapes,
        compiler_params=pltpu.CompilerParams(dimension_semantics=("parallel",),
                                             vmem_limit_bytes=VMEM_LIMIT),
        name="inproj",
    )(x, g, w_ext, w_ext_t, cos_t, sin_t)


def _lambda_value(lamv_ref):
    lv = lamv_ref[...]
    s1 = jnp.sum(lv[0:1] * lv[1:2], axis=-1, keepdims=True)
    s2 = jnp.sum(lv[2:3] * lv[3:4], axis=-1, keepdims=True)
    return jnp.exp(s1) - jnp.exp(s2) + LAM_INIT


def _attn_prompt_kernel(qi_tab, ki_tab, qt_ref, k_ref, vt_ref, rel_ref, relm_ref, off_ref,
                        lamv_ref, subg_ref, o_ref, m_sc, l_sc, acc_sc):
    hd = pl.program_id(1)
    t = pl.program_id(2)
    qi = qi_tab[t]
    ki = ki_tab[t]

    @pl.when(ki == 0)
    def _():
        m_sc[...] = jnp.full(m_sc.shape, NEG_INF, F32)
        l_sc[...] = jnp.zeros(l_sc.shape, F32)
        acc_sc[...] = jnp.zeros(acc_sc.shape, F32)

    off = off_ref[hd, t]

    def update(rel):
        qt = qt_ref[0]
        k = k_ref[...]
        vt = vt_ref[0]
        row = lax.broadcasted_iota(I32, qt.shape, 0)
        for c in range(2):
            qc = jnp.where((row >= c * DA_HD) & (row < (c + 1) * DA_HD), qt, jnp.zeros_like(qt))
            tt = jnp.dot(k, qc, preferred_element_type=F32) + rel
            m_old = m_sc[c]
            m_new = jnp.maximum(m_old, jnp.max(tt, axis=0, keepdims=True) + off)
            alpha = jnp.exp(m_old - m_new)
            p = jnp.exp(tt + (off - m_new))
            l_sc[c] = alpha * l_sc[c] + jnp.sum(p, axis=0, keepdims=True)
            acc_sc[c] = alpha * acc_sc[c] + jnp.dot(vt, p.astype(BF16), preferred_element_type=F32)
            m_sc[c] = m_new

    @pl.when(ki < qi)
    def _():
        update(rel_ref[0])

    @pl.when(ki == qi)
    def _():
        update(relm_ref[0])
        lam = _lambda_value(lamv_ref)
        o = acc_sc[0] / l_sc[0] - lam * (acc_sc[1] / l_sc[1])
        ms = jnp.mean(o * o, axis=0, keepdims=True)
        a = o * lax.rsqrt(ms + NORM_EPS) * subg_ref[...] * (1.0 - LAM_INIT)
        o_ref[0] = a.astype(o_ref.dtype)


def _attn_prompt(qat, kab, vat, lamv, subg, *, batch, seq, tq):
    nq = seq // tq
    steps = [(i, j) for i in range(nq) for j in range(i + 1)]
    qi_tab = jnp.asarray([s[0] for s in steps], I32)
    ki_tab = jnp.asarray([s[1] for s in steps], I32)
    slopes = _alibi_slopes()
    dist = (np.arange(tq)[None, :] - np.arange(tq)[:, None]).astype(np.float32)
    rel = -slopes[:, None, None] * dist[None]
    relm = np.where(dist[None] >= 0, rel, np.float32(NEG_INF)).astype(np.float32)
    blk = np.asarray([(s[0] - s[1]) * tq for s in steps], np.float32)
    off = -slopes[:, None] * blk[None, :]
    qspec = pl.BlockSpec((1, DA_VD, tq), lambda b, h, t, qt, kt: (b, h, qt[t]))
    kspec = pl.BlockSpec((tq, DA_VD), lambda b, h, t, qt, kt: (b * nq + kt[t], h))
    vspec = pl.BlockSpec((1, DA_VD, tq), lambda b, h, t, qt, kt: (b, h, kt[t]))
    relspec = pl.BlockSpec((1, tq, tq), lambda b, h, t, qt, kt: (h, 0, 0))
    grid_spec = pltpu.PrefetchScalarGridSpec(
        num_scalar_prefetch=2,
        grid=(batch, DA_HEADS, len(steps)),
        in_specs=[qspec, kspec, vspec, relspec, relspec,
                  pl.BlockSpec(memory_space=pltpu.SMEM),
                  pl.BlockSpec((4, DA_HD), lambda b, h, t, qt, kt: (0, 0)),
                  pl.BlockSpec((DA_VD, 1), lambda b, h, t, qt, kt: (0, 0))],
        out_specs=qspec,
        scratch_shapes=[pltpu.VMEM((2, 1, tq), F32), pltpu.VMEM((2, 1, tq), F32),
                        pltpu.VMEM((2, DA_VD, tq), F32)],
    )
    return pl.pallas_call(
        _attn_prompt_kernel,
        grid_spec=grid_spec,
        out_shape=jax.ShapeDtypeStruct((batch, DA_WIDTH, seq), BF16),
        compiler_params=pltpu.CompilerParams(
            dimension_semantics=("parallel", "parallel", "arbitrary"),
            vmem_limit_bytes=VMEM_LIMIT),
        name="attn_prompt",
    )(qi_tab, ki_tab, qat, kab, vat, jnp.asarray(rel), jnp.asarray(relm), jnp.asarray(off),
      lamv, subg.reshape(DA_VD, 1))


def _ret_prompt_kernel(q_ref, k_ref, v_ref, g_ref, dec_ref, qdec_ref, kdec_ref, sdec_ref,
                       r_ref, s_ref, *, batch):
    c = pl.program_id(0)

    @pl.when(c == 0)
    def _():
        s_ref[...] = jnp.zeros(s_ref.shape, F32)

    qdec = qdec_ref[...]
    kdec = kdec_ref[...]
    for b in range(batch):
        q = q_ref[b].astype(F32)
        k = k_ref[b].astype(F32)
        qd = (q * qdec).astype(BF16)
        kd = (k * kdec).astype(BF16)
        qb = q_ref[b]
        kb = k_ref[b]
        for h in range(RET_HEADS):
            ks = slice(h * RET_DK, (h + 1) * RET_DK)
            vs = slice(h * RET_DV, (h + 1) * RET_DV)
            vh = v_ref[b, :, vs]
            qk = lax.dot_general(qb[:, ks], kb[:, ks], (((1,), (1,)), ((), ())),
                                 preferred_element_type=F32) * dec_ref[h]
            s_old = s_ref[b, h]
            o = jnp.dot(qk.astype(BF16), vh, preferred_element_type=F32)
            o = o + jnp.dot(qd[:, ks], s_old.astype(BF16), preferred_element_type=F32)
            s_ref[b, h] = sdec_ref[h] * s_old + lax.dot_general(
                kd[:, ks], vh, (((0,), (0,)), ((), ())), preferred_element_type=F32)
            gate = _silu(g_ref[b, :, vs].astype(F32))
            r_ref[b, :, vs] = (_rms(o) * gate).astype(r_ref.dtype)


def _ret_tables(length):
    log_g = jnp.log(1.0 - 2.0 ** (-5.0 - jnp.arange(RET_HEADS, dtype=F32)))
    idx = jnp.arange(length, dtype=F32)
    diff = idx[:, None] - idx[None, :]
    dec = jnp.where(diff >= 0, jnp.exp(jnp.maximum(diff, 0.0)[None] * log_g[:, None, None]), 0.0)
    qdec = jnp.exp((idx + 1.0)[:, None] * log_g[None, :])
    kdec = jnp.exp((length - 1.0 - idx)[:, None] * log_g[None, :])
    sdec = jnp.exp(length * log_g)
    rep = lambda a: jnp.repeat(a, RET_DK, axis=1)
    sdec_b = jnp.broadcast_to(sdec[:, None, None], (RET_HEADS, RET_DK, RET_DV))
    return dec, rep(qdec), rep(kdec), sdec_b


def _ret_prompt(qr, kr, vr, gr, *, batch, seq):
    nc = seq // RET_CHUNK
    dec, qdec, kdec, sdec = _ret_tables(RET_CHUNK)
    q3 = qr.reshape(batch, seq, RET_QK_WIDTH)
    k3 = kr.reshape(batch, seq, RET_QK_WIDTH)
    v3 = vr.reshape(batch, seq, RET_WIDTH)
    g3 = gr.reshape(batch, seq, RET_WIDTH)
    blk = lambda w: pl.BlockSpec((batch, RET_CHUNK, w), lambda c: (0, c, 0))
    const = lambda shape: pl.BlockSpec(shape, lambda c: (0,) * len(shape))
    r, s = pl.pallas_call(
        functools.partial(_ret_prompt_kernel, batch=batch),
        grid=(nc,),
        in_specs=[blk(RET_QK_WIDTH), blk(RET_QK_WIDTH), blk(RET_WIDTH), blk(RET_WIDTH),
                  const((RET_HEADS, RET_CHUNK, RET_CHUNK)),
                  const((RET_CHUNK, RET_QK_WIDTH)), const((RET_CHUNK, RET_QK_WIDTH)),
                  const((RET_HEADS, RET_DK, RET_DV))],
        out_specs=(blk(RET_WIDTH), const((batch, RET_HEADS, RET_DK, RET_DV))),
        out_shape=(jax.ShapeDtypeStruct((batch, seq, RET_WIDTH), BF16),
                   jax.ShapeDtypeStruct((batch, RET_HEADS, RET_DK, RET_DV), F32)),
        compiler_params=pltpu.CompilerParams(dimension_semantics=("arbitrary",),
                                             vmem_limit_bytes=VMEM_LIMIT),
        name="ret_prompt",
    )(q3, k3, v3, g3, dec, qdec, kdec, sdec)
    return r.reshape(batch * seq, RET_WIDTH), s


def _attn_sample_kernel(pt_ref, q_ref, kn_ref, vn_ref, rel_ref, slope_ref, lamv_ref, subg_ref,
                        *rest, pages_per_step, past_len, n_steps):
    g_pages = pages_per_step
    k_refs = rest[:g_pages]
    v_refs = rest[g_pages:2 * g_pages]
    o_ref = rest[2 * g_pages]
    s_sc, wnew_sc, acc_sc = rest[2 * g_pages + 1:]
    j = pl.program_id(1)
    rows = 2 * DA_HEADS

    def q_rows():
        col_group = lax.broadcasted_iota(I32, (rows, DA_WIDTH), 1) // DA_HD
        row_id = lax.broadcasted_iota(I32, (rows, DA_WIDTH), 0)
        want = 2 * (row_id % DA_HEADS) + row_id // DA_HEADS
        qb = jnp.broadcast_to(q_ref[0], (rows, DA_WIDTH))
        return jnp.where(col_group == want, qb, 0.0)

    @pl.when(j < n_steps)
    def _():
        qr = q_rows().astype(BF16)
        for g in range(g_pages):
            page = j * g_pages + g
            s = jnp.dot(qr, k_refs[g][0].astype(BF16), preferred_element_type=F32)
            base = (page * LANES - past_len).astype(F32)
            s_sc[page] = s + (rel_ref[...] + slope_ref[...] * base)

    @pl.when(j == n_steps - 1)
    def _():
        s_all = s_sc[...]
        s_new = jnp.sum(q_rows() * kn_ref[0], axis=-1, keepdims=True)
        m = jnp.maximum(jnp.max(jnp.max(s_all, axis=0), axis=-1, keepdims=True), s_new)
        p = jnp.exp(s_all - m)
        p_new = jnp.exp(s_new - m)
        l = jnp.sum(jnp.sum(p, axis=0), axis=-1, keepdims=True) + p_new
        wn = p / l
        wn_new = p_new / l
        lam = _lambda_value(lamv_ref)
        s_sc[:, 0:DA_HEADS, :] = wn[:, 0:DA_HEADS, :] - lam * wn[:, DA_HEADS:rows, :]
        s_sc[:, DA_HEADS:rows, :] = jnp.zeros((s_sc.shape[0], DA_HEADS, LANES), F32)
        wnew_sc[...] = wn_new[0:DA_HEADS] - lam * wn_new[DA_HEADS:rows]
        acc_sc[...] = jnp.zeros(acc_sc.shape, F32)

    @pl.when(j >= n_steps)
    def _():
        lane = lax.broadcasted_iota(I32, (rows, LANES), 1)
        own = lane % DA_HEADS == lax.broadcasted_iota(I32, (rows, LANES), 0)
        acc = acc_sc[...]
        for g in range(g_pages):
            page = (j - n_steps) * g_pages + g
            w = s_sc[page]
            w_rows = jnp.concatenate(
                [jnp.where(own, jnp.take_along_axis(w, (LANES // DA_HEADS) * c + lane // DA_HEADS,
                                                    axis=1), 0.0)
                 for c in range(DA_HEADS)], axis=-1)
            acc = acc + jnp.dot(w_rows.astype(BF16), v_refs[g][0].astype(BF16),
                                preferred_element_type=F32)
        acc_sc[...] = acc

    @pl.when(j == 2 * n_steps - 1)
    def _():
        acc = acc_sc[...]
        w_new = wnew_sc[...]
        v_new = vn_ref[0]
        outs = []
        for h in range(DA_HEADS):
            vs = slice(h * DA_VD, (h + 1) * DA_VD)
            o = acc[h:h + 1, :] + w_new[h:h + 1, :] * v_new[:, vs]
            outs.append(_rms(o) * subg_ref[...] * (1.0 - LAM_INIT))
        o_ref[0] = jnp.concatenate(outs, axis=-1)


def _attn_sample(q, k_new, v_new, cache_kt, cache_vr, page_table, lamv, subg, *, pages_per_step):
    nb, n_pages = page_table.shape
    page = cache_kt.shape[2]
    assert page == LANES
    past_len = n_pages * page
    g_pages = pages_per_step
    n_steps = n_pages // g_pages
    rows = 2 * DA_HEADS
    slope_rows = np.tile(_alibi_slopes(), 2)[:, None]
    rel = slope_rows * np.arange(page, dtype=np.float32)[None, :]
    pt = page_table.reshape(-1).astype(I32)
    row3 = lambda: pl.BlockSpec((1, 1, DA_WIDTH), lambda b, j, pt: (b, 0, 0))
    const2 = lambda shape: pl.BlockSpec(shape, lambda b, j, pt: (0, 0))

    def k_spec(g):
        return pl.BlockSpec(
            (1, DA_WIDTH, page),
            lambda b, j, pt: (pt[b * n_pages + jnp.minimum(j, n_steps - 1) * g_pages + g], 0, 0))

    def v_spec(g):
        return pl.BlockSpec(
            (1, page * DA_HEADS, DA_VD),
            lambda b, j, pt: (pt[b * n_pages + jnp.maximum(j - n_steps, 0) * g_pages + g], 0, 0))

    grid_spec = pltpu.PrefetchScalarGridSpec(
        num_scalar_prefetch=1,
        grid=(nb, 2 * n_steps),
        in_specs=[row3(), row3(), row3(), const2((rows, page)), const2((rows, 1)),
                  const2((4, DA_HD)), const2((1, DA_VD))]
                 + [k_spec(g) for g in range(g_pages)] + [v_spec(g) for g in range(g_pages)],
        out_specs=row3(),
        scratch_shapes=[pltpu.VMEM((n_pages, rows, page), F32), pltpu.VMEM((DA_HEADS, 1), F32),
                        pltpu.VMEM((rows, DA_VD), F32)],
    )
    out = pl.pallas_call(
        functools.partial(_attn_sample_kernel, pages_per_step=g_pages, past_len=past_len,
                          n_steps=n_steps),
        grid_spec=grid_spec,
        out_shape=jax.ShapeDtypeStruct((nb, 1, DA_WIDTH), F32),
        compiler_params=pltpu.CompilerParams(dimension_semantics=("parallel", "arbitrary"),
                                             vmem_limit_bytes=VMEM_LIMIT),
        name="attn_sample",
    )(pt, q.reshape(nb, 1, DA_WIDTH), k_new.reshape(nb, 1, DA_WIDTH), v_new.reshape(nb, 1, DA_WIDTH),
      jnp.asarray(rel), jnp.asarray(slope_rows), lamv, subg,
      *([cache_kt] * g_pages), *([cache_vr] * g_pages))
    return out.reshape(nb, DA_WIDTH)


def _ret_sample_kernel(q_ref, k_ref, qc_ref, kc_ref, v_ref, g_ref, s_ref, qdec_ref, sdec_ref,
                       r_ref, so_ref):
    q = q_ref[0]
    k = k_ref[0]
    qd_col = _bf(qc_ref[0] * qdec_ref[...])
    k_col = kc_ref[0]
    outs = []
    for h in range(RET_HEADS):
        ks = slice(h * RET_DK, (h + 1) * RET_DK)
        vs = slice(h * RET_DV, (h + 1) * RET_DV)
        vh = v_ref[0, :, vs]
        s_old = s_ref[0, h]
        qk = jnp.sum(q[:, ks] * k[:, ks], axis=-1, keepdims=True)
        o = qk * vh + jnp.sum(qd_col[ks, :] * _bf(s_old), axis=0, keepdims=True)
        so_ref[0, h] = sdec_ref[h] * s_old + k_col[ks, :] * vh
        outs.append(_rms(o) * _silu(g_ref[0, :, vs]))
    r_ref[0] = jnp.concatenate(outs, axis=-1)


def _ret_sample(qr, kr, vr, gr, state):
    nb = qr.shape[0]
    _, qdec, _, sdec = _ret_tables(1)
    row3 = lambda w: pl.BlockSpec((1, 1, w), lambda b: (b, 0, 0))
    col3 = pl.BlockSpec((1, RET_QK_WIDTH, 1), lambda b: (b, 0, 0))
    st = pl.BlockSpec((1, RET_HEADS, RET_DK, RET_DV), lambda b: (b, 0, 0, 0))
    r, s_new = pl.pallas_call(
        _ret_sample_kernel,
        grid=(nb,),
        in_specs=[row3(RET_QK_WIDTH), row3(RET_QK_WIDTH), col3, col3, row3(RET_WIDTH),
                  row3(RET_WIDTH), st,
                  pl.BlockSpec((RET_QK_WIDTH, 1), lambda b: (0, 0)),
                  pl.BlockSpec((RET_HEADS, RET_DK, RET_DV), lambda b: (0, 0, 0))],
        out_specs=(row3(RET_WIDTH), st),
        out_shape=(jax.ShapeDtypeStruct((nb, 1, RET_WIDTH), F32),
                   jax.ShapeDtypeStruct((nb, RET_HEADS, RET_DK, RET_DV), F32)),
        compiler_params=pltpu.CompilerParams(dimension_semantics=("parallel",)),
        name="ret_sample",
    )(qr.reshape(nb, 1, -1), kr.reshape(nb, 1, -1), qr.reshape(nb, -1, 1), kr.reshape(nb, -1, 1),
      vr.reshape(nb, 1, -1), gr.reshape(nb, 1, -1), state, qdec.reshape(-1, 1), sdec)
    return r.reshape(nb, RET_WIDTH), s_new


def _outproj_kernel(a_ref, r_ref, x_ref, wo_ref, g_ref, wr_ref, br_ref,
                    x1_ref, hfp_ref, ti_ref, tg_ref, *, a_transposed):
    if a_transposed:
        mix_a = lax.dot_general(a_ref[0], wo_ref[:DA_WIDTH, :], (((0,), (0,)), ((), ())),
                                preferred_element_type=F32)
    else:
        mix_a = _mm(a_ref[...], wo_ref[:DA_WIDTH, :])
    mix = mix_a + _mm(r_ref[...], wo_ref[DA_WIDTH:, :])
    x1 = x_ref[...] + mix
    x1_ref[...] = x1
    hf = _rms(x1) * g_ref[...]
    hfp_ref[0] = _pack_pairs(hf[:, :HALF])
    hfp_ref[1] = _pack_pairs(hf[:, HALF:])
    logits = _mm(hf, wr_ref[...]) + br_ref[...]
    lane = lax.broadcasted_iota(I32, logits.shape, 1)
    work = logits
    top_v, top_i = [], []
    for _ in range(TOP_K):
        m = jnp.max(work, axis=-1, keepdims=True)
        idx = jnp.min(jnp.where(work == m, lane, LANES), axis=-1, keepdims=True)
        top_v.append(m)
        top_i.append(idx)
        work = jnp.where(lane == idx, -jnp.inf, work)
    es = [jnp.exp(v - top_v[0]) for v in top_v]
    denom = es[0] + es[1] + es[2] + es[3]
    ti = jnp.zeros(logits.shape, I32)
    tg = jnp.zeros(logits.shape, F32)
    for kk in range(TOP_K):
        ti = jnp.where(lane == kk, top_i[kk], ti)
        tg = jnp.where(lane == kk, es[kk] / denom, tg)
    ti_ref[...] = ti
    tg_ref[...] = tg


def _outproj(a, r, x, w_out, g, w_router, b_router, *, tm):
    n = x.shape[0]
    row = lambda w: pl.BlockSpec((tm, w), lambda i: (i, 0))
    const = lambda shape: pl.BlockSpec(shape, lambda i: (0, 0))
    a_transposed = a.ndim == 3
    if a_transposed:
        n_s = a.shape[2] // tm
        a_spec = pl.BlockSpec((1, DA_WIDTH, tm), lambda i: (i // n_s, 0, i % n_s))
    else:
        a_spec = row(DA_WIDTH)
    return pl.pallas_call(
        functools.partial(_outproj_kernel, a_transposed=a_transposed),
        grid=(n // tm,),
        in_specs=[a_spec, row(RET_WIDTH), row(D_MODEL), const((D_MODEL, D_MODEL)),
                  const((1, D_MODEL)), const((D_MODEL, LANES)), const((1, LANES))],
        out_specs=(row(D_MODEL), pl.BlockSpec((2, tm, QUARTER), lambda i: (0, i, 0)),
                   row(LANES), row(LANES)),
        out_shape=(jax.ShapeDtypeStruct((n, D_MODEL), F32),
                   jax.ShapeDtypeStruct((2, n, QUARTER), I32),
                   jax.ShapeDtypeStruct((n, LANES), I32),
                   jax.ShapeDtypeStruct((n, LANES), F32)),
        compiler_params=pltpu.CompilerParams(dimension_semantics=("parallel",),
                                             vmem_limit_bytes=VMEM_LIMIT),
        name="outproj",
    )(a, r, x, w_out, g, w_router, b_router)


def _sc_gather(x, idx):
    m = idx.shape[0]
    cols = x.shape[1]
    mesh = plsc.VectorSubcoreMesh(core_axis_name="c", subcore_axis_name="s")

    @pl.kernel(out_type=jax.ShapeDtypeStruct((m, cols), x.dtype), mesh=mesh)
    def gather_kernel(x_hbm, i_hbm, o_hbm):
        def body(i_vmem, o_vmem):
            pltpu.sync_copy(x_hbm.at[i_vmem.at[0]], o_vmem)

        pltpu.emit_pipeline(
            body,
            grid=(m // SC_WINDOW,),
            in_specs=[pl.BlockSpec((1, SC_WINDOW), lambda i: (0, i))],
            out_specs=[pl.BlockSpec((SC_WINDOW, cols), lambda i: (i, 0))],
            core_axis_name=("c", "s"),
            dimension_semantics=(pltpu.PARALLEL,),
        )(i_hbm, o_hbm)

    return gather_kernel(x, idx.reshape(1, m))


def _filler(n, modulus):
    return jnp.arange(n, dtype=I32) % modulus


def _sc_scatter(x, idx, out_rows):
    m = idx.shape[0]
    cols = x.shape[1]
    n_blk = x.shape[0] // 2 // SC_WINDOW
    mesh = plsc.VectorSubcoreMesh(core_axis_name="c", subcore_axis_name="s")

    @pl.kernel(out_type=jax.ShapeDtypeStruct((out_rows, cols), x.dtype), mesh=mesh)
    def scatter_kernel(x_hbm, i_hbm, o_hbm):
        def body(x_vmem, i_vmem):
            pltpu.sync_copy(x_vmem, o_hbm.at[i_vmem.at[0]])

        pltpu.emit_pipeline(
            body,
            grid=(m // SC_WINDOW,),
            in_specs=[pl.BlockSpec((SC_WINDOW, cols),
                                   lambda i: ((i // (TOP_K * n_blk)) * n_blk + i % n_blk, 0)),
                      pl.BlockSpec((1, SC_WINDOW), lambda i: (0, i))],
            out_specs=[],
            core_axis_name=("c", "s"),
            dimension_semantics=(pltpu.PARALLEL,),
        )(x_hbm, i_hbm)

    return scatter_kernel(x, idx.reshape(1, m))


def _rank_kernel(ti_ref, tri_ref, rank_ref, cnt_ref, base_sc):
    @pl.when(pl.program_id(0) == 0)
    def _():
        base_sc[...] = jnp.zeros(base_sc.shape, F32)

    ti = ti_ref[...]
    lane = lax.broadcasted_iota(I32, ti.shape, 1)
    picks = [lane == ti[:, k:k + 1] for k in range(TOP_K)]
    onehot = jnp.zeros(ti.shape, F32)
    for pk in picks:
        onehot = onehot + jnp.where(pk, 1.0, 0.0)
    pos = jnp.dot(tri_ref[...], onehot.astype(BF16), preferred_element_type=F32) + base_sc[...]
    rank = jnp.zeros(ti.shape, I32)
    for k, pk in enumerate(picks):
        r_k = jnp.sum(jnp.where(pk, pos, 0.0), axis=-1, keepdims=True)
        rank = jnp.where(lane == k, r_k.astype(I32), rank)
    rank_ref[...] = rank
    base_sc[...] = base_sc[...] + jnp.sum(onehot, axis=0, keepdims=True)
    cnt_ref[...] = base_sc[...]


def _rank(ti, *, tr):
    n = ti.shape[0]
    tri = jnp.asarray(np.tril(np.ones((tr, tr), np.float32), -1), BF16)
    return pl.pallas_call(
        _rank_kernel,
        grid=(n // tr,),
        in_specs=[pl.BlockSpec((tr, LANES), lambda i: (i, 0)),
                  pl.BlockSpec((tr, tr), lambda i: (0, 0))],
        out_specs=(pl.BlockSpec((tr, LANES), lambda i: (i, 0)),
                   pl.BlockSpec((1, LANES), lambda i: (0, 0))),
        out_shape=(jax.ShapeDtypeStruct((n, LANES), I32), jax.ShapeDtypeStruct((1, LANES), F32)),
        scratch_shapes=[pltpu.VMEM((1, LANES), F32)],
        compiler_params=pltpu.CompilerParams(dimension_semantics=("arbitrary",)),
        name="route_rank",
    )(ti, tri)


def _prep_expert_weights(wup_ref, wdn_ref, wupb, wdnb):
    lane = lax.broadcasted_iota(I32, (PREP_ROWS, LANES), 1)
    first_half = lane < LANES // 2
    idx_even = (2 * lane) % LANES
    idx_odd = (2 * lane + 1) % LANES

    def body(r, carry):
        rows = pl.ds(pl.multiple_of(r * PREP_ROWS, PREP_ROWS), PREP_ROWS)
        for ct in range(D_FF // LANES):
            a = wup_ref[0, rows, 2 * ct * LANES:(2 * ct + 1) * LANES]
            b = wup_ref[0, rows, (2 * ct + 1) * LANES:(2 * ct + 2) * LANES]
            even = jnp.where(first_half, jnp.take_along_axis(a, idx_even, axis=1),
                             jnp.take_along_axis(b, idx_even, axis=1))
            odd = jnp.where(first_half, jnp.take_along_axis(a, idx_odd, axis=1),
                            jnp.take_along_axis(b, idx_odd, axis=1))
            wupb[rows, ct * LANES:(ct + 1) * LANES] = even.astype(BF16)
            wupb[rows, D_FF + ct * LANES:D_FF + (ct + 1) * LANES] = odd.astype(BF16)
        wdnb[rows, :] = wdn_ref[0, rows, :].astype(BF16)
        return carry

    lax.fori_loop(0, D_MODEL // PREP_ROWS, body, 0)


def _moe_kernel(be_ref, na_ref, nv_ref, xs_ref, wup_ref, bup_ref, wdn_ref, bdn_ref, out_ref,
                wupb, wdnb):
    i = pl.program_id(0)
    n_act = na_ref[0]
    e = be_ref[jnp.minimum(i, n_act - 1)]
    e_prev = be_ref[jnp.maximum(i - 1, 0)]

    @pl.when(jnp.logical_and(i < n_act, jnp.logical_or(i == 0, e != e_prev)))
    def _():
        _prep_expert_weights(wup_ref, wdn_ref, wupb, wdnb)

    @pl.when(i < n_act)
    def _():
        lo0, hi0 = _unpack_pairs(xs_ref[0])
        lo1, hi1 = _unpack_pairs(xs_ref[1])
        x = jnp.concatenate([lo0, hi0, lo1, hi1], axis=-1)
        live = lax.broadcasted_iota(I32, x.shape, 0) < nv_ref[i]
        x = jnp.where(live, x, 0.0).astype(BF16)
        u = jnp.dot(x, wupb[...], preferred_element_type=F32) + bup_ref[0]
        g = jnp.minimum(u[:, :D_FF], SWIGLU_LIMIT)
        lin = jnp.clip(u[:, D_FF:], -SWIGLU_LIMIT, SWIGLU_LIMIT)
        y = g * (1.0 / (1.0 + jnp.exp(-SWIGLU_ALPHA * g))) * (lin + 1.0)
        o = jnp.dot(y.astype(BF16), wdnb[...], preferred_element_type=F32) + bdn_ref[0]
        out_ref[0] = _pack_pairs(o[:, :HALF])
        out_ref[1] = _pack_pairs(o[:, HALF:])

    @pl.when(i >= n_act)
    def _():
        out_ref[...] = jnp.zeros(out_ref.shape, I32)


def _moe(block_e, n_active, n_valid, xs, w_up, b_up_d, w_down, b_down):
    n_blocks = block_e.shape[0]
    n_slots = n_blocks * MOE_TM

    def act(i, na):
        return jnp.minimum(i, na[0] - 1)

    grid_spec = pltpu.PrefetchScalarGridSpec(
        num_scalar_prefetch=3,
        grid=(n_blocks,),
        in_specs=[
            pl.BlockSpec((2, MOE_TM, QUARTER), lambda i, be, na, nv: (0, act(i, na), 0)),
            pl.BlockSpec((1, D_MODEL, 2 * D_FF), lambda i, be, na, nv: (be[act(i, na)], 0, 0)),
            pl.BlockSpec((1, 1, 2 * D_FF), lambda i, be, na, nv: (be[act(i, na)], 0, 0)),
            pl.BlockSpec((1, D_FF, D_MODEL), lambda i, be, na, nv: (be[act(i, na)], 0, 0)),
            pl.BlockSpec((1, 1, D_MODEL), lambda i, be, na, nv: (be[act(i, na)], 0, 0)),
        ],
        out_specs=pl.BlockSpec((2, MOE_TM, QUARTER), lambda i, be, na, nv: (0, i, 0)),
        scratch_shapes=[pltpu.VMEM((D_MODEL, 2 * D_FF), BF16), pltpu.VMEM((D_FF, D_MODEL), BF16)],
    )
    return pl.pallas_call(
        _moe_kernel,
        grid_spec=grid_spec,
        out_shape=jax.ShapeDtypeStruct((2, n_slots, QUARTER), I32),
        compiler_params=pltpu.CompilerParams(dimension_semantics=("arbitrary",),
                                             vmem_limit_bytes=VMEM_LIMIT),
        name="moe_ffn",
    )(block_e, n_active, n_valid, xs, w_up, b_up_d, w_down, b_down)


def _slot_layout(counts, top_i, rank, n_blocks):
    experts = jnp.arange(N_EXPERTS, dtype=I32)
    padded = (counts + MOE_TM - 1) // MOE_TM * MOE_TM
    pad_end = jnp.cumsum(padded)
    pad_start = pad_end - padded
    pick = top_i[:, :, None] == experts[None, None, :]
    dest = rank + jnp.sum(jnp.where(pick, pad_start[None, None, :], 0), axis=-1)
    blk_start = jnp.arange(n_blocks, dtype=I32) * MOE_TM
    block_e = jnp.minimum(jnp.sum((blk_start[:, None] >= pad_end[None, :]).astype(I32), axis=1),
                          N_EXPERTS - 1)
    live = jnp.clip(counts[block_e] - (blk_start - pad_start[block_e]), 0, MOE_TM)
    n_active = (pad_end[-1] // MOE_TM).astype(I32).reshape(1)
    return dest.astype(I32), block_e.astype(I32), live.astype(I32), n_active


def _combine_kernel(yg_ref, tg_ref, x1_ref, g_ref, o_ref):
    tg = tg_ref[...]
    parts = [jnp.zeros((x1_ref.shape[0], QUARTER), F32) for _ in range(4)]
    for kk in range(TOP_K):
        gate = tg[:, kk:kk + 1]
        lo0, hi0 = _unpack_pairs(yg_ref[0, kk])
        lo1, hi1 = _unpack_pairs(yg_ref[1, kk])
        for j, piece in enumerate((lo0, hi0, lo1, hi1)):
            parts[j] = parts[j] + gate * piece
    x2 = x1_ref[...] + jnp.concatenate(parts, axis=-1)
    o_ref[...] = _rms(x2) * g_ref[...]


def _combine(yg, tg, x1, g, *, tm, row_offset):
    n = x1.shape[0]
    blk0 = row_offset // tm
    return pl.pallas_call(
        _combine_kernel,
        grid=(n // tm,),
        in_specs=[pl.BlockSpec((2, TOP_K, tm, QUARTER), lambda i: (0, 0, i + blk0, 0)),
                  pl.BlockSpec((tm, LANES), lambda i: (i, 0)),
                  pl.BlockSpec((tm, D_MODEL), lambda i: (i, 0)),
                  pl.BlockSpec((1, D_MODEL), lambda i: (0, 0))],
        out_specs=pl.BlockSpec((tm, D_MODEL), lambda i: (i, 0)),
        out_shape=jax.ShapeDtypeStruct((n, D_MODEL), F32),
        compiler_params=pltpu.CompilerParams(dimension_semantics=("parallel",),
                                             vmem_limit_bytes=VMEM_LIMIT),
        name="combine",
    )(yg, tg, x1, g)


def _extend_w_in(w):
    swap = np.arange(RET_QK_WIDTH).reshape(RET_HEADS, 2, RET_DK // 2)[:, ::-1, :].reshape(-1)
    qa, ka, va = w[:, 0:512], w[:, 512:1024], w[:, 1024:1536]
    qr, kr = w[:, 1536:1792], w[:, 1792:2048]
    vr, gr = w[:, 2048:2560], w[:, 2560:3072]
    sa = DA_HD ** -0.5
    sr = RET_DK ** -0.5
    return jnp.concatenate([qa * sa, ka, va, qr, qr[:, swap], kr * sr, kr[:, swap] * sr, vr, gr], axis=1)


def _rotary_tables(pos):
    half = RET_DK // 2
    inv = ROPE_BASE ** (-jnp.arange(half, dtype=F32) / half)
    ang = pos.astype(F32)[:, None] * inv[None, :]
    cos = jnp.cos(ang)
    sin = jnp.sin(ang)
    cos_t = jnp.tile(jnp.concatenate([cos, cos], axis=1), (1, RET_HEADS))
    sin_t = jnp.tile(jnp.concatenate([-sin, sin], axis=1), (1, RET_HEADS))
    return cos_t, sin_t


def kernel(x_prompt, x_sample, cache_k, cache_v, state_ret, page_table, norm_attn_g, w_in, lam_q1,
           lam_k1, lam_q2, lam_k2, da_subln_g, w_out, norm_ffn_g, w_router, b_router, w_up, b_up,
           w_down, b_down, norm_final_g):
    return _forward(x_prompt, x_sample, cache_k, cache_v, state_ret, page_table, norm_attn_g, w_in,
                    lam_q1, lam_k1, lam_q2, lam_k2, da_subln_g, w_out, norm_ffn_g, w_router,
                    b_router, w_up, b_up, w_down, b_down, norm_final_g,
                    tm=512, tq=512, pages_per_step=32, gather=_sc_gather, scatter=_sc_scatter)


def _forward(x_prompt, x_sample, cache_k, cache_v, state_ret, page_table, norm_attn_g, w_in, lam_q1,
             lam_k1, lam_q2, lam_k2, da_subln_g, w_out, norm_ffn_g, w_router, b_router, w_up, b_up,
             w_down, b_down, norm_final_g, *, tm, tq, pages_per_step, gather, scatter):
    batch, seq, _ = x_prompt.shape
    nb = x_sample.shape[0]
    n_pool, page = cache_k.shape[1], cache_k.shape[2]
    past_len = page_table.shape[1] * page
    n_prompt = batch * seq

    w_ext_b = _extend_w_in(w_in[0]).astype(BF16)
    w_ext_t = w_ext_b[:, :_C_QR].T
    g_attn = norm_attn_g[0].reshape(1, D_MODEL)
    g_ffn = norm_ffn_g[0].reshape(1, D_MODEL)
    g_fin = norm_final_g.reshape(1, D_MODEL)
    lamv = jnp.stack([lam_q1[0], lam_k1[0], lam_q2[0], lam_k2[0]]).astype(F32)
    subg = da_subln_g[0].reshape(1, DA_VD)
    w_o_b = w_out[0].astype(BF16)
    w_r = jnp.pad(w_router[0], ((0, 0), (0, LANES - N_EXPERTS))).astype(BF16)
    b_r = jnp.pad(b_router[0], (0, LANES - N_EXPERTS), constant_values=NEG_INF).reshape(1, LANES)
    assert cache_k.shape[0] == 1 and w_up.shape[0] == 1
    w_up_e = w_up.reshape(N_EXPERTS, D_MODEL, 2 * D_FF)
    b_up_d = jnp.concatenate([b_up[0][:, 0::2], b_up[0][:, 1::2]], axis=-1).reshape(N_EXPERTS, 1, 2 * D_FF)
    w_down_e = w_down.reshape(N_EXPERTS, D_FF, D_MODEL)
    b_dn = b_down.reshape(N_EXPERTS, 1, D_MODEL)
    cos_p, sin_p = _rotary_tables(jnp.arange(seq))
    cos_s, sin_s = _rotary_tables(past_len + jnp.zeros((nb,), I32))

    xp = x_prompt.reshape(n_prompt, D_MODEL)
    qat, kat, kab, va, vat, qr, kr, vr, gr = _inproj(xp, g_attn, w_ext_b, w_ext_t, cos_p, sin_p,
                                                     tm=tm, seq=seq)
    a_p = _attn_prompt(qat, kab, vat, lamv, subg, batch=batch, seq=seq, tq=tq)
    r_p, ret_p = _ret_prompt(qr, kr, vr, gr, batch=batch, seq=seq)
    x1_p, hfp_p, ti_p, tg_p = _outproj(a_p, r_p, xp, w_o_b, g_ffn, w_r, b_r, tm=tm)

    xs_ = x_sample.reshape(nb, D_MODEL)
    qa_s, ka_s, va_s, qr_s, kr_s, vr_s, gr_s = _inproj(xs_, g_attn, w_ext_b, w_ext_t, cos_s, sin_s,
                                                       tm=nb)
    ckt = jnp.transpose(cache_k.reshape(n_pool, page, DA_WIDTH), (0, 2, 1))
    cvr = cache_v.reshape(n_pool, page * DA_HEADS, DA_VD)
    a_s = _attn_sample(qa_s, ka_s, va_s, ckt, cvr, page_table, lamv, subg,
                       pages_per_step=pages_per_step)
    r_s, ret_s = _ret_sample(qr_s, kr_s, vr_s, gr_s,
                             state_ret.reshape(nb, RET_HEADS, RET_DK, RET_DV))
    x1_s, hfp_s, ti_s, tg_s = _outproj(a_s, r_s, xs_, w_o_b, g_ffn, w_r, b_r, tm=nb)

    n_tok = n_prompt + nb
    tok_pad = -(-n_tok // ROUTE_TILE) * ROUTE_TILE
    pad = tok_pad - n_tok
    hfp = jnp.concatenate([hfp_p, hfp_s, jnp.zeros((2, pad, QUARTER), I32)], axis=1)
    ti_all = jnp.concatenate([ti_p, ti_s, jnp.full((pad, LANES), LANES - 1, I32)], axis=0)
    rank, cnt = _rank(ti_all, tr=ROUTE_TILE)
    counts = cnt[0, :N_EXPERTS].astype(I32)
    n_blocks = -(-n_tok * TOP_K // MOE_TM) + N_EXPERTS
    n_slots = n_blocks * MOE_TM
    dest, block_e, n_live, n_active = _slot_layout(counts, ti_all[:n_tok, :TOP_K],
                                                   rank[:n_tok, :TOP_K], n_blocks)
    slots_pad = -(-(n_slots + 1) // (SC_UNIT // 2)) * (SC_UNIT // 2)
    dump = n_slots + _filler(pad, slots_pad - n_slots)
    dest_sc = jnp.concatenate([dest, jnp.broadcast_to(dump[:, None], (pad, TOP_K))], axis=0).T
    xs_sorted = scatter(hfp.reshape(2 * tok_pad, QUARTER),
                        jnp.concatenate([dest_sc, slots_pad + dest_sc]).reshape(-1), 2 * slots_pad)
    out_sorted = _moe(block_e, n_active, n_live, xs_sorted.reshape(2, slots_pad, QUARTER),
                      w_up_e, b_up_d, w_down_e, b_dn)
    fill = jnp.broadcast_to(_filler(pad, n_slots)[:, None], (pad, TOP_K))
    dest_g = jnp.concatenate([dest, fill], axis=0).T
    yg = gather(out_sorted.reshape(2 * n_slots, QUARTER),
                jnp.concatenate([dest_g, n_slots + dest_g]).reshape(-1))
    yg = yg.reshape(2, TOP_K, tok_pad, QUARTER)
    y_p = _combine(yg, tg_p, x1_p, g_fin, tm=tm, row_offset=0)
    y_s = _combine(yg, tg_s, x1_s, g_fin, tm=nb, row_offset=n_prompt)

    depth = 1
    return (y_p.reshape(batch, seq, D_MODEL),
            y_s.reshape(nb, 1, D_MODEL),
            jnp.transpose(kat.reshape(depth, batch, DA_HEADS, 2, DA_HD, seq), (0, 1, 5, 2, 3, 4)),
            va.reshape(depth, batch, seq, DA_HEADS, DA_VD),
            ret_p.reshape(depth, batch, RET_HEADS, RET_DK, RET_DV),
            ka_s.reshape(depth, nb, 1, DA_HEADS, 2, DA_HD),
            va_s.reshape(depth, nb, 1, DA_HEADS, DA_VD),
            ret_s.reshape(depth, nb, RET_HEADS, RET_DK, RET_DV))
```

```python
import functools
import math

import numpy as np
import jax
import jax.numpy as jnp
from jax import lax
from jax.experimental import pallas as pl
from jax.experimental.pallas import tpu as pltpu
from jax.experimental.pallas import tpu_sc as plsc

F32 = jnp.float32
BF16 = jnp.bfloat16
I32 = jnp.int32

D_MODEL = 1024
DA_HEADS = 4
DA_VD = 128
DA_HD = 64
DA_WIDTH = DA_HEADS * DA_VD
RET_HEADS = 4
RET_DV = 128
RET_DK = 64
RET_WIDTH = RET_HEADS * RET_DV
RET_QK_WIDTH = RET_HEADS * RET_DK
RET_CHUNK = 128
ROPE_BASE = 10000.0
N_EXPERTS = 32
TOP_K = 4
D_FF = 1024
SWIGLU_LIMIT = 7.0
SWIGLU_ALPHA = 1.702
NORM_EPS = 1e-5
NEG_INF = -1e30
LAM_INIT = 0.8 - 0.6 * math.exp(-0.3 * 0)
LANES = 128
HALF = D_MODEL // 2
QUARTER = D_MODEL // 4
MOE_TM = 512
PREP_ROWS = 64
ROUTE_TILE = 512
SC_WINDOW = 128
SC_SUBCORES = 32
SC_UNIT = SC_WINDOW * SC_SUBCORES
VMEM_LIMIT = 56 * 1024 * 1024

_C_QA, _C_KA, _C_VA = 0, 512, 1024
_C_QR, _C_QRS, _C_KR, _C_KRS = 1536, 1792, 2048, 2304
_C_VR, _C_GR, _C_END = 2560, 3072, 3584


def _mm(a, b):
    return jnp.dot(a.astype(BF16), b.astype(BF16), preferred_element_type=F32)


def _bf(x):
    return x.astype(BF16).astype(F32)


def _rms(x):
    return x * lax.rsqrt(jnp.mean(x * x, axis=-1, keepdims=True) + NORM_EPS)


def _silu(g):
    return g / (1.0 + jnp.exp(-g))


def _pack_pairs(x):
    w = x.shape[-1] // 2
    lo = lax.bitcast_convert_type(x[:, :w].astype(BF16).astype(F32), jnp.uint32)
    hi = lax.bitcast_convert_type(x[:, w:].astype(BF16).astype(F32), jnp.uint32)
    return lax.bitcast_convert_type(hi | (lo >> 16), I32)


def _unpack_pairs(p):
    u = lax.bitcast_convert_type(p, jnp.uint32)
    lo = lax.bitcast_convert_type(u << 16, F32)
    hi = lax.bitcast_convert_type(u & jnp.uint32(0xFFFF0000), F32)
    return lo, hi


def _alibi_slopes():
    return np.asarray([2.0 ** (-8.0 * (h + 1) / DA_HEADS) for h in range(DA_HEADS)], np.float32)


def _inproj_kernel(x_ref, g_ref, w_ref, wt_ref, cos_ref, sin_ref, *outs, transposed):
    hm = (_rms(x_ref[...]) * g_ref[...]).astype(BF16)

    def mm(lo, hi):
        return _mm(hm, w_ref[:, lo:hi])

    def mm_t(lo, hi):
        return lax.dot_general(wt_ref[lo:hi, :], hm, (((1,), (1,)), ((), ())),
                               preferred_element_type=F32)

    if transposed:
        qat_ref, kat_ref, kab_ref, va_ref, vat_ref, qr_ref, kr_ref, vr_ref, gr_ref = outs
        qat_ref[0] = mm_t(_C_QA, _C_KA).astype(BF16)
        kat_ref[0] = mm_t(_C_KA, _C_VA)
        kab_ref[...] = mm(_C_KA, _C_VA).astype(BF16)
        va = mm(_C_VA, _C_QR)
        for h in range(DA_HEADS):
            va_ref[pl.ds(h, va.shape[0], stride=DA_HEADS), :] = va[:, h * DA_VD:(h + 1) * DA_VD]
        vat_ref[0] = mm_t(_C_VA, _C_QR).astype(BF16)
    else:
        qa_ref, ka_ref, va_ref, qr_ref, kr_ref, vr_ref, gr_ref = outs
        qa_ref[...] = mm(_C_QA, _C_KA)
        ka_ref[...] = mm(_C_KA, _C_VA)
        va_ref[...] = mm(_C_VA, _C_QR)
    cos = cos_ref[...]
    sin = sin_ref[...]
    qr_ref[...] = (mm(_C_QR, _C_QRS) * cos + mm(_C_QRS, _C_KR) * sin).astype(qr_ref.dtype)
    kr_ref[...] = (mm(_C_KR, _C_KRS) * cos + mm(_C_KRS, _C_VR) * sin).astype(kr_ref.dtype)
    vr_ref[...] = mm(_C_VR, _C_GR).astype(vr_ref.dtype)
    gr_ref[...] = mm(_C_GR, _C_END).astype(gr_ref.dtype)


def _inproj(x, g, w_ext, w_ext_t, cos_t, sin_t, *, tm, seq=None):
    n = x.shape[0]
    n_tab = cos_t.shape[0] // tm
    transposed = seq is not None
    act = BF16 if transposed else F32
    row = lambda w: pl.BlockSpec((tm, w), lambda i: (i, 0))
    tab = pl.BlockSpec((tm, RET_QK_WIDTH), lambda i: (i % n_tab, 0))
    sds = jax.ShapeDtypeStruct
    tail_shapes = (sds((n, RET_QK_WIDTH), act), sds((n, RET_QK_WIDTH), act),
                   sds((n, RET_WIDTH), act), sds((n, RET_WIDTH), act))
    tail_specs = (row(RET_QK_WIDTH), row(RET_QK_WIDTH), row(RET_WIDTH), row(RET_WIDTH))
    if transposed:
        n_s = seq // tm
        tr = pl.BlockSpec((1, DA_WIDTH, tm), lambda i: (i // n_s, 0, i % n_s))
        head_shapes = (sds((n // seq, DA_WIDTH, seq), BF16), sds((n // seq, DA_WIDTH, seq), F32),
                       sds((n, DA_WIDTH), BF16), sds((n * DA_HEADS, DA_VD), F32),
                       sds((n // seq, DA_WIDTH, seq), BF16))
        head_specs = (tr, tr, row(DA_WIDTH),
                      pl.BlockSpec((tm * DA_HEADS, DA_VD), lambda i: (i, 0)), tr)
    else:
        head_shapes = (sds((n, DA_WIDTH), F32),) * 3
        head_specs = (row(DA_WIDTH),) * 3
    return pl.pallas_call(
        functools.partial(_inproj_kernel, transposed=transposed),
        grid=(n // tm,),
        in_specs=[row(D_MODEL),
                  pl.BlockSpec((1, D_MODEL), lambda i: (0, 0)),
                  pl.BlockSpec((D_MODEL, _C_END), lambda i: (0, 0)),
                  pl.BlockSpec((_C_QR, D_MODEL), lambda i: (0, 0)),
                  tab, tab],
        out_specs=head_specs + tail_specs,
        out_shape=head_shapes + tail_shapes,
        compiler_params=pltpu.CompilerParams(dimension_semantics=("parallel",),
                                             vmem_limit_bytes=VMEM_LIMIT),
        name="inproj",
    )(x, g, w_ext, w_ext_t, cos_t, sin_t)


def _lambda_value(lamv_ref):
    lv = lamv_ref[...]
    s1 = jnp.sum(lv[0:1] * lv[1:2], axis=-1, keepdims=True)
    s2 = jnp.sum(lv[2:3] * lv[3:4], axis=-1, keepdims=True)
    return jnp.exp(s1) - jnp.exp(s2) + LAM_INIT


def _attn_prompt_kernel(qi_tab, ki_tab, qt_ref, k_ref, vt_ref, rel_ref, relm_ref, off_ref,
                        lamv_ref, subg_ref, o_ref, m_sc, l_sc, acc_sc):
    hd = pl.program_id(1)
    t = pl.program_id(2)
    qi = qi_tab[t]
    ki = ki_tab[t]

    @pl.when(ki == 0)
    def _():
        m_sc[...] = jnp.full(m_sc.shape, NEG_INF, F32)
        l_sc[...] = jnp.zeros(l_sc.shape, F32)
        acc_sc[...] = jnp.zeros(acc_sc.shape, F32)

    off = off_ref[hd, t]

    def update(rel):
        qt = qt_ref[0]
        k = k_ref[...]
        vt = vt_ref[0]
        row = lax.broadcasted_iota(I32, qt.shape, 0)
        for c in range(2):
            qc = jnp.where((row >= c * DA_HD) & (row < (c + 1) * DA_HD), qt, jnp.zeros_like(qt))
            tt = jnp.dot(k, qc, preferred_element_type=F32) + rel
            m_old = m_sc[c]
            m_new = jnp.maximum(m_old, jnp.max(tt, axis=0, keepdims=True) + off)
            alpha = jnp.exp(m_old - m_new)
            p = jnp.exp(tt + (off - m_new))
            l_sc[c] = alpha * l_sc[c] + jnp.sum(p, axis=0, keepdims=True)
            acc_sc[c] = alpha * acc_sc[c] + jnp.dot(vt, p.astype(BF16), preferred_element_type=F32)
            m_sc[c] = m_new

    @pl.when(ki < qi)
    def _():
        update(rel_ref[0])

    @pl.when(ki == qi)
    def _():
        update(relm_ref[0])
        lam = _lambda_value(lamv_ref)
        o = acc_sc[0] / l_sc[0] - lam * (acc_sc[1] / l_sc[1])
        ms = jnp.mean(o * o, axis=0, keepdims=True)
        a = o * lax.rsqrt(ms + NORM_EPS) * subg_ref[...] * (1.0 - LAM_INIT)
        o_ref[0] = a.astype(o_ref.dtype)


def _attn_prompt(qat, kab, vat, lamv, subg, *, batch, seq, tq):
    nq = seq // tq
    steps = [(i, j) for i in range(nq) for j in range(i + 1)]
    qi_tab = jnp.asarray([s[0] for s in steps], I32)
    ki_tab = jnp.asarray([s[1] for s in steps], I32)
    slopes = _alibi_slopes()
    dist = (np.arange(tq)[None, :] - np.arange(tq)[:, None]).astype(np.float32)
    rel = -slopes[:, None, None] * dist[None]
    relm = np.where(dist[None] >= 0, rel, np.float32(NEG_INF)).astype(np.float32)
    blk = np.asarray([(s[0] - s[1]) * tq for s in steps], np.float32)
    off = -slopes[:, None] * blk[None, :]
    qspec = pl.BlockSpec((1, DA_VD, tq), lambda b, h, t, qt, kt: (b, h, qt[t]))
    kspec = pl.BlockSpec((tq, DA_VD), lambda b, h, t, qt, kt: (b * nq + kt[t], h))
    vspec = pl.BlockSpec((1, DA_VD, tq), lambda b, h, t, qt, kt: (b, h, kt[t]))
    relspec = pl.BlockSpec((1, tq, tq), lambda b, h, t, qt, kt: (h, 0, 0))
    grid_spec = pltpu.PrefetchScalarGridSpec(
        num_scalar_prefetch=2,
        grid=(batch, DA_HEADS, len(steps)),
        in_specs=[qspec, kspec, vspec, relspec, relspec,
                  pl.BlockSpec(memory_space=pltpu.SMEM),
                  pl.BlockSpec((4, DA_HD), lambda b, h, t, qt, kt: (0, 0)),
                  pl.BlockSpec((DA_VD, 1), lambda b, h, t, qt, kt: (0, 0))],
        out_specs=qspec,
        scratch_shapes=[pltpu.VMEM((2, 1, tq), F32), pltpu.VMEM((2, 1, tq), F32),
                        pltpu.VMEM((2, DA_VD, tq), F32)],
    )
    return pl.pallas_call(
        _attn_prompt_kernel,
        grid_spec=grid_spec,
        out_shape=jax.ShapeDtypeStruct((batch, DA_WIDTH, seq), BF16),
        compiler_params=pltpu.CompilerParams(
            dimension_semantics=("parallel", "parallel", "arbitrary"),
            vmem_limit_bytes=VMEM_LIMIT),
        name="attn_prompt",
    )(qi_tab, ki_tab, qat, kab, vat, jnp.asarray(rel), jnp.asarray(relm), jnp.asarray(off),
      lamv, subg.reshape(DA_VD, 1))


def _ret_prompt_kernel(q_ref, k_ref, v_ref, g_ref, dec_ref, qdec_ref, kdec_ref, sdec_ref,
                       r_ref, s_ref, *, batch):
    c = pl.program_id(0)

    @pl.when(c == 0)
    def _():
        s_ref[...] = jnp.zeros(s_ref.shape, F32)

    qdec = qdec_ref[...]
    kdec = kdec_ref[...]
    for b in range(batch):
        q = q_ref[b].astype(F32)
        k = k_ref[b].astype(F32)
        qd = (q * qdec).astype(BF16)
        kd = (k * kdec).astype(BF16)
        qb = q_ref[b]
        kb = k_ref[b]
        for h in range(RET_HEADS):
            ks = slice(h * RET_DK, (h + 1) * RET_DK)
            vs = slice(h * RET_DV, (h + 1) * RET_DV)
            vh = v_ref[b, :, vs]
            qk = lax.dot_general(qb[:, ks], kb[:, ks], (((1,), (1,)), ((), ())),
                                 preferred_element_type=F32) * dec_ref[h]
            s_old = s_ref[b, h]
            o = jnp.dot(qk.astype(BF16), vh, preferred_element_type=F32)
            o = o + jnp.dot(qd[:, ks], s_old.astype(BF16), preferred_element_type=F32)
            s_ref[b, h] = sdec_ref[h] * s_old + lax.dot_general(
                kd[:, ks], vh, (((0,), (0,)), ((), ())), preferred_element_type=F32)
            gate = _silu(g_ref[b, :, vs].astype(F32))
            r_ref[b, :, vs] = (_rms(o) * gate).astype(r_ref.dtype)


def _ret_tables(length):
    log_g = jnp.log(1.0 - 2.0 ** (-5.0 - jnp.arange(RET_HEADS, dtype=F32)))
    idx = jnp.arange(length, dtype=F32)
    diff = idx[:, None] - idx[None, :]
    dec = jnp.where(diff >= 0, jnp.exp(jnp.maximum(diff, 0.0)[None] * log_g[:, None, None]), 0.0)
    qdec = jnp.exp((idx + 1.0)[:, None] * log_g[None, :])
    kdec = jnp.exp((length - 1.0 - idx)[:, None] * log_g[None, :])
    sdec = jnp.exp(length * log_g)
    rep = lambda a: jnp.repeat(a, RET_DK, axis=1)
    sdec_b = jnp.broadcast_to(sdec[:, None, None], (RET_HEADS, RET_DK, RET_DV))
    return dec, rep(qdec), rep(kdec), sdec_b


def _ret_prompt(qr, kr, vr, gr, *, batch, seq):
    nc = seq // RET_CHUNK
    dec, qdec, kdec, sdec = _ret_tables(RET_CHUNK)
    q3 = qr.reshape(batch, seq, RET_QK_WIDTH)
    k3 = kr.reshape(batch, seq, RET_QK_WIDTH)
    v3 = vr.reshape(batch, seq, RET_WIDTH)
    g3 = gr.reshape(batch, seq, RET_WIDTH)
    blk = lambda w: pl.BlockSpec((batch, RET_CHUNK, w), lambda c: (0, c, 0))
    const = lambda shape: pl.BlockSpec(shape, lambda c: (0,) * len(shape))
    r, s = pl.pallas_call(
        functools.partial(_ret_prompt_kernel, batch=batch),
        grid=(nc,),
        in_specs=[blk(RET_QK_WIDTH), blk(RET_QK_WIDTH), blk(RET_WIDTH), blk(RET_WIDTH),
                  const((RET_HEADS, RET_CHUNK, RET_CHUNK)),
                  const((RET_CHUNK, RET_QK_WIDTH)), const((RET_CHUNK, RET_QK_WIDTH)),
                  const((RET_HEADS, RET_DK, RET_DV))],
        out_specs=(blk(RET_WIDTH), const((batch, RET_HEADS, RET_DK, RET_DV))),
        out_shape=(jax.ShapeDtypeStruct((batch, seq, RET_WIDTH), BF16),
                   jax.ShapeDtypeStruct((batch, RET_HEADS, RET_DK, RET_DV), F32)),
        compiler_params=pltpu.CompilerParams(dimension_semantics=("arbitrary",),
                                             vmem_limit_bytes=VMEM_LIMIT),
        name="ret_prompt",
    )(q3, k3, v3, g3, dec, qdec, kdec, sdec)
    return r.reshape(batch * seq, RET_WIDTH), s


def _attn_sample_kernel(pt_ref, q_ref, kn_ref, vn_ref, rel_ref, slope_ref, lamv_ref, subg_ref,
                        *rest, pages_per_step, past_len, n_steps):
    g_pages = pages_per_step
    k_refs = rest[:g_pages]
    v_refs = rest[g_pages:2 * g_pages]
    o_ref = rest[2 * g_pages]
    s_sc, wnew_sc, acc_sc = rest[2 * g_pages + 1:]
    j = pl.program_id(1)
    rows = 2 * DA_HEADS

    def q_rows():
        col_group = lax.broadcasted_iota(I32, (rows, DA_WIDTH), 1) // DA_HD
        row_id = lax.broadcasted_iota(I32, (rows, DA_WIDTH), 0)
        want = 2 * (row_id % DA_HEADS) + row_id // DA_HEADS
        qb = jnp.broadcast_to(q_ref[0], (rows, DA_WIDTH))
        return jnp.where(col_group == want, qb, 0.0)

    @pl.when(j < n_steps)
    def _():
        qr = q_rows().astype(BF16)
        for g in range(g_pages):
            page = j * g_pages + g
            s = jnp.dot(qr, k_refs[g][0].astype(BF16), preferred_element_type=F32)
            base = (page * LANES - past_len).astype(F32)
            s_sc[page] = s + (rel_ref[...] + slope_ref[...] * base)

    @pl.when(j == n_steps - 1)
    def _():
        s_all = s_sc[...]
        s_new = jnp.sum(q_rows() * kn_ref[0], axis=-1, keepdims=True)
        m = jnp.maximum(jnp.max(jnp.max(s_all, axis=0), axis=-1, keepdims=True), s_new)
        p = jnp.exp(s_all - m)
        p_new = jnp.exp(s_new - m)
        l = jnp.sum(jnp.sum(p, axis=0), axis=-1, keepdims=True) + p_new
        wn = p / l
        wn_new = p_new / l
        lam = _lambda_value(lamv_ref)
        s_sc[:, 0:DA_HEADS, :] = wn[:, 0:DA_HEADS, :] - lam * wn[:, DA_HEADS:rows, :]
        s_sc[:, DA_HEADS:rows, :] = jnp.zeros((s_sc.shape[0], DA_HEADS, LANES), F32)
        wnew_sc[...] = wn_new[0:DA_HEADS] - lam * wn_new[DA_HEADS:rows]
        acc_sc[...] = jnp.zeros(acc_sc.shape, F32)

    @pl.when(j >= n_steps)
    def _():
        lane = lax.broadcasted_iota(I32, (rows, LANES), 1)
        own = lane % DA_HEADS == lax.broadcasted_iota(I32, (rows, LANES), 0)
        acc = acc_sc[...]
        for g in range(g_pages):
            page = (j - n_steps) * g_pages + g
            w = s_sc[page]
            w_rows = jnp.concatenate(
                [jnp.where(own, jnp.take_along_axis(w, (LANES // DA_HEADS) * c + lane // DA_HEADS,
                                                    axis=1), 0.0)
                 for c in range(DA_HEADS)], axis=-1)
            acc = acc + jnp.dot(w_rows.astype(BF16), v_refs[g][0].astype(BF16),
                                preferred_element_type=F32)
        acc_sc[...] = acc

    @pl.when(j == 2 * n_steps - 1)
    def _():
        acc = acc_sc[...]
        w_new = wnew_sc[...]
        v_new = vn_ref[0]
        outs = []
        for h in range(DA_HEADS):
            vs = slice(h * DA_VD, (h + 1) * DA_VD)
            o = acc[h:h + 1, :] + w_new[h:h + 1, :] * v_new[:, vs]
            outs.append(_rms(o) * subg_ref[...] * (1.0 - LAM_INIT))
        o_ref[0] = jnp.concatenate(outs, axis=-1)


def _attn_sample(q, k_new, v_new, cache_kt, cache_vr, page_table, lamv, subg, *, pages_per_step):
    nb, n_pages = page_table.shape
    page = cache_kt.shape[2]
    assert page == LANES
    past_len = n_pages * page
    g_pages = pages_per_step
    n_steps = n_pages // g_pages
    rows = 2 * DA_HEADS
    slope_rows = np.tile(_alibi_slopes(), 2)[:, None]
    rel = slope_rows * np.arange(page, dtype=np.float32)[None, :]
    pt = page_table.reshape(-1).astype(I32)
    row3 = lambda: pl.BlockSpec((1, 1, DA_WIDTH), lambda b, j, pt: (b, 0, 0))
    const2 = lambda shape: pl.BlockSpec(shape, lambda b, j, pt: (0, 0))

    def k_spec(g):
        return pl.BlockSpec(
            (1, DA_WIDTH, page),
            lambda b, j, pt: (pt[b * n_pages + jnp.minimum(j, n_steps - 1) * g_pages + g], 0, 0))

    def v_spec(g):
        return pl.BlockSpec(
            (1, page * DA_HEADS, DA_VD),
            lambda b, j, pt: (pt[b * n_pages + jnp.maximum(j - n_steps, 0) * g_pages + g], 0, 0))

    grid_spec = pltpu.PrefetchScalarGridSpec(
        num_scalar_prefetch=1,
        grid=(nb, 2 * n_steps),
        in_specs=[row3(), row3(), row3(), const2((rows, page)), const2((rows, 1)),
                  const2((4, DA_HD)), const2((1, DA_VD))]
                 + [k_spec(g) for g in range(g_pages)] + [v_spec(g) for g in range(g_pages)],
        out_specs=row3(),
        scratch_shapes=[pltpu.VMEM((n_pages, rows, page), F32), pltpu.VMEM((DA_HEADS, 1), F32),
                        pltpu.VMEM((rows, DA_VD), F32)],
    )
    out = pl.pallas_call(
        functools.partial(_attn_sample_kernel, pages_per_step=g_pages, past_len=past_len,
                          n_steps=n_steps),
        grid_spec=grid_spec,
        out_shape=jax.ShapeDtypeStruct((nb, 1, DA_WIDTH), F32),
        compiler_params=pltpu.CompilerParams(dimension_semantics=("parallel", "arbitrary"),
                                             vmem_limit_bytes=VMEM_LIMIT),
        name="attn_sample",
    )(pt, q.reshape(nb, 1, DA_WIDTH), k_new.reshape(nb, 1, DA_WIDTH), v_new.reshape(nb, 1, DA_WIDTH),
      jnp.asarray(rel), jnp.asarray(slope_rows), lamv, subg,
      *([cache_kt] * g_pages), *([cache_vr] * g_pages))
    return out.reshape(nb, DA_WIDTH)


def _ret_sample_kernel(q_ref, k_ref, qc_ref, kc_ref, v_ref, g_ref, s_ref, qdec_ref, sdec_ref,
                       r_ref, so_ref):
    q = q_ref[0]
    k = k_ref[0]
    qd_col = _bf(qc_ref[0] * qdec_ref[...])
    k_col = kc_ref[0]
    outs = []
    for h in range(RET_HEADS):
        ks = slice(h * RET_DK, (h + 1) * RET_DK)
        vs = slice(h * RET_DV, (h + 1) * RET_DV)
        vh = v_ref[0, :, vs]
        s_old = s_ref[0, h]
        qk = jnp.sum(q[:, ks] * k[:, ks], axis=-1, keepdims=True)
        o = qk * vh + jnp.sum(qd_col[ks, :] * _bf(s_old), axis=0, keepdims=True)
        so_ref[0, h] = sdec_ref[h] * s_old + k_col[ks, :] * vh
        outs.append(_rms(o) * _silu(g_ref[0, :, vs]))
    r_ref[0] = jnp.concatenate(outs, axis=-1)


def _ret_sample(qr, kr, vr, gr, state):
    nb = qr.shape[0]
    _, qdec, _, sdec = _ret_tables(1)
    row3 = lambda w: pl.BlockSpec((1, 1, w), lambda b: (b, 0, 0))
    col3 = pl.BlockSpec((1, RET_QK_WIDTH, 1), lambda b: (b, 0, 0))
    st = pl.BlockSpec((1, RET_HEADS, RET_DK, RET_DV), lambda b: (b, 0, 0, 0))
    r, s_new = pl.pallas_call(
        _ret_sample_kernel,
        grid=(nb,),
        in_specs=[row3(RET_QK_WIDTH), row3(RET_QK_WIDTH), col3, col3, row3(RET_WIDTH),
                  row3(RET_WIDTH), st,
                  pl.BlockSpec((RET_QK_WIDTH, 1), lambda b: (0, 0)),
                  pl.BlockSpec((RET_HEADS, RET_DK, RET_DV), lambda b: (0, 0, 0))],
        out_specs=(row3(RET_WIDTH), st),
        out_shape=(jax.ShapeDtypeStruct((nb, 1, RET_WIDTH), F32),
                   jax.ShapeDtypeStruct((nb, RET_HEADS, RET_DK, RET_DV), F32)),
        compiler_params=pltpu.CompilerParams(dimension_semantics=("parallel",)),
        name="ret_sample",
    )(qr.reshape(nb, 1, -1), kr.reshape(nb, 1, -1), qr.reshape(nb, -1, 1), kr.reshape(nb, -1, 1),
      vr.reshape(nb, 1, -1), gr.reshape(nb, 1, -1), state, qdec.reshape(-1, 1), sdec)
    return r.reshape(nb, RET_WIDTH), s_new


def _outproj_kernel(a_ref, r_ref, x_ref, wo_ref, g_ref, wr_ref, br_ref,
                    x1_ref, hfp_ref, ti_ref, tg_ref, *, a_transposed):
    if a_transposed:
        mix_a = lax.dot_general(a_ref[0], wo_ref[:DA_WIDTH, :], (((0,), (0,)), ((), ())),
                                preferred_element_type=F32)
    else:
        mix_a = _mm(a_ref[...], wo_ref[:DA_WIDTH, :])
    mix = mix_a + _mm(r_ref[...], wo_ref[DA_WIDTH:, :])
    x1 = x_ref[...] + mix
    x1_ref[...] = x1
    hf = _rms(x1) * g_ref[...]
    hfp_ref[0] = _pack_pairs(hf[:, :HALF])
    hfp_ref[1] = _pack_pairs(hf[:, HALF:])
    logits = _mm(hf, wr_ref[...]) + br_ref[...]
    lane = lax.broadcasted_iota(I32, logits.shape, 1)
    work = logits
    top_v, top_i = [], []
    for _ in range(TOP_K):
        m = jnp.max(work, axis=-1, keepdims=True)
        idx = jnp.min(jnp.where(work == m, lane, LANES), axis=-1, keepdims=True)
        top_v.append(m)
        top_i.append(idx)
        work = jnp.where(lane == idx, -jnp.inf, work)
    es = [jnp.exp(v - top_v[0]) for v in top_v]
    denom = es[0] + es[1] + es[2] + es[3]
    ti = jnp.zeros(logits.shape, I32)
    tg = jnp.zeros(logits.shape, F32)
    for kk in range(TOP_K):
        ti = jnp.where(lane == kk, top_i[kk], ti)
        tg = jnp.where(lane == kk, es[kk] / denom, tg)
    ti_ref[...] = ti
    tg_ref[...] = tg


def _outproj(a, r, x, w_out, g, w_router, b_router, *, tm):
    n = x.shape[0]
    row = lambda w: pl.BlockSpec((tm, w), lambda i: (i, 0))
    const = lambda shape: pl.BlockSpec(shape, lambda i: (0, 0))
    a_transposed = a.ndim == 3
    if a_transposed:
        n_s = a.shape[2] // tm
        a_spec = pl.BlockSpec((1, DA_WIDTH, tm), lambda i: (i // n_s, 0, i % n_s))
    else:
        a_spec = row(DA_WIDTH)
    return pl.pallas_call(
        functools.partial(_outproj_kernel, a_transposed=a_transposed),
        grid=(n // tm,),
        in_specs=[a_spec, row(RET_WIDTH), row(D_MODEL), const((D_MODEL, D_MODEL)),
                  const((1, D_MODEL)), const((D_MODEL, LANES)), const((1, LANES))],
        out_specs=(row(D_MODEL), pl.BlockSpec((2, tm, QUARTER), lambda i: (0, i, 0)),
                   row(LANES), row(LANES)),
        out_shape=(jax.ShapeDtypeStruct((n, D_MODEL), F32),
                   jax.ShapeDtypeStruct((2, n, QUARTER), I32),
                   jax.ShapeDtypeStruct((n, LANES), I32),
                   jax.ShapeDtypeStruct((n, LANES), F32)),
        compiler_params=pltpu.CompilerParams(dimension_semantics=("parallel",),
                                             vmem_limit_bytes=VMEM_LIMIT),
        name="outproj",
    )(a, r, x, w_out, g, w_router, b_router)


def _sc_gather(x, idx):
    m = idx.shape[0]
    cols = x.shape[1]
    mesh = plsc.VectorSubcoreMesh(core_axis_name="c", subcore_axis_name="s")

    @pl.kernel(out_type=jax.ShapeDtypeStruct((m, cols), x.dtype), mesh=mesh)
    def gather_kernel(x_hbm, i_hbm, o_hbm):
        def body(i_vmem, o_vmem):
            pltpu.sync_copy(x_hbm.at[i_vmem.at[0]], o_vmem)

        pltpu.emit_pipeline(
            body,
            grid=(m // SC_WINDOW,),
            in_specs=[pl.BlockSpec((1, SC_WINDOW), lambda i: (0, i))],
            out_specs=[pl.BlockSpec((SC_WINDOW, cols), lambda i: (i, 0))],
            core_axis_name=("c", "s"),
            dimension_semantics=(pltpu.PARALLEL,),
        )(i_hbm, o_hbm)

    return gather_kernel(x, idx.reshape(1, m))


def _filler(n, modulus):
    return jnp.arange(n, dtype=I32) % modulus


def _sc_scatter(x, idx, out_rows):
    m = idx.shape[0]
    cols = x.shape[1]
    n_blk = x.shape[0] // 2 // SC_WINDOW
    mesh = plsc.VectorSubcoreMesh(core_axis_name="c", subcore_axis_name="s")

    @pl.kernel(out_type=jax.ShapeDtypeStruct((out_rows, cols), x.dtype), mesh=mesh)
    def scatter_kernel(x_hbm, i_hbm, o_hbm):
        def body(x_vmem, i_vmem):
            pltpu.sync_copy(x_vmem, o_hbm.at[i_vmem.at[0]])

        pltpu.emit_pipeline(
            body,
            grid=(m // SC_WINDOW,),
            in_specs=[pl.BlockSpec((SC_WINDOW, cols),
                                   lambda i: ((i // (TOP_K * n_blk)) * n_blk + i % n_blk, 0)),
                      pl.BlockSpec((1, SC_WINDOW), lambda i: (0, i))],
            out_specs=[],
            core_axis_name=("c", "s"),
            dimension_semantics=(pltpu.PARALLEL,),
        )(x_hbm, i_hbm)

    return scatter_kernel(x, idx.reshape(1, m))


def _rank_kernel(ti_ref, tri_ref, rank_ref, cnt_ref, base_sc):
    @pl.when(pl.program_id(0) == 0)
    def _():
        base_sc[...] = jnp.zeros(base_sc.shape, F32)

    ti = ti_ref[...]
    lane = lax.broadcasted_iota(I32, ti.shape, 1)
    picks = [lane == ti[:, k:k + 1] for k in range(TOP_K)]
    onehot = jnp.zeros(ti.shape, F32)
    for pk in picks:
        onehot = onehot + jnp.where(pk, 1.0, 0.0)
    pos = jnp.dot(tri_ref[...], onehot.astype(BF16), preferred_element_type=F32) + base_sc[...]
    rank = jnp.zeros(ti.shape, I32)
    for k, pk in enumerate(picks):
        r_k = jnp.sum(jnp.where(pk, pos, 0.0), axis=-1, keepdims=True)
        rank = jnp.where(lane == k, r_k.astype(I32), rank)
    rank_ref[...] = rank
    base_sc[...] = base_sc[...] + jnp.sum(onehot, axis=0, keepdims=True)
    cnt_ref[...] = base_sc[...]


def _rank(ti, *, tr):
    n = ti.shape[0]
    tri = jnp.asarray(np.tril(np.ones((tr, tr), np.float32), -1), BF16)
    return pl.pallas_call(
        _rank_kernel,
        grid=(n // tr,),
        in_specs=[pl.BlockSpec((tr, LANES), lambda i: (i, 0)),
                  pl.BlockSpec((tr, tr), lambda i: (0, 0))],
        out_specs=(pl.BlockSpec((tr, LANES), lambda i: (i, 0)),
                   pl.BlockSpec((1, LANES), lambda i: (0, 0))),
        out_shape=(jax.ShapeDtypeStruct((n, LANES), I32), jax.ShapeDtypeStruct((1, LANES), F32)),
        scratch_shapes=[pltpu.VMEM((1, LANES), F32)],
        compiler_params=pltpu.CompilerParams(dimension_semantics=("arbitrary",)),
        name="route_rank",
    )(ti, tri)


def _prep_expert_weights(wup_ref, wdn_ref, wupb, wdnb):
    lane = lax.broadcasted_iota(I32, (PREP_ROWS, LANES), 1)
    first_half = lane < LANES // 2
    idx_even = (2 * lane) % LANES
    idx_odd = (2 * lane + 1) % LANES

    def body(r, carry):
        rows = pl.ds(pl.multiple_of(r * PREP_ROWS, PREP_ROWS), PREP_ROWS)
        for ct in range(D_FF // LANES):
            a = wup_ref[0, rows, 2 * ct * LANES:(2 * ct + 1) * LANES]
            b = wup_ref[0, rows, (2 * ct + 1) * LANES:(2 * ct + 2) * LANES]
            even = jnp.where(first_half, jnp.take_along_axis(a, idx_even, axis=1),
                             jnp.take_along_axis(b, idx_even, axis=1))
            odd = jnp.where(first_half, jnp.take_along_axis(a, idx_odd, axis=1),
                            jnp.take_along_axis(b, idx_odd, axis=1))
            wupb[rows, ct * LANES:(ct + 1) * LANES] = even.astype(BF16)
            wupb[rows, D_FF + ct * LANES:D_FF + (ct + 1) * LANES] = odd.astype(BF16)
        wdnb[rows, :] = wdn_ref[0, rows, :].astype(BF16)
        return carry

    lax.fori_loop(0, D_MODEL // PREP_ROWS, body, 0)


def _moe_kernel(be_ref, na_ref, nv_ref, xs_ref, wup_ref, bup_ref, wdn_ref, bdn_ref, out_ref,
                wupb, wdnb):
    i = pl.program_id(0)
    n_act = na_ref[0]
    e = be_ref[jnp.minimum(i, n_act - 1)]
    e_prev = be_ref[jnp.maximum(i - 1, 0)]

    @pl.when(jnp.logical_and(i < n_act, jnp.logical_or(i == 0, e != e_prev)))
    def _():
        _prep_expert_weights(wup_ref, wdn_ref, wupb, wdnb)

    @pl.when(i < n_act)
    def _():
        lo0, hi0 = _unpack_pairs(xs_ref[0])
        lo1, hi1 = _unpack_pairs(xs_ref[1])
        x = jnp.concatenate([lo0, hi0, lo1, hi1], axis=-1)
        live = lax.broadcasted_iota(I32, x.shape, 0) < nv_ref[i]
        x = jnp.where(live, x, 0.0).astype(BF16)
        u = jnp.dot(x, wupb[...], preferred_element_type=F32) + bup_ref[0]
        g = jnp.minimum(u[:, :D_FF], SWIGLU_LIMIT)
        lin = jnp.clip(u[:, D_FF:], -SWIGLU_LIMIT, SWIGLU_LIMIT)
        y = g * (1.0 / (1.0 + jnp.exp(-SWIGLU_ALPHA * g))) * (lin + 1.0)
        o = jnp.dot(y.astype(BF16), wdnb[...], preferred_element_type=F32) + bdn_ref[0]
        out_ref[0] = _pack_pairs(o[:, :HALF])
        out_ref[1] = _pack_pairs(o[:, HALF:])

    @pl.when(i >= n_act)
    def _():
        out_ref[...] = jnp.zeros(out_ref.shape, I32)


def _moe(block_e, n_active, n_valid, xs, w_up, b_up_d, w_down, b_down):
    n_blocks = block_e.shape[0]
    n_slots = n_blocks * MOE_TM

    def act(i, na):
        return jnp.minimum(i, na[0] - 1)

    grid_spec = pltpu.PrefetchScalarGridSpec(
        num_scalar_prefetch=3,
        grid=(n_blocks,),
        in_specs=[
            pl.BlockSpec((2, MOE_TM, QUARTER), lambda i, be, na, nv: (0, act(i, na), 0)),
            pl.BlockSpec((1, D_MODEL, 2 * D_FF), lambda i, be, na, nv: (be[act(i, na)], 0, 0)),
            pl.BlockSpec((1, 1, 2 * D_FF), lambda i, be, na, nv: (be[act(i, na)], 0, 0)),
            pl.BlockSpec((1, D_FF, D_MODEL), lambda i, be, na, nv: (be[act(i, na)], 0, 0)),
            pl.BlockSpec((1, 1, D_MODEL), lambda i, be, na, nv: (be[act(i, na)], 0, 0)),
        ],
        out_specs=pl.BlockSpec((2, MOE_TM, QUARTER), lambda i, be, na, nv: (0, i, 0)),
        scratch_shapes=[pltpu.VMEM((D_MODEL, 2 * D_FF), BF16), pltpu.VMEM((D_FF, D_MODEL), BF16)],
    )
    return pl.pallas_call(
        _moe_kernel,
        grid_spec=grid_spec,
        out_shape=jax.ShapeDtypeStruct((2, n_slots, QUARTER), I32),
        compiler_params=pltpu.CompilerParams(dimension_semantics=("arbitrary",),
                                             vmem_limit_bytes=VMEM_LIMIT),
        name="moe_ffn",
    )(block_e, n_active, n_valid, xs, w_up, b_up_d, w_down, b_down)


def _slot_layout(counts, top_i, rank, n_blocks):
    experts = jnp.arange(N_EXPERTS, dtype=I32)
    padded = (counts + MOE_TM - 1) // MOE_TM * MOE_TM
    pad_end = jnp.cumsum(padded)
    pad_start = pad_end - padded
    pick = top_i[:, :, None] == experts[None, None, :]
    dest = rank + jnp.sum(jnp.where(pick, pad_start[None, None, :], 0), axis=-1)
    blk_start = jnp.arange(n_blocks, dtype=I32) * MOE_TM
    block_e = jnp.minimum(jnp.sum((blk_start[:, None] >= pad_end[None, :]).astype(I32), axis=1),
                          N_EXPERTS - 1)
    live = jnp.clip(counts[block_e] - (blk_start - pad_start[block_e]), 0, MOE_TM)
    n_active = (pad_end[-1] // MOE_TM).astype(I32).reshape(1)
    return dest.astype(I32), block_e.astype(I32), live.astype(I32), n_active


def _combine_kernel(yg_ref, tg_ref, x1_ref, g_ref, o_ref):
    tg = tg_ref[...]
    parts = [jnp.zeros((x1_ref.shape[0], QUARTER), F32) for _ in range(4)]
    for kk in range(TOP_K):
        gate = tg[:, kk:kk + 1]
        lo0, hi0 = _unpack_pairs(yg_ref[0, kk])
        lo1, hi1 = _unpack_pairs(yg_ref[1, kk])
        for j, piece in enumerate((lo0, hi0, lo1, hi1)):
            parts[j] = parts[j] + gate * piece
    x2 = x1_ref[...] + jnp.concatenate(parts, axis=-1)
    o_ref[...] = _rms(x2) * g_ref[...]


def _combine(yg, tg, x1, g, *, tm, row_offset):
    n = x1.shape[0]
    blk0 = row_offset // tm
    return pl.pallas_call(
        _combine_kernel,
        grid=(n // tm,),
        in_specs=[pl.BlockSpec((2, TOP_K, tm, QUARTER), lambda i: (0, 0, i + blk0, 0)),
                  pl.BlockSpec((tm, LANES), lambda i: (i, 0)),
                  pl.BlockSpec((tm, D_MODEL), lambda i: (i, 0)),
                  pl.BlockSpec((1, D_MODEL), lambda i: (0, 0))],
        out_specs=pl.BlockSpec((tm, D_MODEL), lambda i: (i, 0)),
        out_shape=jax.ShapeDtypeStruct((n, D_MODEL), F32),
        compiler_params=pltpu.CompilerParams(dimension_semantics=("parallel",),
                                             vmem_limit_bytes=VMEM_LIMIT),
        name="combine",
    )(yg, tg, x1, g)


def _extend_w_in(w):
    swap = np.arange(RET_QK_WIDTH).reshape(RET_HEADS, 2, RET_DK // 2)[:, ::-1, :].reshape(-1)
    qa, ka, va = w[:, 0:512], w[:, 512:1024], w[:, 1024:1536]
    qr, kr = w[:, 1536:1792], w[:, 1792:2048]
    vr, gr = w[:, 2048:2560], w[:, 2560:3072]
    sa = DA_HD ** -0.5
    sr = RET_DK ** -0.5
    return jnp.concatenate([qa * sa, ka, va, qr, qr[:, swap], kr * sr, kr[:, swap] * sr, vr, gr], axis=1)


def _rotary_tables(pos):
    half = RET_DK // 2
    inv = ROPE_BASE ** (-jnp.arange(half, dtype=F32) / half)
    ang = pos.astype(F32)[:, None] * inv[None, :]
    cos = jnp.cos(ang)
    sin = jnp.sin(ang)
    cos_t = jnp.tile(jnp.concatenate([cos, cos], axis=1), (1, RET_HEADS))
    sin_t = jnp.tile(jnp.concatenate([-sin, sin], axis=1), (1, RET_HEADS))
    return cos_t, sin_t


def kernel(x_prompt, x_sample, cache_k, cache_v, state_ret, page_table, norm_attn_g, w_in, lam_q1,
           lam_k1, lam_q2, lam_k2, da_subln_g, w_out, norm_ffn_g, w_router, b_router, w_up, b_up,
           w_down, b_down, norm_final_g):
    return _forward(x_prompt, x_sample, cache_k, cache_v, state_ret, page_table, norm_attn_g, w_in,
                    lam_q1, lam_k1, lam_q2, lam_k2, da_subln_g, w_out, norm_ffn_g, w_router,
                    b_router, w_up, b_up, w_down, b_down, norm_final_g,
                    tm=512, tq=512, pages_per_step=32, gather=_sc_gather, scatter=_sc_scatter)


def _forward(x_prompt, x_sample, cache_k, cache_v, state_ret, page_table, norm_attn_g, w_in, lam_q1,
             lam_k1, lam_q2, lam_k2, da_subln_g, w_out, norm_ffn_g, w_router, b_router, w_up, b_up,
             w_down, b_down, norm_final_g, *, tm, tq, pages_per_step, gather, scatter):
    batch, seq, _ = x_prompt.shape
    nb = x_sample.shape[0]
    n_pool, page = cache_k.shape[1], cache_k.shape[2]
    past_len = page_table.shape[1] * page
    n_prompt = batch * seq

    w_ext_b = _extend_w_in(w_in[0]).astype(BF16)
    w_ext_t = w_ext_b[:, :_C_QR].T
    g_attn = norm_attn_g[0].reshape(1, D_MODEL)
    g_ffn = norm_ffn_g[0].reshape(1, D_MODEL)
    g_fin = norm_final_g.reshape(1, D_MODEL)
    lamv = jnp.stack([lam_q1[0], lam_k1[0], lam_q2[0], lam_k2[0]]).astype(F32)
    subg = da_subln_g[0].reshape(1, DA_VD)
    w_o_b = w_out[0].astype(BF16)
    w_r = jnp.pad(w_router[0], ((0, 0), (0, LANES - N_EXPERTS))).astype(BF16)
    b_r = jnp.pad(b_router[0], (0, LANES - N_EXPERTS), constant_values=NEG_INF).reshape(1, LANES)
    assert cache_k.shape[0] == 1 and w_up.shape[0] == 1
    w_up_e = w_up.reshape(N_EXPERTS, D_MODEL, 2 * D_FF)
    b_up_d = jnp.concatenate([b_up[0][:, 0::2], b_up[0][:, 1::2]], axis=-1).reshape(N_EXPERTS, 1, 2 * D_FF)
    w_down_e = w_down.reshape(N_EXPERTS, D_FF, D_MODEL)
    b_dn = b_down.reshape(N_EXPERTS, 1, D_MODEL)
    cos_p, sin_p = _rotary_tables(jnp.arange(seq))
    cos_s, sin_s = _rotary_tables(past_len + jnp.zeros((nb,), I32))

    xp = x_prompt.reshape(n_prompt, D_MODEL)
    qat, kat, kab, va, vat, qr, kr, vr, gr = _inproj(xp, g_attn, w_ext_b, w_ext_t, cos_p, sin_p,
                                                     tm=tm, seq=seq)
    a_p = _attn_prompt(qat, kab, vat, lamv, subg, batch=batch, seq=seq, tq=tq)
    r_p, ret_p = _ret_prompt(qr, kr, vr, gr, batch=batch, seq=seq)
    x1_p, hfp_p, ti_p, tg_p = _outproj(a_p, r_p, xp, w_o_b, g_ffn, w_r, b_r, tm=tm)

    xs_ = x_sample.reshape(nb, D_MODEL)
    qa_s, ka_s, va_s, qr_s, kr_s, vr_s, gr_s = _inproj(xs_, g_attn, w_ext_b, w_ext_t, cos_s, sin_s,
                                                       tm=nb)
    ckt = jnp.transpose(cache_k.reshape(n_pool, page, DA_WIDTH), (0, 2, 1))
    cvr = cache_v.reshape(n_pool, page * DA_HEADS, DA_VD)
    a_s = _attn_sample(qa_s, ka_s, va_s, ckt, cvr, page_table, lamv, subg,
                       pages_per_step=pages_per_step)
    r_s, ret_s = _ret_sample(qr_s, kr_s, vr_s, gr_s,
                             state_ret.reshape(nb, RET_HEADS, RET_DK, RET_DV))
    x1_s, hfp_s, ti_s, tg_s = _outproj(a_s, r_s, xs_, w_o_b, g_ffn, w_r, b_r, tm=nb)

    n_tok = n_prompt + nb
    tok_pad = -(-n_tok // ROUTE_TILE) * ROUTE_TILE
    pad = tok_pad - n_tok
    hfp = jnp.concatenate([hfp_p, hfp_s, jnp.zeros((2, pad, QUARTER), I32)], axis=1)
    ti_all = jnp.concatenate([ti_p, ti_s, jnp.full((pad, LANES), LANES - 1, I32)], axis=0)
    rank, cnt = _rank(ti_all, tr=ROUTE_TILE)
    counts = cnt[0, :N_EXPERTS].astype(I32)
    n_blocks = -(-n_tok * TOP_K // MOE_TM) + N_EXPERTS
    n_slots = n_blocks * MOE_TM
    dest, block_e, n_live, n_active = _slot_layout(counts, ti_all[:n_tok, :TOP_K],
                                                   rank[:n_tok, :TOP_K], n_blocks)
    slots_pad = -(-(n_slots + 1) // (SC_UNIT // 2)) * (SC_UNIT // 2)
    dump = n_slots + _filler(pad, slots_pad - n_slots)
    dest_sc = jnp.concatenate([dest, jnp.broadcast_to(dump[:, None], (pad, TOP_K))], axis=0).T
    xs_sorted = scatter(hfp.reshape(2 * tok_pad, QUARTER),
                        jnp.concatenate([dest_sc, slots_pad + dest_sc]).reshape(-1), 2 * slots_pad)
    out_sorted = _moe(block_e, n_active, n_live, xs_sorted.reshape(2, slots_pad, QUARTER),
                      w_up_e, b_up_d, w_down_e, b_dn)
    fill = jnp.broadcast_to(_filler(pad, n_slots)[:, None], (pad, TOP_K))
    dest_g = jnp.concatenate([dest, fill], axis=0).T
    yg = gather(out_sorted.reshape(2 * n_slots, QUARTER),
                jnp.concatenate([dest_g, n_slots + dest_g]).reshape(-1))
    yg = yg.reshape(2, TOP_K, tok_pad, QUARTER)
    y_p = _combine(yg, tg_p, x1_p, g_fin, tm=tm, row_offset=0)
    y_s = _combine(yg, tg_s, x1_s, g_fin, tm=nb, row_offset=n_prompt)

    depth = 1
    return (y_p.reshape(batch, seq, D_MODEL),
            y_s.reshape(nb, 1, D_MODEL),
            jnp.transpose(kat.reshape(depth, batch, DA_HEADS, 2, DA_HD, seq), (0, 1, 5, 2, 3, 4)),
            va.reshape(depth, batch, seq, DA_HEADS, DA_VD),
            ret_p.reshape(depth, batch, RET_HEADS, RET_DK, RET_DV),
            ka_s.reshape(depth, nb, 1, DA_HEADS, 2, DA_HD),
            va_s.reshape(depth, nb, 1, DA_HEADS, DA_VD),
            ret_s.reshape(depth, nb, RET_HEADS, RET_DK, RET_DV))
```

```python
import functools
import math

import numpy as np
import jax
import jax.numpy as jnp
from jax import lax
from jax.experimental import pallas as pl
from jax.experimental.pallas import tpu as pltpu
from jax.experimental.pallas import tpu_sc as plsc

F32 = jnp.float32
BF16 = jnp.bfloat16
I32 = jnp.int32

D_MODEL = 1024
DA_HEADS = 4
DA_VD = 128
DA_HD = 64
DA_WIDTH = DA_HEADS * DA_VD
RET_HEADS = 4
RET_DV = 128
RET_DK = 64
RET_WIDTH = RET_HEADS * RET_DV
RET_QK_WIDTH = RET_HEADS * RET_DK
RET_CHUNK = 128
ROPE_BASE = 10000.0
N_EXPERTS = 32
TOP_K = 4
D_FF = 1024
SWIGLU_LIMIT = 7.0
SWIGLU_ALPHA = 1.702
NORM_EPS = 1e-5
NEG_INF = -1e30
LAM_INIT = 0.8 - 0.6 * math.exp(-0.3 * 0)
LOG2_E = math.log2(math.e)
LANES = 128
HALF = D_MODEL // 2
QUARTER = D_MODEL // 4
MOE_TM = 512
PREP_ROWS = 64
ROUTE_TILE = 512
SC_WINDOW = 128
SC_SUBCORES = 32
SC_UNIT = SC_WINDOW * SC_SUBCORES
VMEM_LIMIT = 56 * 1024 * 1024

_C_QA, _C_KA, _C_VA = 0, 512, 1024
_C_QR, _C_QRS, _C_KR, _C_KRS = 1536, 1792, 2048, 2304
_C_VR, _C_GR, _C_END = 2560, 3072, 3584


def _mm(a, b):
    return jnp.dot(a.astype(BF16), b.astype(BF16), preferred_element_type=F32)


def _bf(x):
    return x.astype(BF16).astype(F32)


def _rms(x):
    return x * lax.rsqrt(jnp.mean(x * x, axis=-1, keepdims=True) + NORM_EPS)


def _silu(g):
    return g / (1.0 + jnp.exp(-g))


def _pack_pairs(x):
    w = x.shape[-1] // 2
    lo = lax.bitcast_convert_type(x[:, :w].astype(BF16).astype(F32), jnp.uint32)
    hi = lax.bitcast_convert_type(x[:, w:].astype(BF16).astype(F32), jnp.uint32)
    return lax.bitcast_convert_type(hi | (lo >> 16), I32)


def _unpack_pairs(p):
    u = lax.bitcast_convert_type(p, jnp.uint32)
    lo = lax.bitcast_convert_type(u << 16, F32)
    hi = lax.bitcast_convert_type(u & jnp.uint32(0xFFFF0000), F32)
    return lo, hi


def _alibi_slopes():
    return np.asarray([2.0 ** (-8.0 * (h + 1) / DA_HEADS) for h in range(DA_HEADS)], np.float32)


def _inproj_kernel(x_ref, g_ref, w_ref, wt_ref, cos_ref, sin_ref, *outs, transposed):
    hm = (_rms(x_ref[...]) * g_ref[...]).astype(BF16)

    def mm(lo, hi):
        return _mm(hm, w_ref[:, lo:hi])

    def mm_t(lo, hi):
        return lax.dot_general(wt_ref[lo:hi, :], hm, (((1,), (1,)), ((), ())),
                               preferred_element_type=F32)

    if transposed:
        qat_ref, kat_ref, kab_ref, va_ref, vat_ref, qr_ref, kr_ref, vr_ref, gr_ref = outs
        qat_ref[0] = mm_t(_C_QA, _C_KA).astype(BF16)
        kat_ref[0] = mm_t(_C_KA, _C_VA)
        kab_ref[...] = mm(_C_KA, _C_VA).astype(BF16)
        va = mm(_C_VA, _C_QR)
        for h in range(DA_HEADS):
            va_ref[pl.ds(h, va.shape[0], stride=DA_HEADS), :] = va[:, h * DA_VD:(h + 1) * DA_VD]
        vat_ref[0] = mm_t(_C_VA, _C_QR).astype(BF16)
    else:
        qa_ref, ka_ref, va_ref, qr_ref, kr_ref, vr_ref, gr_ref = outs
        qa_ref[...] = mm(_C_QA, _C_KA)
        ka_ref[...] = mm(_C_KA, _C_VA)
        va_ref[...] = mm(_C_VA, _C_QR)
    cos = cos_ref[...]
    sin = sin_ref[...]
    qr_ref[...] = (mm(_C_QR, _C_QRS) * cos + mm(_C_QRS, _C_KR) * sin).astype(qr_ref.dtype)
    kr_ref[...] = (mm(_C_KR, _C_KRS) * cos + mm(_C_KRS, _C_VR) * sin).astype(kr_ref.dtype)
    vr_ref[...] = mm(_C_VR, _C_GR).astype(vr_ref.dtype)
    gr_ref[...] = mm(_C_GR, _C_END).astype(gr_ref.dtype)


def _inproj(x, g, w_ext, w_ext_t, cos_t, sin_t, *, tm, seq=None):
    n = x.shape[0]
    n_tab = cos_t.shape[0] // tm
    transposed = seq is not None
    act = BF16 if transposed else F32
    row = lambda w: pl.BlockSpec((tm, w), lambda i: (i, 0))
    tab = pl.BlockSpec((tm, RET_QK_WIDTH), lambda i: (i % n_tab, 0))
    sds = jax.ShapeDtypeStruct
    tail_shapes = (sds((n, RET_QK_WIDTH), act), sds((n, RET_QK_WIDTH), act),
                   sds((n, RET_WIDTH), act), sds((n, RET_WIDTH), act))
    tail_specs = (row(RET_QK_WIDTH), row(RET_QK_WIDTH), row(RET_WIDTH), row(RET_WIDTH))
    if transposed:
        n_s = seq // tm
        tr = pl.BlockSpec((1, DA_WIDTH, tm), lambda i: (i // n_s, 0, i % n_s))
        head_shapes = (sds((n // seq, DA_WIDTH, seq), BF16), sds((n // seq, DA_WIDTH, seq), F32),
                       sds((n, DA_WIDTH), BF16), sds((n * DA_HEADS, DA_VD), F32),
                       sds((n // seq, DA_WIDTH, seq), BF16))
        head_specs = (tr, tr, row(DA_WIDTH),
                      pl.BlockSpec((tm * DA_HEADS, DA_VD), lambda i: (i, 0)), tr)
    else:
        head_shapes = (sds((n, DA_WIDTH), F32),) * 3
        head_specs = (row(DA_WIDTH),) * 3
    return pl.pallas_call(
        functools.partial(_inproj_kernel, transposed=transposed),
        grid=(n // tm,),
        in_specs=[row(D_MODEL),
                  pl.BlockSpec((1, D_MODEL), lambda i: (0, 0)),
                  pl.BlockSpec((D_MODEL, _C_END), lambda i: (0, 0)),
                  pl.BlockSpec((_C_QR, D_MODEL), lambda i: (0, 0)),
                  tab, tab],
        out_specs=head_specs + tail_specs,
        out_shape=head_shapes + tail_shapes,
        compiler_params=pltpu.CompilerParams(dimension_semantics=("parallel",),
                                             vmem_limit_bytes=VMEM_LIMIT),
        name="inproj",
    )(x, g, w_ext, w_ext_t, cos_t, sin_t)


def _lambda_value(lamv_ref):
    lv = lamv_ref[...]
    s1 = jnp.sum(lv[0:1] * lv[1:2], axis=-1, keepdims=True)
    s2 = jnp.sum(lv[2:3] * lv[3:4], axis=-1, keepdims=True)
    return jnp.exp(s1) - jnp.exp(s2) + LAM_INIT


def _attn_prompt_kernel(qi_tab, ki_tab, qt_ref, k_ref, vt_ref, rel_ref, relm_ref, off_ref,
                        lamv_ref, subg_ref, o_ref, m_sc, l_sc, acc_sc):
    hd = pl.program_id(1)
    t = pl.program_id(2)
    qi = qi_tab[t]
    ki = ki_tab[t]

    @pl.when(ki == 0)
    def _():
        m_sc[...] = jnp.full(m_sc.shape, NEG_INF, F32)
        l_sc[...] = jnp.zeros(l_sc.shape, F32)
        acc_sc[...] = jnp.zeros(acc_sc.shape, F32)

    off = off_ref[hd, t]

    def update(rel):
        qt = qt_ref[0]
        k = k_ref[...]
        vt = vt_ref[0]
        row = lax.broadcasted_iota(I32, qt.shape, 0)
        for c in range(2):
            qc = jnp.where((row >= c * DA_HD) & (row < (c + 1) * DA_HD), qt, jnp.zeros_like(qt))
            tt = jnp.dot(k, qc, preferred_element_type=F32) + rel
            m_old = m_sc[c]
            m_new = jnp.maximum(m_old, jnp.max(tt, axis=0, keepdims=True) + off)
            alpha = jnp.exp2(m_old - m_new)
            p = jnp.exp2(tt + (off - m_new))
            l_sc[c] = alpha * l_sc[c] + jnp.sum(p, axis=0, keepdims=True)
            acc_sc[c] = alpha * acc_sc[c] + jnp.dot(vt, p.astype(BF16), preferred_element_type=F32)
            m_sc[c] = m_new

    @pl.when(ki < qi)
    def _():
        update(rel_ref[0])

    @pl.when(ki == qi)
    def _():
        update(relm_ref[0])
        lam = _lambda_value(lamv_ref)
        o = acc_sc[0] / l_sc[0] - lam * (acc_sc[1] / l_sc[1])
        ms = jnp.mean(o * o, axis=0, keepdims=True)
        a = o * lax.rsqrt(ms + NORM_EPS) * subg_ref[...] * (1.0 - LAM_INIT)
        o_ref[0] = a.astype(o_ref.dtype)


def _attn_prompt(qat, kab, vat, lamv, subg, *, batch, seq, tq):
    nq = seq // tq
    steps = [(i, j) for i in range(nq) for j in range(i + 1)]
    qi_tab = jnp.asarray([s[0] for s in steps], I32)
    ki_tab = jnp.asarray([s[1] for s in steps], I32)
    slopes = (_alibi_slopes().astype(np.float64) * LOG2_E).astype(np.float32)
    dist = (np.arange(tq)[None, :] - np.arange(tq)[:, None]).astype(np.float32)
    rel = -slopes[:, None, None] * dist[None]
    relm = np.where(dist[None] >= 0, rel, np.float32(NEG_INF)).astype(np.float32)
    blk = np.asarray([(s[0] - s[1]) * tq for s in steps], np.float32)
    off = -slopes[:, None] * blk[None, :]
    qspec = pl.BlockSpec((1, DA_VD, tq), lambda b, h, t, qt, kt: (b, h, qt[t]))
    kspec = pl.BlockSpec((tq, DA_VD), lambda b, h, t, qt, kt: (b * nq + kt[t], h))
    vspec = pl.BlockSpec((1, DA_VD, tq), lambda b, h, t, qt, kt: (b, h, kt[t]))
    relspec = pl.BlockSpec((1, tq, tq), lambda b, h, t, qt, kt: (h, 0, 0))
    grid_spec = pltpu.PrefetchScalarGridSpec(
        num_scalar_prefetch=2,
        grid=(batch, DA_HEADS, len(steps)),
        in_specs=[qspec, kspec, vspec, relspec, relspec,
                  pl.BlockSpec(memory_space=pltpu.SMEM),
                  pl.BlockSpec((4, DA_HD), lambda b, h, t, qt, kt: (0, 0)),
                  pl.BlockSpec((DA_VD, 1), lambda b, h, t, qt, kt: (0, 0))],
        out_specs=qspec,
        scratch_shapes=[pltpu.VMEM((2, 1, tq), F32), pltpu.VMEM((2, 1, tq), F32),
                        pltpu.VMEM((2, DA_VD, tq), F32)],
    )
    return pl.pallas_call(
        _attn_prompt_kernel,
        grid_spec=grid_spec,
        out_shape=jax.ShapeDtypeStruct((batch, DA_WIDTH, seq), BF16),
        compiler_params=pltpu.CompilerParams(
            dimension_semantics=("parallel", "parallel", "arbitrary"),
            vmem_limit_bytes=VMEM_LIMIT),
        name="attn_prompt",
    )(qi_tab, ki_tab, qat, kab, vat, jnp.asarray(rel), jnp.asarray(relm), jnp.asarray(off),
      lamv, subg.reshape(DA_VD, 1))


def _ret_prompt_kernel(q_ref, k_ref, v_ref, g_ref, dec_ref, qdec_ref, kdec_ref, sdec_ref,
                       r_ref, s_ref, *, batch):
    c = pl.program_id(0)

    @pl.when(c == 0)
    def _():
        s_ref[...] = jnp.zeros(s_ref.shape, F32)

    qdec = qdec_ref[...]
    kdec = kdec_ref[...]
    for b in range(batch):
        q = q_ref[b].astype(F32)
        k = k_ref[b].astype(F32)
        qd = (q * qdec).astype(BF16)
        kd = (k * kdec).astype(BF16)
        qb = q_ref[b]
        kb = k_ref[b]
        for h in range(RET_HEADS):
            ks = slice(h * RET_DK, (h + 1) * RET_DK)
            vs = slice(h * RET_DV, (h + 1) * RET_DV)
            vh = v_ref[b, :, vs]
            qk = lax.dot_general(qb[:, ks], kb[:, ks], (((1,), (1,)), ((), ())),
                                 preferred_element_type=F32) * dec_ref[h]
            s_old = s_ref[b, h]
            o = jnp.dot(qk.astype(BF16), vh, preferred_element_type=F32)
            o = o + jnp.dot(qd[:, ks], s_old.astype(BF16), preferred_element_type=F32)
            s_ref[b, h] = sdec_ref[h] * s_old + lax.dot_general(
                kd[:, ks], vh, (((0,), (0,)), ((), ())), preferred_element_type=F32)
            gate = _silu(g_ref[b, :, vs].astype(F32))
            r_ref[b, :, vs] = (_rms(o) * gate).astype(r_ref.dtype)


def _ret_tables(length):
    log_g = jnp.log(1.0 - 2.0 ** (-5.0 - jnp.arange(RET_HEADS, dtype=F32)))
    idx = jnp.arange(length, dtype=F32)
    diff = idx[:, None] - idx[None, :]
    dec = jnp.where(diff >= 0, jnp.exp(jnp.maximum(diff, 0.0)[None] * log_g[:, None, None]), 0.0)
    qdec = jnp.exp((idx + 1.0)[:, None] * log_g[None, :])
    kdec = jnp.exp((length - 1.0 - idx)[:, None] * log_g[None, :])
    sdec = jnp.exp(length * log_g)
    rep = lambda a: jnp.repeat(a, RET_DK, axis=1)
    sdec_b = jnp.broadcast_to(sdec[:, None, None], (RET_HEADS, RET_DK, RET_DV))
    return dec, rep(qdec), rep(kdec), sdec_b


def _ret_prompt(qr, kr, vr, gr, *, batch, seq):
    nc = seq // RET_CHUNK
    dec, qdec, kdec, sdec = _ret_tables(RET_CHUNK)
    q3 = qr.reshape(batch, seq, RET_QK_WIDTH)
    k3 = kr.reshape(batch, seq, RET_QK_WIDTH)
    v3 = vr.reshape(batch, seq, RET_WIDTH)
    g3 = gr.reshape(batch, seq, RET_WIDTH)
    blk = lambda w: pl.BlockSpec((batch, RET_CHUNK, w), lambda c: (0, c, 0))
    const = lambda shape: pl.BlockSpec(shape, lambda c: (0,) * len(shape))
    r, s = pl.pallas_call(
        functools.partial(_ret_prompt_kernel, batch=batch),
        grid=(nc,),
        in_specs=[blk(RET_QK_WIDTH), blk(RET_QK_WIDTH), blk(RET_WIDTH), blk(RET_WIDTH),
                  const((RET_HEADS, RET_CHUNK, RET_CHUNK)),
                  const((RET_CHUNK, RET_QK_WIDTH)), const((RET_CHUNK, RET_QK_WIDTH)),
                  const((RET_HEADS, RET_DK, RET_DV))],
        out_specs=(blk(RET_WIDTH), const((batch, RET_HEADS, RET_DK, RET_DV))),
        out_shape=(jax.ShapeDtypeStruct((batch, seq, RET_WIDTH), BF16),
                   jax.ShapeDtypeStruct((batch, RET_HEADS, RET_DK, RET_DV), F32)),
        compiler_params=pltpu.CompilerParams(dimension_semantics=("arbitrary",),
                                             vmem_limit_bytes=VMEM_LIMIT),
        name="ret_prompt",
    )(q3, k3, v3, g3, dec, qdec, kdec, sdec)
    return r.reshape(batch * seq, RET_WIDTH), s


def _attn_sample_kernel(pt_ref, q_ref, kn_ref, vn_ref, rel_ref, slope_ref, lamv_ref, subg_ref,
                        *rest, pages_per_step, past_len, n_steps):
    g_pages = pages_per_step
    k_refs = rest[:g_pages]
    v_refs = rest[g_pages:2 * g_pages]
    o_ref = rest[2 * g_pages]
    s_sc, wnew_sc, acc_sc = rest[2 * g_pages + 1:]
    j = pl.program_id(1)
    rows = 2 * DA_HEADS

    def q_rows():
        col_group = lax.broadcasted_iota(I32, (rows, DA_WIDTH), 1) // DA_HD
        row_id = lax.broadcasted_iota(I32, (rows, DA_WIDTH), 0)
        want = 2 * (row_id % DA_HEADS) + row_id // DA_HEADS
        qb = jnp.broadcast_to(q_ref[0], (rows, DA_WIDTH))
        return jnp.where(col_group == want, qb, 0.0)

    @pl.when(j < n_steps)
    def _():
        qr = q_rows().astype(BF16)
        for g in range(g_pages):
            page = j * g_pages + g
            s = jnp.dot(qr, k_refs[g][0].astype(BF16), preferred_element_type=F32)
            base = (page * LANES - past_len).astype(F32)
            s_sc[page] = s + (rel_ref[...] + slope_ref[...] * base)

    @pl.when(j == n_steps - 1)
    def _():
        s_all = s_sc[...]
        s_new = jnp.sum(q_rows() * kn_ref[0], axis=-1, keepdims=True)
        m = jnp.maximum(jnp.max(jnp.max(s_all, axis=0), axis=-1, keepdims=True), s_new)
        p = jnp.exp(s_all - m)
        p_new = jnp.exp(s_new - m)
        l = jnp.sum(jnp.sum(p, axis=0), axis=-1, keepdims=True) + p_new
        wn = p / l
        wn_new = p_new / l
        lam = _lambda_value(lamv_ref)
        s_sc[:, 0:DA_HEADS, :] = wn[:, 0:DA_HEADS, :] - lam * wn[:, DA_HEADS:rows, :]
        s_sc[:, DA_HEADS:rows, :] = jnp.zeros((s_sc.shape[0], DA_HEADS, LANES), F32)
        wnew_sc[...] = wn_new[0:DA_HEADS] - lam * wn_new[DA_HEADS:rows]
        acc_sc[...] = jnp.zeros(acc_sc.shape, F32)

    @pl.when(j >= n_steps)
    def _():
        lane = lax.broadcasted_iota(I32, (rows, LANES), 1)
        own = lane % DA_HEADS == lax.broadcasted_iota(I32, (rows, LANES), 0)
        acc = acc_sc[...]
        for g in range(g_pages):
            page = (j - n_steps) * g_pages + g
            w = s_sc[page]
            w_rows = jnp.concatenate(
                [jnp.where(own, jnp.take_along_axis(w, (LANES // DA_HEADS) * c + lane // DA_HEADS,
                                                    axis=1), 0.0)
                 for c in range(DA_HEADS)], axis=-1)
            acc = acc + jnp.dot(w_rows.astype(BF16), v_refs[g][0].astype(BF16),
                                preferred_element_type=F32)
        acc_sc[...] = acc

    @pl.when(j == 2 * n_steps - 1)
    def _():
        acc = acc_sc[...]
        w_new = wnew_sc[...]
        v_new = vn_ref[0]
        outs = []
        for h in range(DA_HEADS):
            vs = slice(h * DA_VD, (h + 1) * DA_VD)
            o = acc[h:h + 1, :] + w_new[h:h + 1, :] * v_new[:, vs]
            outs.append(_rms(o) * subg_ref[...] * (1.0 - LAM_INIT))
        o_ref[0] = jnp.concatenate(outs, axis=-1)


def _attn_sample(q, k_new, v_new, cache_kt, cache_vr, page_table, lamv, subg, *, pages_per_step):
    nb, n_pages = page_table.shape
    page = cache_kt.shape[2]
    assert page == LANES
    past_len = n_pages * page
    g_pages = pages_per_step
    n_steps = n_pages // g_pages
    rows = 2 * DA_HEADS
    slope_rows = np.tile(_alibi_slopes(), 2)[:, None]
    rel = slope_rows * np.arange(page, dtype=np.float32)[None, :]
    pt = page_table.reshape(-1).astype(I32)
    row3 = lambda: pl.BlockSpec((1, 1, DA_WIDTH), lambda b, j, pt: (b, 0, 0))
    const2 = lambda shape: pl.BlockSpec(shape, lambda b, j, pt: (0, 0))

    def k_spec(g):
        return pl.BlockSpec(
            (1, DA_WIDTH, page),
            lambda b, j, pt: (pt[b * n_pages + jnp.minimum(j, n_steps - 1) * g_pages + g], 0, 0))

    def v_spec(g):
        return pl.BlockSpec(
            (1, page * DA_HEADS, DA_VD),
            lambda b, j, pt: (pt[b * n_pages + jnp.maximum(j - n_steps, 0) * g_pages + g], 0, 0))

    grid_spec = pltpu.PrefetchScalarGridSpec(
        num_scalar_prefetch=1,
        grid=(nb, 2 * n_steps),
        in_specs=[row3(), row3(), row3(), const2((rows, page)), const2((rows, 1)),
                  const2((4, DA_HD)), const2((1, DA_VD))]
                 + [k_spec(g) for g in range(g_pages)] + [v_spec(g) for g in range(g_pages)],
        out_specs=row3(),
        scratch_shapes=[pltpu.VMEM((n_pages, rows, page), F32), pltpu.VMEM((DA_HEADS, 1), F32),
                        pltpu.VMEM((rows, DA_VD), F32)],
    )
    out = pl.pallas_call(
        functools.partial(_attn_sample_kernel, pages_per_step=g_pages, past_len=past_len,
                          n_steps=n_steps),
        grid_spec=grid_spec,
        out_shape=jax.ShapeDtypeStruct((nb, 1, DA_WIDTH), F32),
        compiler_params=pltpu.CompilerParams(dimension_semantics=("parallel", "arbitrary"),
                                             vmem_limit_bytes=VMEM_LIMIT),
        name="attn_sample",
    )(pt, q.reshape(nb, 1, DA_WIDTH), k_new.reshape(nb, 1, DA_WIDTH), v_new.reshape(nb, 1, DA_WIDTH),
      jnp.asarray(rel), jnp.asarray(slope_rows), lamv, subg,
      *([cache_kt] * g_pages), *([cache_vr] * g_pages))
    return out.reshape(nb, DA_WIDTH)


def _ret_sample_kernel(q_ref, k_ref, qc_ref, kc_ref, v_ref, g_ref, s_ref, qdec_ref, sdec_ref,
                       r_ref, so_ref):
    q = q_ref[0]
    k = k_ref[0]
    qd_col = _bf(qc_ref[0] * qdec_ref[...])
    k_col = kc_ref[0]
    outs = []
    for h in range(RET_HEADS):
        ks = slice(h * RET_DK, (h + 1) * RET_DK)
        vs = slice(h * RET_DV, (h + 1) * RET_DV)
        vh = v_ref[0, :, vs]
        s_old = s_ref[0, h]
        qk = jnp.sum(q[:, ks] * k[:, ks], axis=-1, keepdims=True)
        o = qk * vh + jnp.sum(qd_col[ks, :] * _bf(s_old), axis=0, keepdims=True)
        so_ref[0, h] = sdec_ref[h] * s_old + k_col[ks, :] * vh
        outs.append(_rms(o) * _silu(g_ref[0, :, vs]))
    r_ref[0] = jnp.concatenate(outs, axis=-1)


def _ret_sample(qr, kr, vr, gr, state):
    nb = qr.shape[0]
    _, qdec, _, sdec = _ret_tables(1)
    row3 = lambda w: pl.BlockSpec((1, 1, w), lambda b: (b, 0, 0))
    col3 = pl.BlockSpec((1, RET_QK_WIDTH, 1), lambda b: (b, 0, 0))
    st = pl.BlockSpec((1, RET_HEADS, RET_DK, RET_DV), lambda b: (b, 0, 0, 0))
    r, s_new = pl.pallas_call(
        _ret_sample_kernel,
        grid=(nb,),
        in_specs=[row3(RET_QK_WIDTH), row3(RET_QK_WIDTH), col3, col3, row3(RET_WIDTH),
                  row3(RET_WIDTH), st,
                  pl.BlockSpec((RET_QK_WIDTH, 1), lambda b: (0, 0)),
                  pl.BlockSpec((RET_HEADS, RET_DK, RET_DV), lambda b: (0, 0, 0))],
        out_specs=(row3(RET_WIDTH), st),
        out_shape=(jax.ShapeDtypeStruct((nb, 1, RET_WIDTH), F32),
                   jax.ShapeDtypeStruct((nb, RET_HEADS, RET_DK, RET_DV), F32)),
        compiler_params=pltpu.CompilerParams(dimension_semantics=("parallel",)),
        name="ret_sample",
    )(qr.reshape(nb, 1, -1), kr.reshape(nb, 1, -1), qr.reshape(nb, -1, 1), kr.reshape(nb, -1, 1),
      vr.reshape(nb, 1, -1), gr.reshape(nb, 1, -1), state, qdec.reshape(-1, 1), sdec)
    return r.reshape(nb, RET_WIDTH), s_new


def _outproj_kernel(a_ref, r_ref, x_ref, wo_ref, g_ref, wr_ref, br_ref,
                    x1_ref, hfp_ref, ti_ref, tg_ref, *, a_transposed):
    if a_transposed:
        mix_a = lax.dot_general(a_ref[0], wo_ref[:DA_WIDTH, :], (((0,), (0,)), ((), ())),
                                preferred_element_type=F32)
    else:
        mix_a = _mm(a_ref[...], wo_ref[:DA_WIDTH, :])
    mix = mix_a + _mm(r_ref[...], wo_ref[DA_WIDTH:, :])
    x1 = x_ref[...] + mix
    x1_ref[...] = x1
    hf = _rms(x1) * g_ref[...]
    hfp_ref[0] = _pack_pairs(hf[:, :HALF])
    hfp_ref[1] = _pack_pairs(hf[:, HALF:])
    logits = _mm(hf, wr_ref[...]) + br_ref[...]
    lane = lax.broadcasted_iota(I32, logits.shape, 1)
    work = logits
    top_v, top_i = [], []
    for _ in range(TOP_K):
        m = jnp.max(work, axis=-1, keepdims=True)
        idx = jnp.min(jnp.where(work == m, lane, LANES), axis=-1, keepdims=True)
        top_v.append(m)
        top_i.append(idx)
        work = jnp.where(lane == idx, -jnp.inf, work)
    es = [jnp.exp(v - top_v[0]) for v in top_v]
    denom = es[0] + es[1] + es[2] + es[3]
    ti = jnp.zeros(logits.shape, I32)
    tg = jnp.zeros(logits.shape, F32)
    for kk in range(TOP_K):
        ti = jnp.where(lane == kk, top_i[kk], ti)
        tg = jnp.where(lane == kk, es[kk] / denom, tg)
    ti_ref[...] = ti
    tg_ref[...] = tg


def _outproj(a, r, x, w_out, g, w_router, b_router, *, tm):
    n = x.shape[0]
    row = lambda w: pl.BlockSpec((tm, w), lambda i: (i, 0))
    const = lambda shape: pl.BlockSpec(shape, lambda i: (0, 0))
    a_transposed = a.ndim == 3
    if a_transposed:
        n_s = a.shape[2] // tm
        a_spec = pl.BlockSpec((1, DA_WIDTH, tm), lambda i: (i // n_s, 0, i % n_s))
    else:
        a_spec = row(DA_WIDTH)
    return pl.pallas_call(
        functools.partial(_outproj_kernel, a_transposed=a_transposed),
        grid=(n // tm,),
        in_specs=[a_spec, row(RET_WIDTH), row(D_MODEL), const((D_MODEL, D_MODEL)),
                  const((1, D_MODEL)), const((D_MODEL, LANES)), const((1, LANES))],
        out_specs=(row(D_MODEL), pl.BlockSpec((2, tm, QUARTER), lambda i: (0, i, 0)),
                   row(LANES), row(LANES)),
        out_shape=(jax.ShapeDtypeStruct((n, D_MODEL), F32),
                   jax.ShapeDtypeStruct((2, n, QUARTER), I32),
                   jax.ShapeDtypeStruct((n, LANES), I32),
                   jax.ShapeDtypeStruct((n, LANES), F32)),
        compiler_params=pltpu.CompilerParams(dimension_semantics=("parallel",),
                                             vmem_limit_bytes=VMEM_LIMIT),
        name="outproj",
    )(a, r, x, w_out, g, w_router, b_router)


def _sc_gather(x, idx):
    m = idx.shape[0]
    cols = x.shape[1]
    mesh = plsc.VectorSubcoreMesh(core_axis_name="c", subcore_axis_name="s")

    @pl.kernel(out_type=jax.ShapeDtypeStruct((m, cols), x.dtype), mesh=mesh)
    def gather_kernel(x_hbm, i_hbm, o_hbm):
        def body(i_vmem, o_vmem):
            pltpu.sync_copy(x_hbm.at[i_vmem.at[0]], o_vmem)

        pltpu.emit_pipeline(
            body,
            grid=(m // SC_WINDOW,),
            in_specs=[pl.BlockSpec((1, SC_WINDOW), lambda i: (0, i))],
            out_specs=[pl.BlockSpec((SC_WINDOW, cols), lambda i: (i, 0))],
            core_axis_name=("c", "s"),
            dimension_semantics=(pltpu.PARALLEL,),
        )(i_hbm, o_hbm)

    return gather_kernel(x, idx.reshape(1, m))


def _filler(n, modulus):
    return jnp.arange(n, dtype=I32) % modulus


def _sc_scatter(x, idx, out_rows):
    m = idx.shape[0]
    cols = x.shape[1]
    n_blk = x.shape[0] // 2 // SC_WINDOW
    mesh = plsc.VectorSubcoreMesh(core_axis_name="c", subcore_axis_name="s")

    @pl.kernel(out_type=jax.ShapeDtypeStruct((out_rows, cols), x.dtype), mesh=mesh)
    def scatter_kernel(x_hbm, i_hbm, o_hbm):
        def body(x_vmem, i_vmem):
            pltpu.sync_copy(x_vmem, o_hbm.at[i_vmem.at[0]])

        pltpu.emit_pipeline(
            body,
            grid=(m // SC_WINDOW,),
            in_specs=[pl.BlockSpec((SC_WINDOW, cols),
                                   lambda i: ((i // (TOP_K * n_blk)) * n_blk + i % n_blk, 0)),
                      pl.BlockSpec((1, SC_WINDOW), lambda i: (0, i))],
            out_specs=[],
            core_axis_name=("c", "s"),
            dimension_semantics=(pltpu.PARALLEL,),
        )(x_hbm, i_hbm)

    return scatter_kernel(x, idx.reshape(1, m))


def _rank_kernel(ti_ref, tri_ref, rank_ref, cnt_ref, base_sc):
    @pl.when(pl.program_id(0) == 0)
    def _():
        base_sc[...] = jnp.zeros(base_sc.shape, F32)

    ti = ti_ref[...]
    lane = lax.broadcasted_iota(I32, ti.shape, 1)
    picks = [lane == ti[:, k:k + 1] for k in range(TOP_K)]
    onehot = jnp.zeros(ti.shape, F32)
    for pk in picks:
        onehot = onehot + jnp.where(pk, 1.0, 0.0)
    pos = jnp.dot(tri_ref[...], onehot.astype(BF16), preferred_element_type=F32) + base_sc[...]
    rank = jnp.zeros(ti.shape, I32)
    for k, pk in enumerate(picks):
        r_k = jnp.sum(jnp.where(pk, pos, 0.0), axis=-1, keepdims=True)
        rank = jnp.where(lane == k, r_k.astype(I32), rank)
    rank_ref[...] = rank
    base_sc[...] = base_sc[...] + jnp.sum(onehot, axis=0, keepdims=True)
    cnt_ref[...] = base_sc[...]


def _rank(ti, *, tr):
    n = ti.shape[0]
    tri = jnp.asarray(np.tril(np.ones((tr, tr), np.float32), -1), BF16)
    return pl.pallas_call(
        _rank_kernel,
        grid=(n // tr,),
        in_specs=[pl.BlockSpec((tr, LANES), lambda i: (i, 0)),
                  pl.BlockSpec((tr, tr), lambda i: (0, 0))],
        out_specs=(pl.BlockSpec((tr, LANES), lambda i: (i, 0)),
                   pl.BlockSpec((1, LANES), lambda i: (0, 0))),
        out_shape=(jax.ShapeDtypeStruct((n, LANES), I32), jax.ShapeDtypeStruct((1, LANES), F32)),
        scratch_shapes=[pltpu.VMEM((1, LANES), F32)],
        compiler_params=pltpu.CompilerParams(dimension_semantics=("arbitrary",)),
        name="route_rank",
    )(ti, tri)


def _prep_expert_weights(wup_ref, wdn_ref, wupb, wdnb):
    lane = lax.broadcasted_iota(I32, (PREP_ROWS, LANES), 1)
    first_half = lane < LANES // 2
    idx_even = (2 * lane) % LANES
    idx_odd = (2 * lane + 1) % LANES

    def body(r, carry):
        rows = pl.ds(pl.multiple_of(r * PREP_ROWS, PREP_ROWS), PREP_ROWS)
        for ct in range(D_FF // LANES):
            a = wup_ref[0, rows, 2 * ct * LANES:(2 * ct + 1) * LANES]
            b = wup_ref[0, rows, (2 * ct + 1) * LANES:(2 * ct + 2) * LANES]
            even = jnp.where(first_half, jnp.take_along_axis(a, idx_even, axis=1),
                             jnp.take_along_axis(b, idx_even, axis=1))
            odd = jnp.where(first_half, jnp.take_along_axis(a, idx_odd, axis=1),
                            jnp.take_along_axis(b, idx_odd, axis=1))
            wupb[rows, ct * LANES:(ct + 1) * LANES] = even.astype(BF16)
            wupb[rows, D_FF + ct * LANES:D_FF + (ct + 1) * LANES] = odd.astype(BF16)
        wdnb[rows, :] = wdn_ref[0, rows, :].astype(BF16)
        return carry

    lax.fori_loop(0, D_MODEL // PREP_ROWS, body, 0)


def _moe_kernel(be_ref, na_ref, nv_ref, xs_ref, wup_ref, bup_ref, wdn_ref, bdn_ref, out_ref,
                wupb, wdnb):
    i = pl.program_id(0)
    n_act = na_ref[0]
    e = be_ref[jnp.minimum(i, n_act - 1)]
    e_prev = be_ref[jnp.maximum(i - 1, 0)]

    @pl.when(jnp.logical_and(i < n_act, jnp.logical_or(i == 0, e != e_prev)))
    def _():
        _prep_expert_weights(wup_ref, wdn_ref, wupb, wdnb)

    @pl.when(i < n_act)
    def _():
        lo0, hi0 = _unpack_pairs(xs_ref[0])
        lo1, hi1 = _unpack_pairs(xs_ref[1])
        x = jnp.concatenate([lo0, hi0, lo1, hi1], axis=-1)
        live = lax.broadcasted_iota(I32, x.shape, 0) < nv_ref[i]
        x = jnp.where(live, x, 0.0).astype(BF16)
        u = jnp.dot(x, wupb[...], preferred_element_type=F32) + bup_ref[0]
        g = jnp.minimum(u[:, :D_FF], SWIGLU_LIMIT)
        lin = jnp.clip(u[:, D_FF:], -SWIGLU_LIMIT, SWIGLU_LIMIT)
        y = g * (1.0 / (1.0 + jnp.exp(-SWIGLU_ALPHA * g))) * (lin + 1.0)
        o = jnp.dot(y.astype(BF16), wdnb[...], preferred_element_type=F32) + bdn_ref[0]
        out_ref[0] = _pack_pairs(o[:, :HALF])
        out_ref[1] = _pack_pairs(o[:, HALF:])

    @pl.when(i >= n_act)
    def _():
        out_ref[...] = jnp.zeros(out_ref.shape, I32)


def _moe(block_e, n_active, n_valid, xs, w_up, b_up_d, w_down, b_down):
    n_blocks = block_e.shape[0]
    n_slots = n_blocks * MOE_TM

    def act(i, na):
        return jnp.minimum(i, na[0] - 1)

    grid_spec = pltpu.PrefetchScalarGridSpec(
        num_scalar_prefetch=3,
        grid=(n_blocks,),
        in_specs=[
            pl.BlockSpec((2, MOE_TM, QUARTER), lambda i, be, na, nv: (0, act(i, na), 0)),
            pl.BlockSpec((1, D_MODEL, 2 * D_FF), lambda i, be, na, nv: (be[act(i, na)], 0, 0)),
            pl.BlockSpec((1, 1, 2 * D_FF), lambda i, be, na, nv: (be[act(i, na)], 0, 0)),
            pl.BlockSpec((1, D_FF, D_MODEL), lambda i, be, na, nv: (be[act(i, na)], 0, 0)),
            pl.BlockSpec((1, 1, D_MODEL), lambda i, be, na, nv: (be[act(i, na)], 0, 0)),
        ],
        out_specs=pl.BlockSpec((2, MOE_TM, QUARTER), lambda i, be, na, nv: (0, i, 0)),
        scratch_shapes=[pltpu.VMEM((D_MODEL, 2 * D_FF), BF16), pltpu.VMEM((D_FF, D_MODEL), BF16)],
    )
    return pl.pallas_call(
        _moe_kernel,
        grid_spec=grid_spec,
        out_shape=jax.ShapeDtypeStruct((2, n_slots, QUARTER), I32),
        compiler_params=pltpu.CompilerParams(dimension_semantics=("arbitrary",),
                                             vmem_limit_bytes=VMEM_LIMIT),
        name="moe_ffn",
    )(block_e, n_active, n_valid, xs, w_up, b_up_d, w_down, b_down)


def _slot_layout(counts, top_i, rank, n_blocks):
    experts = jnp.arange(N_EXPERTS, dtype=I32)
    padded = (counts + MOE_TM - 1) // MOE_TM * MOE_TM
    pad_end = jnp.cumsum(padded)
    pad_start = pad_end - padded
    pick = top_i[:, :, None] == experts[None, None, :]
    dest = rank + jnp.sum(jnp.where(pick, pad_start[None, None, :], 0), axis=-1)
    blk_start = jnp.arange(n_blocks, dtype=I32) * MOE_TM
    block_e = jnp.minimum(jnp.sum((blk_start[:, None] >= pad_end[None, :]).astype(I32), axis=1),
                          N_EXPERTS - 1)
    live = jnp.clip(counts[block_e] - (blk_start - pad_start[block_e]), 0, MOE_TM)
    n_active = (pad_end[-1] // MOE_TM).astype(I32).reshape(1)
    return dest.astype(I32), block_e.astype(I32), live.astype(I32), n_active


def _combine_kernel(yg_ref, tg_ref, x1_ref, g_ref, o_ref):
    tg = tg_ref[...]
    parts = [jnp.zeros((x1_ref.shape[0], QUARTER), F32) for _ in range(4)]
    for kk in range(TOP_K):
        gate = tg[:, kk:kk + 1]
        lo0, hi0 = _unpack_pairs(yg_ref[0, kk])
        lo1, hi1 = _unpack_pairs(yg_ref[1, kk])
        for j, piece in enumerate((lo0, hi0, lo1, hi1)):
            parts[j] = parts[j] + gate * piece
    x2 = x1_ref[...] + jnp.concatenate(parts, axis=-1)
    o_ref[...] = _rms(x2) * g_ref[...]


def _combine(yg, tg, x1, g, *, tm, row_offset):
    n = x1.shape[0]
    blk0 = row_offset // tm
    return pl.pallas_call(
        _combine_kernel,
        grid=(n // tm,),
        in_specs=[pl.BlockSpec((2, TOP_K, tm, QUARTER), lambda i: (0, 0, i + blk0, 0)),
                  pl.BlockSpec((tm, LANES), lambda i: (i, 0)),
                  pl.BlockSpec((tm, D_MODEL), lambda i: (i, 0)),
                  pl.BlockSpec((1, D_MODEL), lambda i: (0, 0))],
        out_specs=pl.BlockSpec((tm, D_MODEL), lambda i: (i, 0)),
        out_shape=jax.ShapeDtypeStruct((n, D_MODEL), F32),
        compiler_params=pltpu.CompilerParams(dimension_semantics=("parallel",),
                                             vmem_limit_bytes=VMEM_LIMIT),
        name="combine",
    )(yg, tg, x1, g)


def _extend_w_in(w):
    swap = np.arange(RET_QK_WIDTH).reshape(RET_HEADS, 2, RET_DK // 2)[:, ::-1, :].reshape(-1)
    qa, ka, va = w[:, 0:512], w[:, 512:1024], w[:, 1024:1536]
    qr, kr = w[:, 1536:1792], w[:, 1792:2048]
    vr, gr = w[:, 2048:2560], w[:, 2560:3072]
    sa = DA_HD ** -0.5
    sr = RET_DK ** -0.5
    return jnp.concatenate([qa * sa, ka, va, qr, qr[:, swap], kr * sr, kr[:, swap] * sr, vr, gr], axis=1)


def _rotary_tables(pos):
    half = RET_DK // 2
    inv = ROPE_BASE ** (-jnp.arange(half, dtype=F32) / half)
    ang = pos.astype(F32)[:, None] * inv[None, :]
    cos = jnp.cos(ang)
    sin = jnp.sin(ang)
    cos_t = jnp.tile(jnp.concatenate([cos, cos], axis=1), (1, RET_HEADS))
    sin_t = jnp.tile(jnp.concatenate([-sin, sin], axis=1), (1, RET_HEADS))
    return cos_t, sin_t


def kernel(x_prompt, x_sample, cache_k, cache_v, state_ret, page_table, norm_attn_g, w_in, lam_q1,
           lam_k1, lam_q2, lam_k2, da_subln_g, w_out, norm_ffn_g, w_router, b_router, w_up, b_up,
           w_down, b_down, norm_final_g):
    return _forward(x_prompt, x_sample, cache_k, cache_v, state_ret, page_table, norm_attn_g, w_in,
                    lam_q1, lam_k1, lam_q2, lam_k2, da_subln_g, w_out, norm_ffn_g, w_router,
                    b_router, w_up, b_up, w_down, b_down, norm_final_g,
                    tm=512, tq=512, pages_per_step=32, gather=_sc_gather, scatter=_sc_scatter)


def _forward(x_prompt, x_sample, cache_k, cache_v, state_ret, page_table, norm_attn_g, w_in, lam_q1,
             lam_k1, lam_q2, lam_k2, da_subln_g, w_out, norm_ffn_g, w_router, b_router, w_up, b_up,
             w_down, b_down, norm_final_g, *, tm, tq, pages_per_step, gather, scatter):
    batch, seq, _ = x_prompt.shape
    nb = x_sample.shape[0]
    n_pool, page = cache_k.shape[1], cache_k.shape[2]
    past_len = page_table.shape[1] * page
    n_prompt = batch * seq

    w_ext = _extend_w_in(w_in[0])
    w_ext_b = w_ext.astype(BF16)
    w_ext_t = jnp.concatenate([(w_ext[:, :_C_KA] * LOG2_E).astype(BF16), w_ext_b[:, _C_KA:_C_QR]],
                              axis=1).T
    g_attn = norm_attn_g[0].reshape(1, D_MODEL)
    g_ffn = norm_ffn_g[0].reshape(1, D_MODEL)
    g_fin = norm_final_g.reshape(1, D_MODEL)
    lamv = jnp.stack([lam_q1[0], lam_k1[0], lam_q2[0], lam_k2[0]]).astype(F32)
    subg = da_subln_g[0].reshape(1, DA_VD)
    w_o_b = w_out[0].astype(BF16)
    w_r = jnp.pad(w_router[0], ((0, 0), (0, LANES - N_EXPERTS))).astype(BF16)
    b_r = jnp.pad(b_router[0], (0, LANES - N_EXPERTS), constant_values=NEG_INF).reshape(1, LANES)
    assert cache_k.shape[0] == 1 and w_up.shape[0] == 1
    w_up_e = w_up.reshape(N_EXPERTS, D_MODEL, 2 * D_FF)
    b_up_d = jnp.concatenate([b_up[0][:, 0::2], b_up[0][:, 1::2]], axis=-1).reshape(N_EXPERTS, 1, 2 * D_FF)
    w_down_e = w_down.reshape(N_EXPERTS, D_FF, D_MODEL)
    b_dn = b_down.reshape(N_EXPERTS, 1, D_MODEL)
    cos_p, sin_p = _rotary_tables(jnp.arange(seq))
    cos_s, sin_s = _rotary_tables(past_len + jnp.zeros((nb,), I32))

    xp = x_prompt.reshape(n_prompt, D_MODEL)
    qat, kat, kab, va, vat, qr, kr, vr, gr = _inproj(xp, g_attn, w_ext_b, w_ext_t, cos_p, sin_p,
                                                     tm=tm, seq=seq)
    a_p = _attn_prompt(qat, kab, vat, lamv, subg, batch=batch, seq=seq, tq=tq)
    r_p, ret_p = _ret_prompt(qr, kr, vr, gr, batch=batch, seq=seq)
    x1_p, hfp_p, ti_p, tg_p = _outproj(a_p, r_p, xp, w_o_b, g_ffn, w_r, b_r, tm=tm)

    xs_ = x_sample.reshape(nb, D_MODEL)
    qa_s, ka_s, va_s, qr_s, kr_s, vr_s, gr_s = _inproj(xs_, g_attn, w_ext_b, w_ext_t, cos_s, sin_s,
                                                       tm=nb)
    ckt = jnp.transpose(cache_k.reshape(n_pool, page, DA_WIDTH), (0, 2, 1))
    cvr = cache_v.reshape(n_pool, page * DA_HEADS, DA_VD)
    a_s = _attn_sample(qa_s, ka_s, va_s, ckt, cvr, page_table, lamv, subg,
                       pages_per_step=pages_per_step)
    r_s, ret_s = _ret_sample(qr_s, kr_s, vr_s, gr_s,
                             state_ret.reshape(nb, RET_HEADS, RET_DK, RET_DV))
    x1_s, hfp_s, ti_s, tg_s = _outproj(a_s, r_s, xs_, w_o_b, g_ffn, w_r, b_r, tm=nb)

    n_tok = n_prompt + nb
    tok_pad = -(-n_tok // ROUTE_TILE) * ROUTE_TILE
    pad = tok_pad - n_tok
    hfp = jnp.concatenate([hfp_p, hfp_s, jnp.zeros((2, pad, QUARTER), I32)], axis=1)
    ti_all = jnp.concatenate([ti_p, ti_s, jnp.full((pad, LANES), LANES - 1, I32)], axis=0)
    rank, cnt = _rank(ti_all, tr=ROUTE_TILE)
    counts = cnt[0, :N_EXPERTS].astype(I32)
    n_blocks = -(-n_tok * TOP_K // MOE_TM) + N_EXPERTS
    n_slots = n_blocks * MOE_TM
    dest, block_e, n_live, n_active = _slot_layout(counts, ti_all[:n_tok, :TOP_K],
                                                   rank[:n_tok, :TOP_K], n_blocks)
    slots_pad = -(-(n_slots + 1) // (SC_UNIT // 2)) * (SC_UNIT // 2)
    dump = n_slots + _filler(pad, slots_pad - n_slots)
    dest_sc = jnp.concatenate([dest, jnp.broadcast_to(dump[:, None], (pad, TOP_K))], axis=0).T
    xs_sorted = scatter(hfp.reshape(2 * tok_pad, QUARTER),
                        jnp.concatenate([dest_sc, slots_pad + dest_sc]).reshape(-1), 2 * slots_pad)
    out_sorted = _moe(block_e, n_active, n_live, xs_sorted.reshape(2, slots_pad, QUARTER),
                      w_up_e, b_up_d, w_down_e, b_dn)
    fill = jnp.broadcast_to(_filler(pad, n_slots)[:, None], (pad, TOP_K))
    dest_g = jnp.concatenate([dest, fill], axis=0).T
    yg = gather(out_sorted.reshape(2 * n_slots, QUARTER),
                jnp.concatenate([dest_g, n_slots + dest_g]).reshape(-1))
    yg = yg.reshape(2, TOP_K, tok_pad, QUARTER)
    y_p = _combine(yg, tg_p, x1_p, g_fin, tm=tm, row_offset=0)
    y_s = _combine(yg, tg_s, x1_s, g_fin, tm=nb, row_offset=n_prompt)

    depth = 1
    return (y_p.reshape(batch, seq, D_MODEL),
            y_s.reshape(nb, 1, D_MODEL),
            jnp.transpose(kat.reshape(depth, batch, DA_HEADS, 2, DA_HD, seq), (0, 1, 5, 2, 3, 4)),
            va.reshape(depth, batch, seq, DA_HEADS, DA_VD),
            ret_p.reshape(depth, batch, RET_HEADS, RET_DK, RET_DV),
            ka_s.reshape(depth, nb, 1, DA_HEADS, 2, DA_HD),
            va_s.reshape(depth, nb, 1, DA_HEADS, DA_VD),
            ret_s.reshape(depth, nb, RET_HEADS, RET_DK, RET_DV))
```

```python
import functools
import math

import numpy as np
import jax
import jax.numpy as jnp
from jax import lax
from jax.experimental import pallas as pl
from jax.experimental.pallas import tpu as pltpu
from jax.experimental.pallas import tpu_sc as plsc

F32 = jnp.float32
BF16 = jnp.bfloat16
I32 = jnp.int32

D_MODEL = 1024
DA_HEADS = 4
DA_VD = 128
DA_HD = 64
DA_WIDTH = DA_HEADS * DA_VD
RET_HEADS = 4
RET_DV = 128
RET_DK = 64
RET_WIDTH = RET_HEADS * RET_DV
RET_QK_WIDTH = RET_HEADS * RET_DK
RET_CHUNK = 128
ROPE_BASE = 10000.0
N_EXPERTS = 32
TOP_K = 4
D_FF = 1024
SWIGLU_LIMIT = 7.0
SWIGLU_ALPHA = 1.702
NORM_EPS = 1e-5
NEG_INF = -1e30
LAM_INIT = 0.8 - 0.6 * math.exp(-0.3 * 0)
LOG2_E = math.log2(math.e)
LANES = 128
HALF = D_MODEL // 2
QUARTER = D_MODEL // 4
MOE_TM = 512
PREP_ROWS = 64
ROUTE_TILE = 512
SC_WINDOW = 128
SC_SUBCORES = 32
SC_UNIT = SC_WINDOW * SC_SUBCORES
VMEM_LIMIT = 56 * 1024 * 1024

_C_QA, _C_KA, _C_VA = 0, 512, 1024
_C_QR, _C_QRS, _C_KR, _C_KRS = 1536, 1792, 2048, 2304
_C_VR, _C_GR, _C_END = 2560, 3072, 3584


def _mm(a, b):
    return jnp.dot(a.astype(BF16), b.astype(BF16), preferred_element_type=F32)


def _bf(x):
    return x.astype(BF16).astype(F32)


def _rms(x):
    return x * lax.rsqrt(jnp.mean(x * x, axis=-1, keepdims=True) + NORM_EPS)


def _silu(g):
    return g / (1.0 + jnp.exp(-g))


def _pack_pairs(x):
    w = x.shape[-1] // 2
    lo = lax.bitcast_convert_type(x[:, :w].astype(BF16).astype(F32), jnp.uint32)
    hi = lax.bitcast_convert_type(x[:, w:].astype(BF16).astype(F32), jnp.uint32)
    return lax.bitcast_convert_type(hi | (lo >> 16), I32)


def _unpack_pairs(p):
    u = lax.bitcast_convert_type(p, jnp.uint32)
    lo = lax.bitcast_convert_type(u << 16, F32)
    hi = lax.bitcast_convert_type(u & jnp.uint32(0xFFFF0000), F32)
    return lo, hi


def _alibi_slopes():
    return np.asarray([2.0 ** (-8.0 * (h + 1) / DA_HEADS) for h in range(DA_HEADS)], np.float32)


def _inproj_kernel(x_ref, g_ref, w_ref, wt_ref, cos_ref, sin_ref, *outs, transposed):
    hm = (_rms(x_ref[...]) * g_ref[...]).astype(BF16)

    def mm(lo, hi):
        return _mm(hm, w_ref[:, lo:hi])

    def mm_t(lo, hi):
        return lax.dot_general(wt_ref[lo:hi, :], hm, (((1,), (1,)), ((), ())),
                               preferred_element_type=F32)

    if transposed:
        qat_ref, kat_ref, kab_ref, va_ref, vat_ref, qr_ref, kr_ref, vr_ref, gr_ref = outs
        qat_ref[0] = mm_t(_C_QA, _C_KA).astype(BF16)
        kat_ref[0] = mm_t(_C_KA, _C_VA)
        kab_ref[...] = mm(_C_KA, _C_VA).astype(BF16)
        va = mm(_C_VA, _C_QR)
        for h in range(DA_HEADS):
            va_ref[pl.ds(h, va.shape[0], stride=DA_HEADS), :] = va[:, h * DA_VD:(h + 1) * DA_VD]
        vat_ref[0] = mm_t(_C_VA, _C_QR).astype(BF16)
    else:
        qa_ref, ka_ref, va_ref, qr_ref, kr_ref, vr_ref, gr_ref = outs
        qa_ref[...] = mm(_C_QA, _C_KA)
        ka_ref[...] = mm(_C_KA, _C_VA)
        va_ref[...] = mm(_C_VA, _C_QR)
    cos = cos_ref[...]
    sin = sin_ref[...]
    qr_ref[...] = (mm(_C_QR, _C_QRS) * cos + mm(_C_QRS, _C_KR) * sin).astype(qr_ref.dtype)
    kr_ref[...] = (mm(_C_KR, _C_KRS) * cos + mm(_C_KRS, _C_VR) * sin).astype(kr_ref.dtype)
    vr_ref[...] = mm(_C_VR, _C_GR).astype(vr_ref.dtype)
    gr_ref[...] = mm(_C_GR, _C_END).astype(gr_ref.dtype)


def _inproj(x, g, w_ext, w_ext_t, cos_t, sin_t, *, tm, seq=None):
    n = x.shape[0]
    n_tab = cos_t.shape[0] // tm
    transposed = seq is not None
    act = BF16 if transposed else F32
    row = lambda w: pl.BlockSpec((tm, w), lambda i: (i, 0))
    tab = pl.BlockSpec((tm, RET_QK_WIDTH), lambda i: (i % n_tab, 0))
    sds = jax.ShapeDtypeStruct
    tail_shapes = (sds((n, RET_QK_WIDTH), act), sds((n, RET_QK_WIDTH), act),
                   sds((n, RET_WIDTH), act), sds((n, RET_WIDTH), act))
    tail_specs = (row(RET_QK_WIDTH), row(RET_QK_WIDTH), row(RET_WIDTH), row(RET_WIDTH))
    if transposed:
        n_s = seq // tm
        tr = pl.BlockSpec((1, DA_WIDTH, tm), lambda i: (i // n_s, 0, i % n_s))
        head_shapes = (sds((n // seq, DA_WIDTH, seq), BF16), sds((n // seq, DA_WIDTH, seq), F32),
                       sds((n, DA_WIDTH), BF16), sds((n * DA_HEADS, DA_VD), F32),
                       sds((n // seq, DA_WIDTH, seq), BF16))
        head_specs = (tr, tr, row(DA_WIDTH),
                      pl.BlockSpec((tm * DA_HEADS, DA_VD), lambda i: (i, 0)), tr)
    else:
        head_shapes = (sds((n, DA_WIDTH), F32),) * 3
        head_specs = (row(DA_WIDTH),) * 3
    return pl.pallas_call(
        functools.partial(_inproj_kernel, transposed=transposed),
        grid=(n // tm,),
        in_specs=[row(D_MODEL),
                  pl.BlockSpec((1, D_MODEL), lambda i: (0, 0)),
                  pl.BlockSpec((D_MODEL, _C_END), lambda i: (0, 0)),
                  pl.BlockSpec((_C_QR, D_MODEL), lambda i: (0, 0)),
                  tab, tab],
        out_specs=head_specs + tail_specs,
        out_shape=head_shapes + tail_shapes,
        compiler_params=pltpu.CompilerParams(dimension_semantics=("parallel",),
                                             vmem_limit_bytes=VMEM_LIMIT),
        name="inproj",
    )(x, g, w_ext, w_ext_t, cos_t, sin_t)


def _lambda_value(lamv_ref):
    lv = lamv_ref[...]
    s1 = jnp.sum(lv[0:1] * lv[1:2], axis=-1, keepdims=True)
    s2 = jnp.sum(lv[2:3] * lv[3:4], axis=-1, keepdims=True)
    return jnp.exp(s1) - jnp.exp(s2) + LAM_INIT


def _attn_prompt_kernel(qi_tab, ki_tab, qt_ref, k_ref, vt_ref, rel_ref, relm_ref, off_ref,
                        lamv_ref, subg_ref, o_ref, m_sc, l_sc, acc_sc):
    hd = pl.program_id(1)
    t = pl.program_id(2)
    qi = qi_tab[t]
    ki = ki_tab[t]

    @pl.when(ki == 0)
    def _():
        m_sc[...] = jnp.full(m_sc.shape, NEG_INF, F32)
        l_sc[...] = jnp.zeros(l_sc.shape, F32)
        acc_sc[...] = jnp.zeros(acc_sc.shape, F32)

    off = off_ref[hd, t]

    def update(rel):
        qt = qt_ref[0]
        k = k_ref[...]
        vt = vt_ref[0]
        row = lax.broadcasted_iota(I32, qt.shape, 0)
        for c in range(2):
            qc = jnp.where((row >= c * DA_HD) & (row < (c + 1) * DA_HD), qt, jnp.zeros_like(qt))
            tt = jnp.dot(k, qc, preferred_element_type=F32) + rel
            m_old = m_sc[c]
            m_new = jnp.maximum(m_old, jnp.max(tt, axis=0, keepdims=True) + off)
            alpha = jnp.exp2(m_old - m_new)
            p = jnp.exp2(tt + (off - m_new))
            l_sc[c] = alpha * l_sc[c] + jnp.sum(p, axis=0, keepdims=True)
            acc_sc[c] = alpha * acc_sc[c] + jnp.dot(vt, p.astype(BF16), preferred_element_type=F32)
            m_sc[c] = m_new

    @pl.when(ki < qi)
    def _():
        update(rel_ref[0])

    @pl.when(ki == qi)
    def _():
        update(relm_ref[0])
        lam = _lambda_value(lamv_ref)
        o = acc_sc[0] / l_sc[0] - lam * (acc_sc[1] / l_sc[1])
        ms = jnp.mean(o * o, axis=0, keepdims=True)
        a = o * lax.rsqrt(ms + NORM_EPS) * subg_ref[...] * (1.0 - LAM_INIT)
        o_ref[0] = a.astype(o_ref.dtype)


def _attn_prompt(qat, kab, vat, lamv, subg, *, batch, seq, tq):
    nq = seq // tq
    steps = [(i, j) for i in range(nq) for j in range(i + 1)]
    qi_tab = jnp.asarray([s[0] for s in steps], I32)
    ki_tab = jnp.asarray([s[1] for s in steps], I32)
    slopes = (_alibi_slopes().astype(np.float64) * LOG2_E).astype(np.float32)
    dist = (np.arange(tq)[None, :] - np.arange(tq)[:, None]).astype(np.float32)
    rel = -slopes[:, None, None] * dist[None]
    relm = np.where(dist[None] >= 0, rel, np.float32(NEG_INF)).astype(np.float32)
    blk = np.asarray([(s[0] - s[1]) * tq for s in steps], np.float32)
    off = -slopes[:, None] * blk[None, :]
    qspec = pl.BlockSpec((1, DA_VD, tq), lambda b, h, t, qt, kt: (b, h, qt[t]))
    kspec = pl.BlockSpec((tq, DA_VD), lambda b, h, t, qt, kt: (b * nq + kt[t], h))
    vspec = pl.BlockSpec((1, DA_VD, tq), lambda b, h, t, qt, kt: (b, h, kt[t]))
    relspec = pl.BlockSpec((1, tq, tq), lambda b, h, t, qt, kt: (h, 0, 0))
    grid_spec = pltpu.PrefetchScalarGridSpec(
        num_scalar_prefetch=2,
        grid=(batch, DA_HEADS, len(steps)),
        in_specs=[qspec, kspec, vspec, relspec, relspec,
                  pl.BlockSpec(memory_space=pltpu.SMEM),
                  pl.BlockSpec((4, DA_HD), lambda b, h, t, qt, kt: (0, 0)),
                  pl.BlockSpec((DA_VD, 1), lambda b, h, t, qt, kt: (0, 0))],
        out_specs=qspec,
        scratch_shapes=[pltpu.VMEM((2, 1, tq), F32), pltpu.VMEM((2, 1, tq), F32),
                        pltpu.VMEM((2, DA_VD, tq), F32)],
    )
    return pl.pallas_call(
        _attn_prompt_kernel,
        grid_spec=grid_spec,
        out_shape=jax.ShapeDtypeStruct((batch, DA_WIDTH, seq), BF16),
        compiler_params=pltpu.CompilerParams(
            dimension_semantics=("parallel", "parallel", "arbitrary"),
            vmem_limit_bytes=VMEM_LIMIT),
        name="attn_prompt",
    )(qi_tab, ki_tab, qat, kab, vat, jnp.asarray(rel), jnp.asarray(relm), jnp.asarray(off),
      lamv, subg.reshape(DA_VD, 1))


def _ret_prompt_kernel(q_ref, k_ref, v_ref, g_ref, dec_ref, qdec_ref, kdec_ref, sdec_ref,
                       r_ref, s_ref, *, batch):
    c = pl.program_id(0)

    @pl.when(c == 0)
    def _():
        s_ref[...] = jnp.zeros(s_ref.shape, F32)

    qdec = qdec_ref[...]
    kdec = kdec_ref[...]
    for b in range(batch):
        q = q_ref[b].astype(F32)
        k = k_ref[b].astype(F32)
        qd = (q * qdec).astype(BF16)
        kd = (k * kdec).astype(BF16)
        qb = q_ref[b]
        kb = k_ref[b]
        for h in range(RET_HEADS):
            ks = slice(h * RET_DK, (h + 1) * RET_DK)
            vs = slice(h * RET_DV, (h + 1) * RET_DV)
            vh = v_ref[b, :, vs]
            qk = lax.dot_general(qb[:, ks], kb[:, ks], (((1,), (1,)), ((), ())),
                                 preferred_element_type=F32) * dec_ref[h]
            s_old = s_ref[b, h]
            o = jnp.dot(qk.astype(BF16), vh, preferred_element_type=F32)
            o = o + jnp.dot(qd[:, ks], s_old.astype(BF16), preferred_element_type=F32)
            s_ref[b, h] = sdec_ref[h] * s_old + lax.dot_general(
                kd[:, ks], vh, (((0,), (0,)), ((), ())), preferred_element_type=F32)
            gate = _silu(g_ref[b, :, vs].astype(F32))
            r_ref[b, :, vs] = (_rms(o) * gate).astype(r_ref.dtype)


def _ret_tables(length):
    log_g = jnp.log(1.0 - 2.0 ** (-5.0 - jnp.arange(RET_HEADS, dtype=F32)))
    idx = jnp.arange(length, dtype=F32)
    diff = idx[:, None] - idx[None, :]
    dec = jnp.where(diff >= 0, jnp.exp(jnp.maximum(diff, 0.0)[None] * log_g[:, None, None]), 0.0)
    qdec = jnp.exp((idx + 1.0)[:, None] * log_g[None, :])
    kdec = jnp.exp((length - 1.0 - idx)[:, None] * log_g[None, :])
    sdec = jnp.exp(length * log_g)
    rep = lambda a: jnp.repeat(a, RET_DK, axis=1)
    sdec_b = jnp.broadcast_to(sdec[:, None, None], (RET_HEADS, RET_DK, RET_DV))
    return dec, rep(qdec), rep(kdec), sdec_b


def _ret_prompt(qr, kr, vr, gr, *, batch, seq):
    nc = seq // RET_CHUNK
    dec, qdec, kdec, sdec = _ret_tables(RET_CHUNK)
    q3 = qr.reshape(batch, seq, RET_QK_WIDTH)
    k3 = kr.reshape(batch, seq, RET_QK_WIDTH)
    v3 = vr.reshape(batch, seq, RET_WIDTH)
    g3 = gr.reshape(batch, seq, RET_WIDTH)
    blk = lambda w: pl.BlockSpec((batch, RET_CHUNK, w), lambda c: (0, c, 0))
    const = lambda shape: pl.BlockSpec(shape, lambda c: (0,) * len(shape))
    r, s = pl.pallas_call(
        functools.partial(_ret_prompt_kernel, batch=batch),
        grid=(nc,),
        in_specs=[blk(RET_QK_WIDTH), blk(RET_QK_WIDTH), blk(RET_WIDTH), blk(RET_WIDTH),
                  const((RET_HEADS, RET_CHUNK, RET_CHUNK)),
                  const((RET_CHUNK, RET_QK_WIDTH)), const((RET_CHUNK, RET_QK_WIDTH)),
                  const((RET_HEADS, RET_DK, RET_DV))],
        out_specs=(blk(RET_WIDTH), const((batch, RET_HEADS, RET_DK, RET_DV))),
        out_shape=(jax.ShapeDtypeStruct((batch, seq, RET_WIDTH), BF16),
                   jax.ShapeDtypeStruct((batch, RET_HEADS, RET_DK, RET_DV), F32)),
        compiler_params=pltpu.CompilerParams(dimension_semantics=("arbitrary",),
                                             vmem_limit_bytes=VMEM_LIMIT),
        name="ret_prompt",
    )(q3, k3, v3, g3, dec, qdec, kdec, sdec)
    return r.reshape(batch * seq, RET_WIDTH), s


def _attn_sample_kernel(pt_ref, q_ref, kn_ref, vn_ref, rel_ref, slope_ref, lamv_ref, subg_ref,
                        *rest, pages_per_step, past_len, n_steps):
    g_pages = pages_per_step
    k_refs = rest[:g_pages]
    v_refs = rest[g_pages:2 * g_pages]
    o_ref = rest[2 * g_pages]
    s_sc, wnew_sc, acc_sc = rest[2 * g_pages + 1:]
    j = pl.program_id(1)
    rows = 2 * DA_HEADS

    def q_rows():
        col_group = lax.broadcasted_iota(I32, (rows, DA_WIDTH), 1) // DA_HD
        row_id = lax.broadcasted_iota(I32, (rows, DA_WIDTH), 0)
        want = 2 * (row_id % DA_HEADS) + row_id // DA_HEADS
        qb = jnp.broadcast_to(q_ref[0], (rows, DA_WIDTH))
        return jnp.where(col_group == want, qb, 0.0)

    @pl.when(j < n_steps)
    def _():
        qr = q_rows().astype(BF16)
        for g in range(g_pages):
            page = j * g_pages + g
            s = jnp.dot(qr, k_refs[g][0].astype(BF16), preferred_element_type=F32)
            base = (page * LANES - past_len).astype(F32)
            s_sc[page] = s + (rel_ref[...] + slope_ref[...] * base)

    @pl.when(j == n_steps - 1)
    def _():
        s_all = s_sc[...]
        s_new = jnp.sum(q_rows() * kn_ref[0], axis=-1, keepdims=True)
        m = jnp.maximum(jnp.max(jnp.max(s_all, axis=0), axis=-1, keepdims=True), s_new)
        p = jnp.exp(s_all - m)
        p_new = jnp.exp(s_new - m)
        l = jnp.sum(jnp.sum(p, axis=0), axis=-1, keepdims=True) + p_new
        wn = p / l
        wn_new = p_new / l
        lam = _lambda_value(lamv_ref)
        s_sc[:, 0:DA_HEADS, :] = wn[:, 0:DA_HEADS, :] - lam * wn[:, DA_HEADS:rows, :]
        s_sc[:, DA_HEADS:rows, :] = jnp.zeros((s_sc.shape[0], DA_HEADS, LANES), F32)
        wnew_sc[...] = wn_new[0:DA_HEADS] - lam * wn_new[DA_HEADS:rows]
        acc_sc[...] = jnp.zeros(acc_sc.shape, F32)

    @pl.when(j >= n_steps)
    def _():
        lane = lax.broadcasted_iota(I32, (rows, LANES), 1)
        own = lane % DA_HEADS == lax.broadcasted_iota(I32, (rows, LANES), 0)
        acc = acc_sc[...]
        for g in range(g_pages):
            page = (j - n_steps) * g_pages + g
            w = s_sc[page]
            w_rows = jnp.concatenate(
                [jnp.where(own, jnp.take_along_axis(w, (LANES // DA_HEADS) * c + lane // DA_HEADS,
                                                    axis=1), 0.0)
                 for c in range(DA_HEADS)], axis=-1)
            acc = acc + jnp.dot(w_rows.astype(BF16), v_refs[g][0].astype(BF16),
                                preferred_element_type=F32)
        acc_sc[...] = acc

    @pl.when(j == 2 * n_steps - 1)
    def _():
        acc = acc_sc[...]
        w_new = wnew_sc[...]
        v_new = vn_ref[0]
        outs = []
        for h in range(DA_HEADS):
            vs = slice(h * DA_VD, (h + 1) * DA_VD)
            o = acc[h:h + 1, :] + w_new[h:h + 1, :] * v_new[:, vs]
            outs.append(_rms(o) * subg_ref[...] * (1.0 - LAM_INIT))
        o_ref[0] = jnp.concatenate(outs, axis=-1)


def _attn_sample(q, k_new, v_new, cache_kt, cache_vr, page_table, lamv, subg, *, pages_per_step):
    nb, n_pages = page_table.shape
    page = cache_kt.shape[2]
    assert page == LANES
    past_len = n_pages * page
    g_pages = pages_per_step
    n_steps = n_pages // g_pages
    rows = 2 * DA_HEADS
    slope_rows = np.tile(_alibi_slopes(), 2)[:, None]
    rel = slope_rows * np.arange(page, dtype=np.float32)[None, :]
    pt = page_table.reshape(-1).astype(I32)
    row3 = lambda: pl.BlockSpec((1, 1, DA_WIDTH), lambda b, j, pt: (b, 0, 0))
    const2 = lambda shape: pl.BlockSpec(shape, lambda b, j, pt: (0, 0))

    def k_spec(g):
        return pl.BlockSpec(
            (1, DA_WIDTH, page),
            lambda b, j, pt: (pt[b * n_pages + jnp.minimum(j, n_steps - 1) * g_pages + g], 0, 0))

    def v_spec(g):
        return pl.BlockSpec(
            (1, page * DA_HEADS, DA_VD),
            lambda b, j, pt: (pt[b * n_pages + jnp.maximum(j - n_steps, 0) * g_pages + g], 0, 0))

    grid_spec = pltpu.PrefetchScalarGridSpec(
        num_scalar_prefetch=1,
        grid=(nb, 2 * n_steps),
        in_specs=[row3(), row3(), row3(), const2((rows, page)), const2((rows, 1)),
                  const2((4, DA_HD)), const2((1, DA_VD))]
                 + [k_spec(g) for g in range(g_pages)] + [v_spec(g) for g in range(g_pages)],
        out_specs=row3(),
        scratch_shapes=[pltpu.VMEM((n_pages, rows, page), F32), pltpu.VMEM((DA_HEADS, 1), F32),
                        pltpu.VMEM((rows, DA_VD), F32)],
    )
    out = pl.pallas_call(
        functools.partial(_attn_sample_kernel, pages_per_step=g_pages, past_len=past_len,
                          n_steps=n_steps),
        grid_spec=grid_spec,
        out_shape=jax.ShapeDtypeStruct((nb, 1, DA_WIDTH), F32),
        compiler_params=pltpu.CompilerParams(dimension_semantics=("parallel", "arbitrary"),
                                             vmem_limit_bytes=VMEM_LIMIT),
        name="attn_sample",
    )(pt, q.reshape(nb, 1, DA_WIDTH), k_new.reshape(nb, 1, DA_WIDTH), v_new.reshape(nb, 1, DA_WIDTH),
      jnp.asarray(rel), jnp.asarray(slope_rows), lamv, subg,
      *([cache_kt] * g_pages), *([cache_vr] * g_pages))
    return out.reshape(nb, DA_WIDTH)


def _ret_sample_kernel(q_ref, k_ref, qc_ref, kc_ref, v_ref, g_ref, s_ref, qdec_ref, sdec_ref,
                       r_ref, so_ref):
    q = q_ref[0]
    k = k_ref[0]
    qd_col = _bf(qc_ref[0] * qdec_ref[...])
    k_col = kc_ref[0]
    outs = []
    for h in range(RET_HEADS):
        ks = slice(h * RET_DK, (h + 1) * RET_DK)
        vs = slice(h * RET_DV, (h + 1) * RET_DV)
        vh = v_ref[0, :, vs]
        s_old = s_ref[0, h]
        qk = jnp.sum(q[:, ks] * k[:, ks], axis=-1, keepdims=True)
        o = qk * vh + jnp.sum(qd_col[ks, :] * _bf(s_old), axis=0, keepdims=True)
        so_ref[0, h] = sdec_ref[h] * s_old + k_col[ks, :] * vh
        outs.append(_rms(o) * _silu(g_ref[0, :, vs]))
    r_ref[0] = jnp.concatenate(outs, axis=-1)


def _ret_sample(qr, kr, vr, gr, state):
    nb = qr.shape[0]
    _, qdec, _, sdec = _ret_tables(1)
    row3 = lambda w: pl.BlockSpec((1, 1, w), lambda b: (b, 0, 0))
    col3 = pl.BlockSpec((1, RET_QK_WIDTH, 1), lambda b: (b, 0, 0))
    st = pl.BlockSpec((1, RET_HEADS, RET_DK, RET_DV), lambda b: (b, 0, 0, 0))
    r, s_new = pl.pallas_call(
        _ret_sample_kernel,
        grid=(nb,),
        in_specs=[row3(RET_QK_WIDTH), row3(RET_QK_WIDTH), col3, col3, row3(RET_WIDTH),
                  row3(RET_WIDTH), st,
                  pl.BlockSpec((RET_QK_WIDTH, 1), lambda b: (0, 0)),
                  pl.BlockSpec((RET_HEADS, RET_DK, RET_DV), lambda b: (0, 0, 0))],
        out_specs=(row3(RET_WIDTH), st),
        out_shape=(jax.ShapeDtypeStruct((nb, 1, RET_WIDTH), F32),
                   jax.ShapeDtypeStruct((nb, RET_HEADS, RET_DK, RET_DV), F32)),
        compiler_params=pltpu.CompilerParams(dimension_semantics=("parallel",)),
        name="ret_sample",
    )(qr.reshape(nb, 1, -1), kr.reshape(nb, 1, -1), qr.reshape(nb, -1, 1), kr.reshape(nb, -1, 1),
      vr.reshape(nb, 1, -1), gr.reshape(nb, 1, -1), state, qdec.reshape(-1, 1), sdec)
    return r.reshape(nb, RET_WIDTH), s_new


def _outproj_kernel(a_ref, r_ref, x_ref, wo_ref, g_ref, wr_ref, br_ref,
                    x1_ref, hfp_ref, ti_ref, tg_ref, *, a_transposed):
    if a_transposed:
        mix_a = lax.dot_general(a_ref[0], wo_ref[:DA_WIDTH, :], (((0,), (0,)), ((), ())),
                                preferred_element_type=F32)
    else:
        mix_a = _mm(a_ref[...], wo_ref[:DA_WIDTH, :])
    mix = mix_a + _mm(r_ref[...], wo_ref[DA_WIDTH:, :])
    x1 = x_ref[...] + mix
    x1_ref[...] = x1
    hf = _rms(x1) * g_ref[...]
    hfp_ref[0] = _pack_pairs(hf[:, :HALF])
    hfp_ref[1] = _pack_pairs(hf[:, HALF:])
    logits = _mm(hf, wr_ref[...]) + br_ref[...]
    lane = lax.broadcasted_iota(I32, logits.shape, 1)
    work = logits
    top_v, top_i = [], []
    for _ in range(TOP_K):
        m = jnp.max(work, axis=-1, keepdims=True)
        idx = jnp.min(jnp.where(work == m, lane, LANES), axis=-1, keepdims=True)
        top_v.append(m)
        top_i.append(idx)
        work = jnp.where(lane == idx, -jnp.inf, work)
    es = [jnp.exp(v - top_v[0]) for v in top_v]
    denom = es[0] + es[1] + es[2] + es[3]
    ti = jnp.zeros(logits.shape, I32)
    tg = jnp.zeros(logits.shape, F32)
    for kk in range(TOP_K):
        ti = jnp.where(lane == kk, top_i[kk], ti)
        tg = jnp.where(lane == kk, es[kk] / denom, tg)
    ti_ref[...] = ti
    tg_ref[...] = tg


def _outproj(a, r, x, w_out, g, w_router, b_router, *, tm):
    n = x.shape[0]
    row = lambda w: pl.BlockSpec((tm, w), lambda i: (i, 0))
    const = lambda shape: pl.BlockSpec(shape, lambda i: (0, 0))
    a_transposed = a.ndim == 3
    if a_transposed:
        n_s = a.shape[2] // tm
        a_spec = pl.BlockSpec((1, DA_WIDTH, tm), lambda i: (i // n_s, 0, i % n_s))
    else:
        a_spec = row(DA_WIDTH)
    return pl.pallas_call(
        functools.partial(_outproj_kernel, a_transposed=a_transposed),
        grid=(n // tm,),
        in_specs=[a_spec, row(RET_WIDTH), row(D_MODEL), const((D_MODEL, D_MODEL)),
                  const((1, D_MODEL)), const((D_MODEL, LANES)), const((1, LANES))],
        out_specs=(row(D_MODEL), pl.BlockSpec((2, tm, QUARTER), lambda i: (0, i, 0)),
                   row(LANES), row(LANES)),
        out_shape=(jax.ShapeDtypeStruct((n, D_MODEL), F32),
                   jax.ShapeDtypeStruct((2, n, QUARTER), I32),
                   jax.ShapeDtypeStruct((n, LANES), I32),
                   jax.ShapeDtypeStruct((n, LANES), F32)),
        compiler_params=pltpu.CompilerParams(dimension_semantics=("parallel",),
                                             vmem_limit_bytes=VMEM_LIMIT),
        name="outproj",
    )(a, r, x, w_out, g, w_router, b_router)


def _sc_gather(x, idx):
    m = idx.shape[0]
    cols = x.shape[1]
    mesh = plsc.VectorSubcoreMesh(core_axis_name="c", subcore_axis_name="s")

    @pl.kernel(out_type=jax.ShapeDtypeStruct((m, cols), x.dtype), mesh=mesh)
    def gather_kernel(x_hbm, i_hbm, o_hbm):
        def body(i_vmem, o_vmem):
            pltpu.sync_copy(x_hbm.at[i_vmem.at[0]], o_vmem)

        pltpu.emit_pipeline(
            body,
            grid=(m // SC_WINDOW,),
            in_specs=[pl.BlockSpec((1, SC_WINDOW), lambda i: (0, i))],
            out_specs=[pl.BlockSpec((SC_WINDOW, cols), lambda i: (i, 0))],
            core_axis_name=("c", "s"),
            dimension_semantics=(pltpu.PARALLEL,),
        )(i_hbm, o_hbm)

    return gather_kernel(x, idx.reshape(1, m))


def _filler(n, modulus):
    return jnp.arange(n, dtype=I32) % modulus


def _sc_scatter(x, idx, out_rows):
    m = idx.shape[0]
    cols = x.shape[1]
    n_blk = x.shape[0] // 2 // SC_WINDOW
    mesh = plsc.VectorSubcoreMesh(core_axis_name="c", subcore_axis_name="s")

    @pl.kernel(out_type=jax.ShapeDtypeStruct((out_rows, cols), x.dtype), mesh=mesh)
    def scatter_kernel(x_hbm, i_hbm, o_hbm):
        def body(x_vmem, i_vmem):
            pltpu.sync_copy(x_vmem, o_hbm.at[i_vmem.at[0]])

        pltpu.emit_pipeline(
            body,
            grid=(m // SC_WINDOW,),
            in_specs=[pl.BlockSpec((SC_WINDOW, cols),
                                   lambda i: ((i // (TOP_K * n_blk)) * n_blk + i % n_blk, 0)),
                      pl.BlockSpec((1, SC_WINDOW), lambda i: (0, i))],
            out_specs=[],
            core_axis_name=("c", "s"),
            dimension_semantics=(pltpu.PARALLEL,),
        )(x_hbm, i_hbm)

    return scatter_kernel(x, idx.reshape(1, m))


def _rank_kernel(ti_ref, tri_ref, rank_ref, cnt_ref, base_sc):
    @pl.when(pl.program_id(0) == 0)
    def _():
        base_sc[...] = jnp.zeros(base_sc.shape, F32)

    ti = ti_ref[...]
    lane = lax.broadcasted_iota(I32, ti.shape, 1)
    picks = [lane == ti[:, k:k + 1] for k in range(TOP_K)]
    onehot = jnp.zeros(ti.shape, F32)
    for pk in picks:
        onehot = onehot + jnp.where(pk, 1.0, 0.0)
    pos = jnp.dot(tri_ref[...], onehot.astype(BF16), preferred_element_type=F32) + base_sc[...]
    rank = jnp.zeros(ti.shape, I32)
    for k, pk in enumerate(picks):
        r_k = jnp.sum(jnp.where(pk, pos, 0.0), axis=-1, keepdims=True)
        rank = jnp.where(lane == k, r_k.astype(I32), rank)
    rank_ref[...] = rank
    base_sc[...] = base_sc[...] + jnp.sum(onehot, axis=0, keepdims=True)
    cnt_ref[...] = base_sc[...]


def _rank(ti, *, tr):
    n = ti.shape[0]
    tri = jnp.asarray(np.tril(np.ones((tr, tr), np.float32), -1), BF16)
    return pl.pallas_call(
        _rank_kernel,
        grid=(n // tr,),
        in_specs=[pl.BlockSpec((tr, LANES), lambda i: (i, 0)),
                  pl.BlockSpec((tr, tr), lambda i: (0, 0))],
        out_specs=(pl.BlockSpec((tr, LANES), lambda i: (i, 0)),
                   pl.BlockSpec((1, LANES), lambda i: (0, 0))),
        out_shape=(jax.ShapeDtypeStruct((n, LANES), I32), jax.ShapeDtypeStruct((1, LANES), F32)),
        scratch_shapes=[pltpu.VMEM((1, LANES), F32)],
        compiler_params=pltpu.CompilerParams(dimension_semantics=("arbitrary",)),
        name="route_rank",
    )(ti, tri)


def _prep_expert_weights(wup_ref, wdn_ref, wupb, wdnb):
    lane = lax.broadcasted_iota(I32, (PREP_ROWS, LANES), 1)
    first_half = lane < LANES // 2
    idx_even = (2 * lane) % LANES
    idx_odd = (2 * lane + 1) % LANES

    def body(r, carry):
        rows = pl.ds(pl.multiple_of(r * PREP_ROWS, PREP_ROWS), PREP_ROWS)
        for ct in range(D_FF // LANES):
            a = wup_ref[0, rows, 2 * ct * LANES:(2 * ct + 1) * LANES]
            b = wup_ref[0, rows, (2 * ct + 1) * LANES:(2 * ct + 2) * LANES]
            even = jnp.where(first_half, jnp.take_along_axis(a, idx_even, axis=1),
                             jnp.take_along_axis(b, idx_even, axis=1))
            odd = jnp.where(first_half, jnp.take_along_axis(a, idx_odd, axis=1),
                            jnp.take_along_axis(b, idx_odd, axis=1))
            wupb[rows, ct * LANES:(ct + 1) * LANES] = even.astype(BF16)
            wupb[rows, D_FF + ct * LANES:D_FF + (ct + 1) * LANES] = odd.astype(BF16)
        wdnb[rows, :] = wdn_ref[0, rows, :].astype(BF16)
        return carry

    lax.fori_loop(0, D_MODEL // PREP_ROWS, body, 0)


def _moe_kernel(be_ref, na_ref, nv_ref, xs_ref, wup_ref, bup_ref, wdn_ref, bdn_ref, out_ref,
                wupb, wdnb):
    i = pl.program_id(0)
    n_act = na_ref[0]
    e = be_ref[jnp.minimum(i, n_act - 1)]
    e_prev = be_ref[jnp.maximum(i - 1, 0)]

    @pl.when(jnp.logical_and(i < n_act, jnp.logical_or(i == 0, e != e_prev)))
    def _():
        _prep_expert_weights(wup_ref, wdn_ref, wupb, wdnb)

    @pl.when(i < n_act)
    def _():
        lo0, hi0 = _unpack_pairs(xs_ref[0])
        lo1, hi1 = _unpack_pairs(xs_ref[1])
        x = jnp.concatenate([lo0, hi0, lo1, hi1], axis=-1)
        live = lax.broadcasted_iota(I32, x.shape, 0) < nv_ref[i]
        x = jnp.where(live, x, 0.0).astype(BF16)
        u = jnp.dot(x, wupb[...], preferred_element_type=F32) + bup_ref[0]
        g = jnp.minimum(u[:, :D_FF], SWIGLU_LIMIT)
        lin = jnp.clip(u[:, D_FF:], -SWIGLU_LIMIT, SWIGLU_LIMIT)
        y = g * (1.0 / (1.0 + jnp.exp(-SWIGLU_ALPHA * g))) * (lin + 1.0)
        o = jnp.dot(y.astype(BF16), wdnb[...], preferred_element_type=F32) + bdn_ref[0]
        out_ref[0] = _pack_pairs(o[:, :HALF])
        out_ref[1] = _pack_pairs(o[:, HALF:])

    @pl.when(i >= n_act)
    def _():
        out_ref[...] = jnp.zeros(out_ref.shape, I32)


def _moe(block_e, n_active, n_valid, xs, w_up, b_up_d, w_down, b_down):
    n_blocks = block_e.shape[0]
    n_slots = n_blocks * MOE_TM

    def act(i, na):
        return jnp.minimum(i, na[0] - 1)

    grid_spec = pltpu.PrefetchScalarGridSpec(
        num_scalar_prefetch=3,
        grid=(n_blocks,),
        in_specs=[
            pl.BlockSpec((2, MOE_TM, QUARTER), lambda i, be, na, nv: (0, act(i, na), 0)),
            pl.BlockSpec((1, D_MODEL, 2 * D_FF), lambda i, be, na, nv: (be[act(i, na)], 0, 0)),
            pl.BlockSpec((1, 1, 2 * D_FF), lambda i, be, na, nv: (be[act(i, na)], 0, 0)),
            pl.BlockSpec((1, D_FF, D_MODEL), lambda i, be, na, nv: (be[act(i, na)], 0, 0)),
            pl.BlockSpec((1, 1, D_MODEL), lambda i, be, na, nv: (be[act(i, na)], 0, 0)),
        ],
        out_specs=pl.BlockSpec((2, MOE_TM, QUARTER), lambda i, be, na, nv: (0, i, 0)),
        scratch_shapes=[pltpu.VMEM((D_MODEL, 2 * D_FF), BF16), pltpu.VMEM((D_FF, D_MODEL), BF16)],
    )
    return pl.pallas_call(
        _moe_kernel,
        grid_spec=grid_spec,
        out_shape=jax.ShapeDtypeStruct((2, n_slots, QUARTER), I32),
        compiler_params=pltpu.CompilerParams(dimension_semantics=("arbitrary",),
                                             vmem_limit_bytes=VMEM_LIMIT),
        name="moe_ffn",
    )(block_e, n_active, n_valid, xs, w_up, b_up_d, w_down, b_down)


def _slot_layout(counts, top_i, rank, n_blocks):
    experts = jnp.arange(N_EXPERTS, dtype=I32)
    padded = (counts + MOE_TM - 1) // MOE_TM * MOE_TM
    pad_end = jnp.cumsum(padded)
    pad_start = pad_end - padded
    pick = top_i[:, :, None] == experts[None, None, :]
    dest = rank + jnp.sum(jnp.where(pick, pad_start[None, None, :], 0), axis=-1)
    blk_start = jnp.arange(n_blocks, dtype=I32) * MOE_TM
    block_e = jnp.minimum(jnp.sum((blk_start[:, None] >= pad_end[None, :]).astype(I32), axis=1),
                          N_EXPERTS - 1)
    mine = block_e[:, None] == experts[None, :]
    count_b = jnp.sum(jnp.where(mine, counts[None, :], 0), axis=1)
    start_b = jnp.sum(jnp.where(mine, pad_start[None, :], 0), axis=1)
    live = jnp.clip(count_b - (blk_start - start_b), 0, MOE_TM)
    n_active = (jnp.sum(padded) // MOE_TM).astype(I32).reshape(1)
    return dest.astype(I32), block_e.astype(I32), live.astype(I32), n_active


def _combine_kernel(yg_ref, tg_ref, x1_ref, g_ref, o_ref):
    tg = tg_ref[...]
    parts = [jnp.zeros((x1_ref.shape[0], QUARTER), F32) for _ in range(4)]
    for kk in range(TOP_K):
        gate = tg[:, kk:kk + 1]
        lo0, hi0 = _unpack_pairs(yg_ref[0, kk])
        lo1, hi1 = _unpack_pairs(yg_ref[1, kk])
        for j, piece in enumerate((lo0, hi0, lo1, hi1)):
            parts[j] = parts[j] + gate * piece
    x2 = x1_ref[...] + jnp.concatenate(parts, axis=-1)
    o_ref[...] = _rms(x2) * g_ref[...]


def _combine(yg, tg, x1, g, *, tm, row_offset):
    n = x1.shape[0]
    blk0 = row_offset // tm
    return pl.pallas_call(
        _combine_kernel,
        grid=(n // tm,),
        in_specs=[pl.BlockSpec((2, TOP_K, tm, QUARTER), lambda i: (0, 0, i + blk0, 0)),
                  pl.BlockSpec((tm, LANES), lambda i: (i, 0)),
                  pl.BlockSpec((tm, D_MODEL), lambda i: (i, 0)),
                  pl.BlockSpec((1, D_MODEL), lambda i: (0, 0))],
        out_specs=pl.BlockSpec((tm, D_MODEL), lambda i: (i, 0)),
        out_shape=jax.ShapeDtypeStruct((n, D_MODEL), F32),
        compiler_params=pltpu.CompilerParams(dimension_semantics=("parallel",),
                                             vmem_limit_bytes=VMEM_LIMIT),
        name="combine",
    )(yg, tg, x1, g)


def _extend_w_in(w):
    swap = np.arange(RET_QK_WIDTH).reshape(RET_HEADS, 2, RET_DK // 2)[:, ::-1, :].reshape(-1)
    qa, ka, va = w[:, 0:512], w[:, 512:1024], w[:, 1024:1536]
    qr, kr = w[:, 1536:1792], w[:, 1792:2048]
    vr, gr = w[:, 2048:2560], w[:, 2560:3072]
    sa = DA_HD ** -0.5
    sr = RET_DK ** -0.5
    return jnp.concatenate([qa * sa, ka, va, qr, qr[:, swap], kr * sr, kr[:, swap] * sr, vr, gr], axis=1)


def _rotary_tables(pos):
    half = RET_DK // 2
    inv = ROPE_BASE ** (-jnp.arange(half, dtype=F32) / half)
    ang = pos.astype(F32)[:, None] * inv[None, :]
    cos = jnp.cos(ang)
    sin = jnp.sin(ang)
    cos_t = jnp.tile(jnp.concatenate([cos, cos], axis=1), (1, RET_HEADS))
    sin_t = jnp.tile(jnp.concatenate([-sin, sin], axis=1), (1, RET_HEADS))
    return cos_t, sin_t


def kernel(x_prompt, x_sample, cache_k, cache_v, state_ret, page_table, norm_attn_g, w_in, lam_q1,
           lam_k1, lam_q2, lam_k2, da_subln_g, w_out, norm_ffn_g, w_router, b_router, w_up, b_up,
           w_down, b_down, norm_final_g):
    return _forward(x_prompt, x_sample, cache_k, cache_v, state_ret, page_table, norm_attn_g, w_in,
                    lam_q1, lam_k1, lam_q2, lam_k2, da_subln_g, w_out, norm_ffn_g, w_router,
                    b_router, w_up, b_up, w_down, b_down, norm_final_g,
                    tm=512, tq=512, pages_per_step=32, gather=_sc_gather, scatter=_sc_scatter)


def _forward(x_prompt, x_sample, cache_k, cache_v, state_ret, page_table, norm_attn_g, w_in, lam_q1,
             lam_k1, lam_q2, lam_k2, da_subln_g, w_out, norm_ffn_g, w_router, b_router, w_up, b_up,
             w_down, b_down, norm_final_g, *, tm, tq, pages_per_step, gather, scatter):
    batch, seq, _ = x_prompt.shape
    nb = x_sample.shape[0]
    n_pool, page = cache_k.shape[1], cache_k.shape[2]
    past_len = page_table.shape[1] * page
    n_prompt = batch * seq

    w_ext = _extend_w_in(w_in[0])
    w_ext_b = w_ext.astype(BF16)
    w_ext_t = jnp.concatenate([(w_ext[:, :_C_KA] * LOG2_E).astype(BF16), w_ext_b[:, _C_KA:_C_QR]],
                              axis=1).T
    g_attn = norm_attn_g[0].reshape(1, D_MODEL)
    g_ffn = norm_ffn_g[0].reshape(1, D_MODEL)
    g_fin = norm_final_g.reshape(1, D_MODEL)
    lamv = jnp.stack([lam_q1[0], lam_k1[0], lam_q2[0], lam_k2[0]]).astype(F32)
    subg = da_subln_g[0].reshape(1, DA_VD)
    w_o_b = w_out[0].astype(BF16)
    w_r = jnp.pad(w_router[0], ((0, 0), (0, LANES - N_EXPERTS))).astype(BF16)
    b_r = jnp.pad(b_router[0], (0, LANES - N_EXPERTS), constant_values=NEG_INF).reshape(1, LANES)
    assert cache_k.shape[0] == 1 and w_up.shape[0] == 1
    w_up_e = w_up.reshape(N_EXPERTS, D_MODEL, 2 * D_FF)
    b_up_d = jnp.concatenate([b_up[0][:, 0::2], b_up[0][:, 1::2]], axis=-1).reshape(N_EXPERTS, 1, 2 * D_FF)
    w_down_e = w_down.reshape(N_EXPERTS, D_FF, D_MODEL)
    b_dn = b_down.reshape(N_EXPERTS, 1, D_MODEL)
    cos_p, sin_p = _rotary_tables(jnp.arange(seq))
    cos_s, sin_s = _rotary_tables(past_len + jnp.zeros((nb,), I32))

    xp = x_prompt.reshape(n_prompt, D_MODEL)
    qat, kat, kab, va, vat, qr, kr, vr, gr = _inproj(xp, g_attn, w_ext_b, w_ext_t, cos_p, sin_p,
                                                     tm=tm, seq=seq)
    a_p = _attn_prompt(qat, kab, vat, lamv, subg, batch=batch, seq=seq, tq=tq)
    r_p, ret_p = _ret_prompt(qr, kr, vr, gr, batch=batch, seq=seq)
    x1_p, hfp_p, ti_p, tg_p = _outproj(a_p, r_p, xp, w_o_b, g_ffn, w_r, b_r, tm=tm)

    xs_ = x_sample.reshape(nb, D_MODEL)
    qa_s, ka_s, va_s, qr_s, kr_s, vr_s, gr_s = _inproj(xs_, g_attn, w_ext_b, w_ext_t, cos_s, sin_s,
                                                       tm=nb)
    ckt = jnp.transpose(cache_k.reshape(n_pool, page, DA_WIDTH), (0, 2, 1))
    cvr = cache_v.reshape(n_pool, page * DA_HEADS, DA_VD)
    a_s = _attn_sample(qa_s, ka_s, va_s, ckt, cvr, page_table, lamv, subg,
                       pages_per_step=pages_per_step)
    r_s, ret_s = _ret_sample(qr_s, kr_s, vr_s, gr_s,
                             state_ret.reshape(nb, RET_HEADS, RET_DK, RET_DV))
    x1_s, hfp_s, ti_s, tg_s = _outproj(a_s, r_s, xs_, w_o_b, g_ffn, w_r, b_r, tm=nb)

    n_tok = n_prompt + nb
    tok_pad = -(-n_tok // ROUTE_TILE) * ROUTE_TILE
    pad = tok_pad - n_tok
    hfp = jnp.concatenate([hfp_p, hfp_s, jnp.zeros((2, pad, QUARTER), I32)], axis=1)
    ti_all = jnp.concatenate([ti_p, ti_s, jnp.full((pad, LANES), LANES - 1, I32)], axis=0)
    rank, cnt = _rank(ti_all, tr=ROUTE_TILE)
    counts = cnt[0, :N_EXPERTS].astype(I32)
    n_blocks = -(-n_tok * TOP_K // MOE_TM) + N_EXPERTS
    n_slots = n_blocks * MOE_TM
    dest, block_e, n_live, n_active = _slot_layout(counts, ti_all[:n_tok, :TOP_K],
                                                   rank[:n_tok, :TOP_K], n_blocks)
    slots_pad = -(-(n_slots + 1) // (SC_UNIT // 2)) * (SC_UNIT // 2)
    dump = n_slots + _filler(pad, slots_pad - n_slots)
    dest_sc = jnp.concatenate([dest, jnp.broadcast_to(dump[:, None], (pad, TOP_K))], axis=0).T
    xs_sorted = scatter(hfp.reshape(2 * tok_pad, QUARTER),
                        jnp.concatenate([dest_sc, slots_pad + dest_sc]).reshape(-1), 2 * slots_pad)
    out_sorted = _moe(block_e, n_active, n_live, xs_sorted.reshape(2, slots_pad, QUARTER),
                      w_up_e, b_up_d, w_down_e, b_dn)
    fill = jnp.broadcast_to(_filler(pad, n_slots)[:, None], (pad, TOP_K))
    dest_g = jnp.concatenate([dest, fill], axis=0).T
    yg = gather(out_sorted.reshape(2 * n_slots, QUARTER),
                jnp.concatenate([dest_g, n_slots + dest_g]).reshape(-1))
    yg = yg.reshape(2, TOP_K, tok_pad, QUARTER)
    y_p = _combine(yg, tg_p, x1_p, g_fin, tm=tm, row_offset=0)
    y_s = _combine(yg, tg_s, x1_s, g_fin, tm=nb, row_offset=n_prompt)

    depth = 1
    return (y_p.reshape(batch, seq, D_MODEL),
            y_s.reshape(nb, 1, D_MODEL),
            jnp.transpose(kat.reshape(depth, batch, DA_HEADS, 2, DA_HD, seq), (0, 1, 5, 2, 3, 4)),
            va.reshape(depth, batch, seq, DA_HEADS, DA_VD),
            ret_p.reshape(depth, batch, RET_HEADS, RET_DK, RET_DV),
            ka_s.reshape(depth, nb, 1, DA_HEADS, 2, DA_HD),
            va_s.reshape(depth, nb, 1, DA_HEADS, DA_VD),
            ret_s.reshape(depth, nb, RET_HEADS, RET_DK, RET_DV))
```

```python
import functools
import math

import numpy as np
import jax
import jax.numpy as jnp
from jax import lax
from jax.experimental import pallas as pl
from jax.experimental.pallas import tpu as pltpu
from jax.experimental.pallas import tpu_sc as plsc

F32 = jnp.float32
BF16 = jnp.bfloat16
I32 = jnp.int32

D_MODEL = 1024
DA_HEADS = 4
DA_VD = 128
DA_HD = 64
DA_WIDTH = DA_HEADS * DA_VD
RET_HEADS = 4
RET_DV = 128
RET_DK = 64
RET_WIDTH = RET_HEADS * RET_DV
RET_QK_WIDTH = RET_HEADS * RET_DK
RET_CHUNK = 128
ROPE_BASE = 10000.0
N_EXPERTS = 32
TOP_K = 4
D_FF = 1024
SWIGLU_LIMIT = 7.0
SWIGLU_ALPHA = 1.702
NORM_EPS = 1e-5
NEG_INF = -1e30
LAM_INIT = 0.8 - 0.6 * math.exp(-0.3 * 0)
LOG2_E = math.log2(math.e)
LANES = 128
HALF = D_MODEL // 2
QUARTER = D_MODEL // 4
MOE_TM = 512
PREP_ROWS = 64
ROUTE_TILE = 512
SC_WINDOW = 128
SC_SUBCORES = 32
SC_UNIT = SC_WINDOW * SC_SUBCORES
VMEM_LIMIT = 56 * 1024 * 1024

_C_QA, _C_KA, _C_VA = 0, 512, 1024
_C_QR, _C_QRS, _C_KR, _C_KRS = 1536, 1792, 2048, 2304
_C_VR, _C_GR, _C_END = 2560, 3072, 3584


def _mm(a, b):
    return jnp.dot(a.astype(BF16), b.astype(BF16), preferred_element_type=F32)


def _bf(x):
    return x.astype(BF16).astype(F32)


def _rms(x):
    return x * lax.rsqrt(jnp.mean(x * x, axis=-1, keepdims=True) + NORM_EPS)


def _silu(g):
    return g / (1.0 + jnp.exp(-g))


def _pack_pairs(x):
    w = x.shape[-1] // 2
    lo = lax.bitcast_convert_type(x[:, :w].astype(BF16).astype(F32), jnp.uint32)
    hi = lax.bitcast_convert_type(x[:, w:].astype(BF16).astype(F32), jnp.uint32)
    return lax.bitcast_convert_type(hi | (lo >> 16), I32)


def _unpack_pairs(p):
    u = lax.bitcast_convert_type(p, jnp.uint32)
    lo = lax.bitcast_convert_type(u << 16, F32)
    hi = lax.bitcast_convert_type(u & jnp.uint32(0xFFFF0000), F32)
    return lo, hi


def _alibi_slopes():
    return np.asarray([2.0 ** (-8.0 * (h + 1) / DA_HEADS) for h in range(DA_HEADS)], np.float32)


def _inproj_kernel(x_ref, g_ref, w_ref, wt_ref, cos_ref, sin_ref, *outs, transposed):
    hm = (_rms(x_ref[...]) * g_ref[...]).astype(BF16)

    def mm(lo, hi):
        return _mm(hm, w_ref[:, lo:hi])

    def mm_t(lo, hi):
        return lax.dot_general(wt_ref[lo:hi, :], hm, (((1,), (1,)), ((), ())),
                               preferred_element_type=F32)

    if transposed:
        qat_ref, kat_ref, kab_ref, va_ref, vat_ref, qr_ref, kr_ref, vr_ref, gr_ref = outs
        qat_ref[0] = mm_t(_C_QA, _C_KA).astype(BF16)
        kat_ref[0] = mm_t(_C_KA, _C_VA)
        kab_ref[...] = mm(_C_KA, _C_VA).astype(BF16)
        va = mm(_C_VA, _C_QR)
        for h in range(DA_HEADS):
            va_ref[pl.ds(h, va.shape[0], stride=DA_HEADS), :] = va[:, h * DA_VD:(h + 1) * DA_VD]
        vat_ref[0] = mm_t(_C_VA, _C_QR).astype(BF16)
    else:
        qa_ref, ka_ref, va_ref, qr_ref, kr_ref, vr_ref, gr_ref = outs
        qa_ref[...] = mm(_C_QA, _C_KA)
        ka_ref[...] = mm(_C_KA, _C_VA)
        va_ref[...] = mm(_C_VA, _C_QR)
    cos = cos_ref[...]
    sin = sin_ref[...]
    qr_ref[...] = (mm(_C_QR, _C_QRS) * cos + mm(_C_QRS, _C_KR) * sin).astype(qr_ref.dtype)
    kr_ref[...] = (mm(_C_KR, _C_KRS) * cos + mm(_C_KRS, _C_VR) * sin).astype(kr_ref.dtype)
    vr_ref[...] = mm(_C_VR, _C_GR).astype(vr_ref.dtype)
    gr_ref[...] = mm(_C_GR, _C_END).astype(gr_ref.dtype)


def _inproj(x, g, w_ext, w_ext_t, cos_t, sin_t, *, tm, seq=None):
    n = x.shape[0]
    n_tab = cos_t.shape[0] // tm
    transposed = seq is not None
    act = BF16 if transposed else F32
    row = lambda w: pl.BlockSpec((tm, w), lambda i: (i, 0))
    tab = pl.BlockSpec((tm, RET_QK_WIDTH), lambda i: (i % n_tab, 0))
    sds = jax.ShapeDtypeStruct
    tail_shapes = (sds((n, RET_QK_WIDTH), act), sds((n, RET_QK_WIDTH), act),
                   sds((n, RET_WIDTH), act), sds((n, RET_WIDTH), act))
    tail_specs = (row(RET_QK_WIDTH), row(RET_QK_WIDTH), row(RET_WIDTH), row(RET_WIDTH))
    if transposed:
        n_s = seq // tm
        tr = pl.BlockSpec((1, DA_WIDTH, tm), lambda i: (i // n_s, 0, i % n_s))
        head_shapes = (sds((n // seq, DA_WIDTH, seq), BF16), sds((n // seq, DA_WIDTH, seq), F32),
                       sds((n, DA_WIDTH), BF16), sds((n * DA_HEADS, DA_VD), F32),
                       sds((n // seq, DA_WIDTH, seq), BF16))
        head_specs = (tr, tr, row(DA_WIDTH),
                      pl.BlockSpec((tm * DA_HEADS, DA_VD), lambda i: (i, 0)), tr)
    else:
        head_shapes = (sds((n, DA_WIDTH), F32),) * 3
        head_specs = (row(DA_WIDTH),) * 3
    return pl.pallas_call(
        functools.partial(_inproj_kernel, transposed=transposed),
        grid=(n // tm,),
        in_specs=[row(D_MODEL),
                  pl.BlockSpec((1, D_MODEL), lambda i: (0, 0)),
                  pl.BlockSpec((D_MODEL, _C_END), lambda i: (0, 0)),
                  pl.BlockSpec((_C_QR, D_MODEL), lambda i: (0, 0)),
                  tab, tab],
        out_specs=head_specs + tail_specs,
        out_shape=head_shapes + tail_shapes,
        compiler_params=pltpu.CompilerParams(dimension_semantics=("parallel",),
                                             vmem_limit_bytes=VMEM_LIMIT),
        name="inproj",
    )(x, g, w_ext, w_ext_t, cos_t, sin_t)


def _lambda_value(lamv_ref):
    lv = lamv_ref[...]
    s1 = jnp.sum(lv[0:1] * lv[1:2], axis=-1, keepdims=True)
    s2 = jnp.sum(lv[2:3] * lv[3:4], axis=-1, keepdims=True)
    return jnp.exp(s1) - jnp.exp(s2) + LAM_INIT


def _attn_prompt_kernel(qi_tab, ki_tab, qt_ref, k_ref, vt_ref, rel_ref, relm_ref, off_ref,
                        lamv_ref, subg_ref, o_ref, m_sc, l_sc, acc_sc):
    hd = pl.program_id(1)
    t = pl.program_id(2)
    qi = qi_tab[t]
    ki = ki_tab[t]

    @pl.when(ki == 0)
    def _():
        m_sc[...] = jnp.full(m_sc.shape, NEG_INF, F32)
        l_sc[...] = jnp.zeros(l_sc.shape, F32)
        acc_sc[...] = jnp.zeros(acc_sc.shape, F32)

    off = off_ref[hd, t]

    def update(rel):
        qt = qt_ref[0]
        k = k_ref[...]
        vt = vt_ref[0]
        row = lax.broadcasted_iota(I32, qt.shape, 0)
        qcs = [jnp.where((row >= c * DA_HD) & (row < (c + 1) * DA_HD), qt, jnp.zeros_like(qt))
               for c in range(2)]
        tts = [jnp.dot(k, qc, preferred_element_type=F32) + rel for qc in qcs]
        m_olds = [m_sc[c] for c in range(2)]
        m_news = [jnp.maximum(m_olds[c], jnp.max(tts[c], axis=0, keepdims=True) + off)
                  for c in range(2)]
        ps = [jnp.exp2(tts[c] + (off - m_news[c])) for c in range(2)]
        for c in range(2):
            alpha = jnp.exp2(m_olds[c] - m_news[c])
            l_sc[c] = alpha * l_sc[c] + jnp.sum(ps[c], axis=0, keepdims=True)
            acc_sc[c] = alpha * acc_sc[c] + jnp.dot(vt, ps[c].astype(BF16),
                                                    preferred_element_type=F32)
            m_sc[c] = m_news[c]

    @pl.when(ki < qi)
    def _():
        update(rel_ref[0])

    @pl.when(ki == qi)
    def _():
        update(relm_ref[0])
        lam = _lambda_value(lamv_ref)
        o = acc_sc[0] / l_sc[0] - lam * (acc_sc[1] / l_sc[1])
        ms = jnp.mean(o * o, axis=0, keepdims=True)
        a = o * lax.rsqrt(ms + NORM_EPS) * subg_ref[...] * (1.0 - LAM_INIT)
        o_ref[0] = a.astype(o_ref.dtype)


def _attn_prompt(qat, kab, vat, lamv, subg, *, batch, seq, tq):
    nq = seq // tq
    steps = [(i, j) for i in range(nq) for j in range(i + 1)]
    qi_tab = jnp.asarray([s[0] for s in steps], I32)
    ki_tab = jnp.asarray([s[1] for s in steps], I32)
    slopes = (_alibi_slopes().astype(np.float64) * LOG2_E).astype(np.float32)
    dist = (np.arange(tq)[None, :] - np.arange(tq)[:, None]).astype(np.float32)
    rel = -slopes[:, None, None] * dist[None]
    relm = np.where(dist[None] >= 0, rel, np.float32(NEG_INF)).astype(np.float32)
    blk = np.asarray([(s[0] - s[1]) * tq for s in steps], np.float32)
    off = -slopes[:, None] * blk[None, :]
    qspec = pl.BlockSpec((1, DA_VD, tq), lambda b, h, t, qt, kt: (b, h, qt[t]))
    kspec = pl.BlockSpec((tq, DA_VD), lambda b, h, t, qt, kt: (b * nq + kt[t], h))
    vspec = pl.BlockSpec((1, DA_VD, tq), lambda b, h, t, qt, kt: (b, h, kt[t]))
    relspec = pl.BlockSpec((1, tq, tq), lambda b, h, t, qt, kt: (h, 0, 0))
    grid_spec = pltpu.PrefetchScalarGridSpec(
        num_scalar_prefetch=2,
        grid=(batch, DA_HEADS, len(steps)),
        in_specs=[qspec, kspec, vspec, relspec, relspec,
                  pl.BlockSpec(memory_space=pltpu.SMEM),
                  pl.BlockSpec((4, DA_HD), lambda b, h, t, qt, kt: (0, 0)),
                  pl.BlockSpec((DA_VD, 1), lambda b, h, t, qt, kt: (0, 0))],
        out_specs=qspec,
        scratch_shapes=[pltpu.VMEM((2, 1, tq), F32), pltpu.VMEM((2, 1, tq), F32),
                        pltpu.VMEM((2, DA_VD, tq), F32)],
    )
    return pl.pallas_call(
        _attn_prompt_kernel,
        grid_spec=grid_spec,
        out_shape=jax.ShapeDtypeStruct((batch, DA_WIDTH, seq), BF16),
        compiler_params=pltpu.CompilerParams(
            dimension_semantics=("parallel", "parallel", "arbitrary"),
            vmem_limit_bytes=VMEM_LIMIT),
        name="attn_prompt",
    )(qi_tab, ki_tab, qat, kab, vat, jnp.asarray(rel), jnp.asarray(relm), jnp.asarray(off),
      lamv, subg.reshape(DA_VD, 1))


def _ret_prompt_kernel(q_ref, k_ref, v_ref, g_ref, dec_ref, qdec_ref, kdec_ref, sdec_ref,
                       r_ref, s_ref, *, batch):
    c = pl.program_id(0)

    @pl.when(c == 0)
    def _():
        s_ref[...] = jnp.zeros(s_ref.shape, F32)

    qdec = qdec_ref[...]
    kdec = kdec_ref[...]
    for b in range(batch):
        q = q_ref[b].astype(F32)
        k = k_ref[b].astype(F32)
        qd = (q * qdec).astype(BF16)
        kd = (k * kdec).astype(BF16)
        qb = q_ref[b]
        kb = k_ref[b]
        for h in range(RET_HEADS):
            ks = slice(h * RET_DK, (h + 1) * RET_DK)
            vs = slice(h * RET_DV, (h + 1) * RET_DV)
            vh = v_ref[b, :, vs]
            qk = lax.dot_general(qb[:, ks], kb[:, ks], (((1,), (1,)), ((), ())),
                                 preferred_element_type=F32) * dec_ref[h]
            s_old = s_ref[b, h]
            o = jnp.dot(qk.astype(BF16), vh, preferred_element_type=F32)
            o = o + jnp.dot(qd[:, ks], s_old.astype(BF16), preferred_element_type=F32)
            s_ref[b, h] = sdec_ref[h] * s_old + lax.dot_general(
                kd[:, ks], vh, (((0,), (0,)), ((), ())), preferred_element_type=F32)
            gate = _silu(g_ref[b, :, vs].astype(F32))
            r_ref[b, :, vs] = (_rms(o) * gate).astype(r_ref.dtype)


def _ret_tables(length):
    log_g = jnp.log(1.0 - 2.0 ** (-5.0 - jnp.arange(RET_HEADS, dtype=F32)))
    idx = jnp.arange(length, dtype=F32)
    diff = idx[:, None] - idx[None, :]
    dec = jnp.where(diff >= 0, jnp.exp(jnp.maximum(diff, 0.0)[None] * log_g[:, None, None]), 0.0)
    qdec = jnp.exp((idx + 1.0)[:, None] * log_g[None, :])
    kdec = jnp.exp((length - 1.0 - idx)[:, None] * log_g[None, :])
    sdec = jnp.exp(length * log_g)
    rep = lambda a: jnp.repeat(a, RET_DK, axis=1)
    sdec_b = jnp.broadcast_to(sdec[:, None, None], (RET_HEADS, RET_DK, RET_DV))
    return dec, rep(qdec), rep(kdec), sdec_b


def _ret_prompt(qr, kr, vr, gr, *, batch, seq):
    nc = seq // RET_CHUNK
    dec, qdec, kdec, sdec = _ret_tables(RET_CHUNK)
    q3 = qr.reshape(batch, seq, RET_QK_WIDTH)
    k3 = kr.reshape(batch, seq, RET_QK_WIDTH)
    v3 = vr.reshape(batch, seq, RET_WIDTH)
    g3 = gr.reshape(batch, seq, RET_WIDTH)
    blk = lambda w: pl.BlockSpec((batch, RET_CHUNK, w), lambda c: (0, c, 0))
    const = lambda shape: pl.BlockSpec(shape, lambda c: (0,) * len(shape))
    r, s = pl.pallas_call(
        functools.partial(_ret_prompt_kernel, batch=batch),
        grid=(nc,),
        in_specs=[blk(RET_QK_WIDTH), blk(RET_QK_WIDTH), blk(RET_WIDTH), blk(RET_WIDTH),
                  const((RET_HEADS, RET_CHUNK, RET_CHUNK)),
                  const((RET_CHUNK, RET_QK_WIDTH)), const((RET_CHUNK, RET_QK_WIDTH)),
                  const((RET_HEADS, RET_DK, RET_DV))],
        out_specs=(blk(RET_WIDTH), const((batch, RET_HEADS, RET_DK, RET_DV))),
        out_shape=(jax.ShapeDtypeStruct((batch, seq, RET_WIDTH), BF16),
                   jax.ShapeDtypeStruct((batch, RET_HEADS, RET_DK, RET_DV), F32)),
        compiler_params=pltpu.CompilerParams(dimension_semantics=("arbitrary",),
                                             vmem_limit_bytes=VMEM_LIMIT),
        name="ret_prompt",
    )(q3, k3, v3, g3, dec, qdec, kdec, sdec)
    return r.reshape(batch * seq, RET_WIDTH), s


def _attn_sample_kernel(pt_ref, q_ref, kn_ref, vn_ref, rel_ref, slope_ref, lamv_ref, subg_ref,
                        *rest, pages_per_step, past_len, n_steps):
    g_pages = pages_per_step
    k_refs = rest[:g_pages]
    v_refs = rest[g_pages:2 * g_pages]
    o_ref = rest[2 * g_pages]
    s_sc, wnew_sc, acc_sc = rest[2 * g_pages + 1:]
    j = pl.program_id(1)
    rows = 2 * DA_HEADS

    def q_rows():
        col_group = lax.broadcasted_iota(I32, (rows, DA_WIDTH), 1) // DA_HD
        row_id = lax.broadcasted_iota(I32, (rows, DA_WIDTH), 0)
        want = 2 * (row_id % DA_HEADS) + row_id // DA_HEADS
        qb = jnp.broadcast_to(q_ref[0], (rows, DA_WIDTH))
        return jnp.where(col_group == want, qb, 0.0)

    @pl.when(j < n_steps)
    def _():
        qr = q_rows().astype(BF16)
        for g in range(g_pages):
            page = j * g_pages + g
            s = jnp.dot(qr, k_refs[g][0].astype(BF16), preferred_element_type=F32)
            base = (page * LANES - past_len).astype(F32)
            s_sc[page] = s + (rel_ref[...] + slope_ref[...] * base)

    @pl.when(j == n_steps - 1)
    def _():
        s_all = s_sc[...]
        s_new = jnp.sum(q_rows() * kn_ref[0], axis=-1, keepdims=True)
        m = jnp.maximum(jnp.max(jnp.max(s_all, axis=0), axis=-1, keepdims=True), s_new)
        p = jnp.exp(s_all - m)
        p_new = jnp.exp(s_new - m)
        l = jnp.sum(jnp.sum(p, axis=0), axis=-1, keepdims=True) + p_new
        wn = p / l
        wn_new = p_new / l
        lam = _lambda_value(lamv_ref)
        s_sc[:, 0:DA_HEADS, :] = wn[:, 0:DA_HEADS, :] - lam * wn[:, DA_HEADS:rows, :]
        s_sc[:, DA_HEADS:rows, :] = jnp.zeros((s_sc.shape[0], DA_HEADS, LANES), F32)
        wnew_sc[...] = wn_new[0:DA_HEADS] - lam * wn_new[DA_HEADS:rows]
        acc_sc[...] = jnp.zeros(acc_sc.shape, F32)

    @pl.when(j >= n_steps)
    def _():
        lane = lax.broadcasted_iota(I32, (rows, LANES), 1)
        own = lane % DA_HEADS == lax.broadcasted_iota(I32, (rows, LANES), 0)
        acc = acc_sc[...]
        for g in range(g_pages):
            page = (j - n_steps) * g_pages + g
            w = s_sc[page]
            w_rows = jnp.concatenate(
                [jnp.where(own, jnp.take_along_axis(w, (LANES // DA_HEADS) * c + lane // DA_HEADS,
                                                    axis=1), 0.0)
                 for c in range(DA_HEADS)], axis=-1)
            acc = acc + jnp.dot(w_rows.astype(BF16), v_refs[g][0].astype(BF16),
                                preferred_element_type=F32)
        acc_sc[...] = acc

    @pl.when(j == 2 * n_steps - 1)
    def _():
        acc = acc_sc[...]
        w_new = wnew_sc[...]
        v_new = vn_ref[0]
        outs = []
        for h in range(DA_HEADS):
            vs = slice(h * DA_VD, (h + 1) * DA_VD)
            o = acc[h:h + 1, :] + w_new[h:h + 1, :] * v_new[:, vs]
            outs.append(_rms(o) * subg_ref[...] * (1.0 - LAM_INIT))
        o_ref[0] = jnp.concatenate(outs, axis=-1)


def _attn_sample(q, k_new, v_new, cache_kt, cache_vr, page_table, lamv, subg, *, pages_per_step):
    nb, n_pages = page_table.shape
    page = cache_kt.shape[2]
    assert page == LANES
    past_len = n_pages * page
    g_pages = pages_per_step
    n_steps = n_pages // g_pages
    rows = 2 * DA_HEADS
    slope_rows = np.tile(_alibi_slopes(), 2)[:, None]
    rel = slope_rows * np.arange(page, dtype=np.float32)[None, :]
    pt = page_table.reshape(-1).astype(I32)
    row3 = lambda: pl.BlockSpec((1, 1, DA_WIDTH), lambda b, j, pt: (b, 0, 0))
    const2 = lambda shape: pl.BlockSpec(shape, lambda b, j, pt: (0, 0))

    def k_spec(g):
        return pl.BlockSpec(
            (1, DA_WIDTH, page),
            lambda b, j, pt: (pt[b * n_pages + jnp.minimum(j, n_steps - 1) * g_pages + g], 0, 0))

    def v_spec(g):
        return pl.BlockSpec(
            (1, page * DA_HEADS, DA_VD),
            lambda b, j, pt: (pt[b * n_pages + jnp.maximum(j - n_steps, 0) * g_pages + g], 0, 0))

    grid_spec = pltpu.PrefetchScalarGridSpec(
        num_scalar_prefetch=1,
        grid=(nb, 2 * n_steps),
        in_specs=[row3(), row3(), row3(), const2((rows, page)), const2((rows, 1)),
                  const2((4, DA_HD)), const2((1, DA_VD))]
                 + [k_spec(g) for g in range(g_pages)] + [v_spec(g) for g in range(g_pages)],
        out_specs=row3(),
        scratch_shapes=[pltpu.VMEM((n_pages, rows, page), F32), pltpu.VMEM((DA_HEADS, 1), F32),
                        pltpu.VMEM((rows, DA_VD), F32)],
    )
    out = pl.pallas_call(
        functools.partial(_attn_sample_kernel, pages_per_step=g_pages, past_len=past_len,
                          n_steps=n_steps),
        grid_spec=grid_spec,
        out_shape=jax.ShapeDtypeStruct((nb, 1, DA_WIDTH), F32),
        compiler_params=pltpu.CompilerParams(dimension_semantics=("parallel", "arbitrary"),
                                             vmem_limit_bytes=VMEM_LIMIT),
        name="attn_sample",
    )(pt, q.reshape(nb, 1, DA_WIDTH), k_new.reshape(nb, 1, DA_WIDTH), v_new.reshape(nb, 1, DA_WIDTH),
      jnp.asarray(rel), jnp.asarray(slope_rows), lamv, subg,
      *([cache_kt] * g_pages), *([cache_vr] * g_pages))
    return out.reshape(nb, DA_WIDTH)


def _ret_sample_kernel(q_ref, k_ref, qc_ref, kc_ref, v_ref, g_ref, s_ref, qdec_ref, sdec_ref,
                       r_ref, so_ref):
    q = q_ref[0]
    k = k_ref[0]
    qd_col = _bf(qc_ref[0] * qdec_ref[...])
    k_col = kc_ref[0]
    outs = []
    for h in range(RET_HEADS):
        ks = slice(h * RET_DK, (h + 1) * RET_DK)
        vs = slice(h * RET_DV, (h + 1) * RET_DV)
        vh = v_ref[0, :, vs]
        s_old = s_ref[0, h]
        qk = jnp.sum(q[:, ks] * k[:, ks], axis=-1, keepdims=True)
        o = qk * vh + jnp.sum(qd_col[ks, :] * _bf(s_old), axis=0, keepdims=True)
        so_ref[0, h] = sdec_ref[h] * s_old + k_col[ks, :] * vh
        outs.append(_rms(o) * _silu(g_ref[0, :, vs]))
    r_ref[0] = jnp.concatenate(outs, axis=-1)


def _ret_sample(qr, kr, vr, gr, state):
    nb = qr.shape[0]
    _, qdec, _, sdec = _ret_tables(1)
    row3 = lambda w: pl.BlockSpec((1, 1, w), lambda b: (b, 0, 0))
    col3 = pl.BlockSpec((1, RET_QK_WIDTH, 1), lambda b: (b, 0, 0))
    st = pl.BlockSpec((1, RET_HEADS, RET_DK, RET_DV), lambda b: (b, 0, 0, 0))
    r, s_new = pl.pallas_call(
        _ret_sample_kernel,
        grid=(nb,),
        in_specs=[row3(RET_QK_WIDTH), row3(RET_QK_WIDTH), col3, col3, row3(RET_WIDTH),
                  row3(RET_WIDTH), st,
                  pl.BlockSpec((RET_QK_WIDTH, 1), lambda b: (0, 0)),
                  pl.BlockSpec((RET_HEADS, RET_DK, RET_DV), lambda b: (0, 0, 0))],
        out_specs=(row3(RET_WIDTH), st),
        out_shape=(jax.ShapeDtypeStruct((nb, 1, RET_WIDTH), F32),
                   jax.ShapeDtypeStruct((nb, RET_HEADS, RET_DK, RET_DV), F32)),
        compiler_params=pltpu.CompilerParams(dimension_semantics=("parallel",)),
        name="ret_sample",
    )(qr.reshape(nb, 1, -1), kr.reshape(nb, 1, -1), qr.reshape(nb, -1, 1), kr.reshape(nb, -1, 1),
      vr.reshape(nb, 1, -1), gr.reshape(nb, 1, -1), state, qdec.reshape(-1, 1), sdec)
    return r.reshape(nb, RET_WIDTH), s_new


def _outproj_kernel(a_ref, r_ref, x_ref, wo_ref, g_ref, wr_ref, br_ref,
                    x1_ref, hfp_ref, ti_ref, tg_ref, *, a_transposed):
    if a_transposed:
        mix_a = lax.dot_general(a_ref[0], wo_ref[:DA_WIDTH, :], (((0,), (0,)), ((), ())),
                                preferred_element_type=F32)
    else:
        mix_a = _mm(a_ref[...], wo_ref[:DA_WIDTH, :])
    mix = mix_a + _mm(r_ref[...], wo_ref[DA_WIDTH:, :])
    x1 = x_ref[...] + mix
    x1_ref[...] = x1
    hf = _rms(x1) * g_ref[...]
    hfp_ref[0] = _pack_pairs(hf[:, :HALF])
    hfp_ref[1] = _pack_pairs(hf[:, HALF:])
    logits = _mm(hf, wr_ref[...]) + br_ref[...]
    lane = lax.broadcasted_iota(I32, logits.shape, 1)
    work = logits
    top_v, top_i = [], []
    for _ in range(TOP_K):
        m = jnp.max(work, axis=-1, keepdims=True)
        idx = jnp.min(jnp.where(work == m, lane, LANES), axis=-1, keepdims=True)
        top_v.append(m)
        top_i.append(idx)
        work = jnp.where(lane == idx, -jnp.inf, work)
    es = [jnp.exp(v - top_v[0]) for v in top_v]
    denom = es[0] + es[1] + es[2] + es[3]
    ti = jnp.zeros(logits.shape, I32)
    tg = jnp.zeros(logits.shape, F32)
    for kk in range(TOP_K):
        ti = jnp.where(lane == kk, top_i[kk], ti)
        tg = jnp.where(lane == kk, es[kk] / denom, tg)
    ti_ref[...] = ti
    tg_ref[...] = tg


def _outproj(a, r, x, w_out, g, w_router, b_router, *, tm):
    n = x.shape[0]
    row = lambda w: pl.BlockSpec((tm, w), lambda i: (i, 0))
    const = lambda shape: pl.BlockSpec(shape, lambda i: (0, 0))
    a_transposed = a.ndim == 3
    if a_transposed:
        n_s = a.shape[2] // tm
        a_spec = pl.BlockSpec((1, DA_WIDTH, tm), lambda i: (i // n_s, 0, i % n_s))
    else:
        a_spec = row(DA_WIDTH)
    return pl.pallas_call(
        functools.partial(_outproj_kernel, a_transposed=a_transposed),
        grid=(n // tm,),
        in_specs=[a_spec, row(RET_WIDTH), row(D_MODEL), const((D_MODEL, D_MODEL)),
                  const((1, D_MODEL)), const((D_MODEL, LANES)), const((1, LANES))],
        out_specs=(row(D_MODEL), pl.BlockSpec((2, tm, QUARTER), lambda i: (0, i, 0)),
                   row(LANES), row(LANES)),
        out_shape=(jax.ShapeDtypeStruct((n, D_MODEL), F32),
                   jax.ShapeDtypeStruct((2, n, QUARTER), I32),
                   jax.ShapeDtypeStruct((n, LANES), I32),
                   jax.ShapeDtypeStruct((n, LANES), F32)),
        compiler_params=pltpu.CompilerParams(dimension_semantics=("parallel",),
                                             vmem_limit_bytes=VMEM_LIMIT),
        name="outproj",
    )(a, r, x, w_out, g, w_router, b_router)


def _sc_gather(x, idx):
    m = idx.shape[0]
    cols = x.shape[1]
    mesh = plsc.VectorSubcoreMesh(core_axis_name="c", subcore_axis_name="s")

    @pl.kernel(out_type=jax.ShapeDtypeStruct((m, cols), x.dtype), mesh=mesh)
    def gather_kernel(x_hbm, i_hbm, o_hbm):
        def body(i_vmem, o_vmem):
            pltpu.sync_copy(x_hbm.at[i_vmem.at[0]], o_vmem)

        pltpu.emit_pipeline(
            body,
            grid=(m // SC_WINDOW,),
            in_specs=[pl.BlockSpec((1, SC_WINDOW), lambda i: (0, i))],
            out_specs=[pl.BlockSpec((SC_WINDOW, cols), lambda i: (i, 0))],
            core_axis_name=("c", "s"),
            dimension_semantics=(pltpu.PARALLEL,),
        )(i_hbm, o_hbm)

    return gather_kernel(x, idx.reshape(1, m))


def _filler(n, modulus):
    return jnp.arange(n, dtype=I32) % modulus


def _sc_scatter(x, idx, out_rows):
    m = idx.shape[0]
    cols = x.shape[1]
    n_blk = x.shape[0] // 2 // SC_WINDOW
    mesh = plsc.VectorSubcoreMesh(core_axis_name="c", subcore_axis_name="s")

    @pl.kernel(out_type=jax.ShapeDtypeStruct((out_rows, cols), x.dtype), mesh=mesh)
    def scatter_kernel(x_hbm, i_hbm, o_hbm):
        def body(x_vmem, i_vmem):
            pltpu.sync_copy(x_vmem, o_hbm.at[i_vmem.at[0]])

        pltpu.emit_pipeline(
            body,
            grid=(m // SC_WINDOW,),
            in_specs=[pl.BlockSpec((SC_WINDOW, cols),
                                   lambda i: ((i // (TOP_K * n_blk)) * n_blk + i % n_blk, 0)),
                      pl.BlockSpec((1, SC_WINDOW), lambda i: (0, i))],
            out_specs=[],
            core_axis_name=("c", "s"),
            dimension_semantics=(pltpu.PARALLEL,),
        )(x_hbm, i_hbm)

    return scatter_kernel(x, idx.reshape(1, m))


def _rank_kernel(ti_ref, tri_ref, rank_ref, cnt_ref, base_sc):
    @pl.when(pl.program_id(0) == 0)
    def _():
        base_sc[...] = jnp.zeros(base_sc.shape, F32)

    ti = ti_ref[...]
    lane = lax.broadcasted_iota(I32, ti.shape, 1)
    picks = [lane == ti[:, k:k + 1] for k in range(TOP_K)]
    onehot = jnp.zeros(ti.shape, F32)
    for pk in picks:
        onehot = onehot + jnp.where(pk, 1.0, 0.0)
    pos = jnp.dot(tri_ref[...], onehot.astype(BF16), preferred_element_type=F32) + base_sc[...]
    rank = jnp.zeros(ti.shape, I32)
    for k, pk in enumerate(picks):
        r_k = jnp.sum(jnp.where(pk, pos, 0.0), axis=-1, keepdims=True)
        rank = jnp.where(lane == k, r_k.astype(I32), rank)
    rank_ref[...] = rank
    base_sc[...] = base_sc[...] + jnp.sum(onehot, axis=0, keepdims=True)
    cnt_ref[...] = base_sc[...]


def _rank(ti, *, tr):
    n = ti.shape[0]
    tri = jnp.asarray(np.tril(np.ones((tr, tr), np.float32), -1), BF16)
    return pl.pallas_call(
        _rank_kernel,
        grid=(n // tr,),
        in_specs=[pl.BlockSpec((tr, LANES), lambda i: (i, 0)),
                  pl.BlockSpec((tr, tr), lambda i: (0, 0))],
        out_specs=(pl.BlockSpec((tr, LANES), lambda i: (i, 0)),
                   pl.BlockSpec((1, LANES), lambda i: (0, 0))),
        out_shape=(jax.ShapeDtypeStruct((n, LANES), I32), jax.ShapeDtypeStruct((1, LANES), F32)),
        scratch_shapes=[pltpu.VMEM((1, LANES), F32)],
        compiler_params=pltpu.CompilerParams(dimension_semantics=("arbitrary",)),
        name="route_rank",
    )(ti, tri)


def _prep_expert_weights(wup_ref, wdn_ref, wupb, wdnb):
    lane = lax.broadcasted_iota(I32, (PREP_ROWS, LANES), 1)
    first_half = lane < LANES // 2
    idx_even = (2 * lane) % LANES
    idx_odd = (2 * lane + 1) % LANES

    def body(r, carry):
        rows = pl.ds(pl.multiple_of(r * PREP_ROWS, PREP_ROWS), PREP_ROWS)
        for ct in range(D_FF // LANES):
            a = wup_ref[0, rows, 2 * ct * LANES:(2 * ct + 1) * LANES]
            b = wup_ref[0, rows, (2 * ct + 1) * LANES:(2 * ct + 2) * LANES]
            even = jnp.where(first_half, jnp.take_along_axis(a, idx_even, axis=1),
                             jnp.take_along_axis(b, idx_even, axis=1))
            odd = jnp.where(first_half, jnp.take_along_axis(a, idx_odd, axis=1),
                            jnp.take_along_axis(b, idx_odd, axis=1))
            wupb[rows, ct * LANES:(ct + 1) * LANES] = even.astype(BF16)
            wupb[rows, D_FF + ct * LANES:D_FF + (ct + 1) * LANES] = odd.astype(BF16)
        wdnb[rows, :] = wdn_ref[0, rows, :].astype(BF16)
        return carry

    lax.fori_loop(0, D_MODEL // PREP_ROWS, body, 0)


def _moe_kernel(be_ref, na_ref, nv_ref, xs_ref, wup_ref, bup_ref, wdn_ref, bdn_ref, out_ref,
                wupb, wdnb):
    i = pl.program_id(0)
    n_act = na_ref[0]
    e = be_ref[jnp.minimum(i, n_act - 1)]
    e_prev = be_ref[jnp.maximum(i - 1, 0)]

    @pl.when(jnp.logical_and(i < n_act, jnp.logical_or(i == 0, e != e_prev)))
    def _():
        _prep_expert_weights(wup_ref, wdn_ref, wupb, wdnb)

    @pl.when(i < n_act)
    def _():
        lo0, hi0 = _unpack_pairs(xs_ref[0])
        lo1, hi1 = _unpack_pairs(xs_ref[1])
        x = jnp.concatenate([lo0, hi0, lo1, hi1], axis=-1)
        live = lax.broadcasted_iota(I32, x.shape, 0) < nv_ref[i]
        x = jnp.where(live, x, 0.0).astype(BF16)
        u = jnp.dot(x, wupb[...], preferred_element_type=F32) + bup_ref[0]
        g = jnp.minimum(u[:, :D_FF], SWIGLU_LIMIT)
        lin = jnp.clip(u[:, D_FF:], -SWIGLU_LIMIT, SWIGLU_LIMIT)
        y = g * (1.0 / (1.0 + jnp.exp(-SWIGLU_ALPHA * g))) * (lin + 1.0)
        o = jnp.dot(y.astype(BF16), wdnb[...], preferred_element_type=F32) + bdn_ref[0]
        out_ref[0] = _pack_pairs(o[:, :HALF])
        out_ref[1] = _pack_pairs(o[:, HALF:])

    @pl.when(i >= n_act)
    def _():
        out_ref[...] = jnp.zeros(out_ref.shape, I32)


def _moe(block_e, n_active, n_valid, xs, w_up, b_up_d, w_down, b_down):
    n_blocks = block_e.shape[0]
    n_slots = n_blocks * MOE_TM

    def act(i, na):
        return jnp.minimum(i, na[0] - 1)

    grid_spec = pltpu.PrefetchScalarGridSpec(
        num_scalar_prefetch=3,
        grid=(n_blocks,),
        in_specs=[
            pl.BlockSpec((2, MOE_TM, QUARTER), lambda i, be, na, nv: (0, act(i, na), 0)),
            pl.BlockSpec((1, D_MODEL, 2 * D_FF), lambda i, be, na, nv: (be[act(i, na)], 0, 0)),
            pl.BlockSpec((1, 1, 2 * D_FF), lambda i, be, na, nv: (be[act(i, na)], 0, 0)),
            pl.BlockSpec((1, D_FF, D_MODEL), lambda i, be, na, nv: (be[act(i, na)], 0, 0)),
            pl.BlockSpec((1, 1, D_MODEL), lambda i, be, na, nv: (be[act(i, na)], 0, 0)),
        ],
        out_specs=pl.BlockSpec((2, MOE_TM, QUARTER), lambda i, be, na, nv: (0, i, 0)),
        scratch_shapes=[pltpu.VMEM((D_MODEL, 2 * D_FF), BF16), pltpu.VMEM((D_FF, D_MODEL), BF16)],
    )
    return pl.pallas_call(
        _moe_kernel,
        grid_spec=grid_spec,
        out_shape=jax.ShapeDtypeStruct((2, n_slots, QUARTER), I32),
        compiler_params=pltpu.CompilerParams(dimension_semantics=("arbitrary",),
                                             vmem_limit_bytes=VMEM_LIMIT),
        name="moe_ffn",
    )(block_e, n_active, n_valid, xs, w_up, b_up_d, w_down, b_down)


def _slot_layout(counts, top_i, rank, n_blocks):
    experts = jnp.arange(N_EXPERTS, dtype=I32)
    padded = (counts + MOE_TM - 1) // MOE_TM * MOE_TM
    pad_end = jnp.cumsum(padded)
    pad_start = pad_end - padded
    pick = top_i[:, :, None] == experts[None, None, :]
    dest = rank + jnp.sum(jnp.where(pick, pad_start[None, None, :], 0), axis=-1)
    blk_start = jnp.arange(n_blocks, dtype=I32) * MOE_TM
    block_e = jnp.minimum(jnp.sum((blk_start[:, None] >= pad_end[None, :]).astype(I32), axis=1),
                          N_EXPERTS - 1)
    mine = block_e[:, None] == experts[None, :]
    count_b = jnp.sum(jnp.where(mine, counts[None, :], 0), axis=1)
    start_b = jnp.sum(jnp.where(mine, pad_start[None, :], 0), axis=1)
    live = jnp.clip(count_b - (blk_start - start_b), 0, MOE_TM)
    n_active = (jnp.sum(padded) // MOE_TM).astype(I32).reshape(1)
    return dest.astype(I32), block_e.astype(I32), live.astype(I32), n_active


def _combine_kernel(yg_ref, tg_ref, x1_ref, g_ref, o_ref):
    tg = tg_ref[...]
    parts = [jnp.zeros((x1_ref.shape[0], QUARTER), F32) for _ in range(4)]
    for kk in range(TOP_K):
        gate = tg[:, kk:kk + 1]
        lo0, hi0 = _unpack_pairs(yg_ref[0, kk])
        lo1, hi1 = _unpack_pairs(yg_ref[1, kk])
        for j, piece in enumerate((lo0, hi0, lo1, hi1)):
            parts[j] = parts[j] + gate * piece
    x2 = x1_ref[...] + jnp.concatenate(parts, axis=-1)
    o_ref[...] = _rms(x2) * g_ref[...]


def _combine(yg, tg, x1, g, *, tm, row_offset):
    n = x1.shape[0]
    blk0 = row_offset // tm
    return pl.pallas_call(
        _combine_kernel,
        grid=(n // tm,),
        in_specs=[pl.BlockSpec((2, TOP_K, tm, QUARTER), lambda i: (0, 0, i + blk0, 0)),
                  pl.BlockSpec((tm, LANES), lambda i: (i, 0)),
                  pl.BlockSpec((tm, D_MODEL), lambda i: (i, 0)),
                  pl.BlockSpec((1, D_MODEL), lambda i: (0, 0))],
        out_specs=pl.BlockSpec((tm, D_MODEL), lambda i: (i, 0)),
        out_shape=jax.ShapeDtypeStruct((n, D_MODEL), F32),
        compiler_params=pltpu.CompilerParams(dimension_semantics=("parallel",),
                                             vmem_limit_bytes=VMEM_LIMIT),
        name="combine",
    )(yg, tg, x1, g)


def _extend_w_in(w):
    swap = np.arange(RET_QK_WIDTH).reshape(RET_HEADS, 2, RET_DK // 2)[:, ::-1, :].reshape(-1)
    qa, ka, va = w[:, 0:512], w[:, 512:1024], w[:, 1024:1536]
    qr, kr = w[:, 1536:1792], w[:, 1792:2048]
    vr, gr = w[:, 2048:2560], w[:, 2560:3072]
    sa = DA_HD ** -0.5
    sr = RET_DK ** -0.5
    return jnp.concatenate([qa * sa, ka, va, qr, qr[:, swap], kr * sr, kr[:, swap] * sr, vr, gr], axis=1)


def _rotary_tables(pos):
    half = RET_DK // 2
    inv = ROPE_BASE ** (-jnp.arange(half, dtype=F32) / half)
    ang = pos.astype(F32)[:, None] * inv[None, :]
    cos = jnp.cos(ang)
    sin = jnp.sin(ang)
    cos_t = jnp.tile(jnp.concatenate([cos, cos], axis=1), (1, RET_HEADS))
    sin_t = jnp.tile(jnp.concatenate([-sin, sin], axis=1), (1, RET_HEADS))
    return cos_t, sin_t


def kernel(x_prompt, x_sample, cache_k, cache_v, state_ret, page_table, norm_attn_g, w_in, lam_q1,
           lam_k1, lam_q2, lam_k2, da_subln_g, w_out, norm_ffn_g, w_router, b_router, w_up, b_up,
           w_down, b_down, norm_final_g):
    return _forward(x_prompt, x_sample, cache_k, cache_v, state_ret, page_table, norm_attn_g, w_in,
                    lam_q1, lam_k1, lam_q2, lam_k2, da_subln_g, w_out, norm_ffn_g, w_router,
                    b_router, w_up, b_up, w_down, b_down, norm_final_g,
                    tm=512, tq=512, pages_per_step=32, gather=_sc_gather, scatter=_sc_scatter)


def _forward(x_prompt, x_sample, cache_k, cache_v, state_ret, page_table, norm_attn_g, w_in, lam_q1,
             lam_k1, lam_q2, lam_k2, da_subln_g, w_out, norm_ffn_g, w_router, b_router, w_up, b_up,
             w_down, b_down, norm_final_g, *, tm, tq, pages_per_step, gather, scatter):
    batch, seq, _ = x_prompt.shape
    nb = x_sample.shape[0]
    n_pool, page = cache_k.shape[1], cache_k.shape[2]
    past_len = page_table.shape[1] * page
    n_prompt = batch * seq

    w_ext = _extend_w_in(w_in[0])
    w_ext_b = w_ext.astype(BF16)
    w_ext_t = jnp.concatenate([(w_ext[:, :_C_KA] * LOG2_E).astype(BF16), w_ext_b[:, _C_KA:_C_QR]],
                              axis=1).T
    g_attn = norm_attn_g[0].reshape(1, D_MODEL)
    g_ffn = norm_ffn_g[0].reshape(1, D_MODEL)
    g_fin = norm_final_g.reshape(1, D_MODEL)
    lamv = jnp.stack([lam_q1[0], lam_k1[0], lam_q2[0], lam_k2[0]]).astype(F32)
    subg = da_subln_g[0].reshape(1, DA_VD)
    w_o_b = w_out[0].astype(BF16)
    w_r = jnp.pad(w_router[0], ((0, 0), (0, LANES - N_EXPERTS))).astype(BF16)
    b_r = jnp.pad(b_router[0], (0, LANES - N_EXPERTS), constant_values=NEG_INF).reshape(1, LANES)
    assert cache_k.shape[0] == 1 and w_up.shape[0] == 1
    w_up_e = w_up.reshape(N_EXPERTS, D_MODEL, 2 * D_FF)
    b_up_d = jnp.concatenate([b_up[0][:, 0::2], b_up[0][:, 1::2]], axis=-1).reshape(N_EXPERTS, 1, 2 * D_FF)
    w_down_e = w_down.reshape(N_EXPERTS, D_FF, D_MODEL)
    b_dn = b_down.reshape(N_EXPERTS, 1, D_MODEL)
    cos_p, sin_p = _rotary_tables(jnp.arange(seq))
    cos_s, sin_s = _rotary_tables(past_len + jnp.zeros((nb,), I32))

    xp = x_prompt.reshape(n_prompt, D_MODEL)
    qat, kat, kab, va, vat, qr, kr, vr, gr = _inproj(xp, g_attn, w_ext_b, w_ext_t, cos_p, sin_p,
                                                     tm=tm, seq=seq)
    a_p = _attn_prompt(qat, kab, vat, lamv, subg, batch=batch, seq=seq, tq=tq)
    r_p, ret_p = _ret_prompt(qr, kr, vr, gr, batch=batch, seq=seq)
    x1_p, hfp_p, ti_p, tg_p = _outproj(a_p, r_p, xp, w_o_b, g_ffn, w_r, b_r, tm=tm)

    xs_ = x_sample.reshape(nb, D_MODEL)
    qa_s, ka_s, va_s, qr_s, kr_s, vr_s, gr_s = _inproj(xs_, g_attn, w_ext_b, w_ext_t, cos_s, sin_s,
                                                       tm=nb)
    ckt = jnp.transpose(cache_k.reshape(n_pool, page, DA_WIDTH), (0, 2, 1))
    cvr = cache_v.reshape(n_pool, page * DA_HEADS, DA_VD)
    a_s = _attn_sample(qa_s, ka_s, va_s, ckt, cvr, page_table, lamv, subg,
                       pages_per_step=pages_per_step)
    r_s, ret_s = _ret_sample(qr_s, kr_s, vr_s, gr_s,
                             state_ret.reshape(nb, RET_HEADS, RET_DK, RET_DV))
    x1_s, hfp_s, ti_s, tg_s = _outproj(a_s, r_s, xs_, w_o_b, g_ffn, w_r, b_r, tm=nb)

    n_tok = n_prompt + nb
    tok_pad = -(-n_tok // ROUTE_TILE) * ROUTE_TILE
    pad = tok_pad - n_tok
    hfp = jnp.concatenate([hfp_p, hfp_s, jnp.zeros((2, pad, QUARTER), I32)], axis=1)
    ti_all = jnp.concatenate([ti_p, ti_s, jnp.full((pad, LANES), LANES - 1, I32)], axis=0)
    rank, cnt = _rank(ti_all, tr=ROUTE_TILE)
    counts = cnt[0, :N_EXPERTS].astype(I32)
    n_blocks = -(-n_tok * TOP_K // MOE_TM) + N_EXPERTS
    n_slots = n_blocks * MOE_TM
    dest, block_e, n_live, n_active = _slot_layout(counts, ti_all[:n_tok, :TOP_K],
                                                   rank[:n_tok, :TOP_K], n_blocks)
    slots_pad = -(-(n_slots + 1) // (SC_UNIT // 2)) * (SC_UNIT // 2)
    dump = n_slots + _filler(pad, slots_pad - n_slots)
    dest_sc = jnp.concatenate([dest, jnp.broadcast_to(dump[:, None], (pad, TOP_K))], axis=0).T
    xs_sorted = scatter(hfp.reshape(2 * tok_pad, QUARTER),
                        jnp.concatenate([dest_sc, slots_pad + dest_sc]).reshape(-1), 2 * slots_pad)
    out_sorted = _moe(block_e, n_active, n_live, xs_sorted.reshape(2, slots_pad, QUARTER),
                      w_up_e, b_up_d, w_down_e, b_dn)
    fill = jnp.broadcast_to(_filler(pad, n_slots)[:, None], (pad, TOP_K))
    dest_g = jnp.concatenate([dest, fill], axis=0).T
    yg = gather(out_sorted.reshape(2 * n_slots, QUARTER),
                jnp.concatenate([dest_g, n_slots + dest_g]).reshape(-1))
    yg = yg.reshape(2, TOP_K, tok_pad, QUARTER)
    y_p = _combine(yg, tg_p, x1_p, g_fin, tm=tm, row_offset=0)
    y_s = _combine(yg, tg_s, x1_s, g_fin, tm=nb, row_offset=n_prompt)

    depth = 1
    return (y_p.reshape(batch, seq, D_MODEL),
            y_s.reshape(nb, 1, D_MODEL),
            jnp.transpose(kat.reshape(depth, batch, DA_HEADS, 2, DA_HD, seq), (0, 1, 5, 2, 3, 4)),
            va.reshape(depth, batch, seq, DA_HEADS, DA_VD),
            ret_p.reshape(depth, batch, RET_HEADS, RET_DK, RET_DV),
            ka_s.reshape(depth, nb, 1, DA_HEADS, 2, DA_HD),
            va_s.reshape(depth, nb, 1, DA_HEADS, DA_VD),
            ret_s.reshape(depth, nb, RET_HEADS, RET_DK, RET_DV))
```

```python
import functools
import math

import numpy as np
import jax
import jax.numpy as jnp
from jax import lax
from jax.experimental import pallas as pl
from jax.experimental.pallas import tpu as pltpu
from jax.experimental.pallas import tpu_sc as plsc

F32 = jnp.float32
BF16 = jnp.bfloat16
I32 = jnp.int32

D_MODEL = 1024
DA_HEADS = 4
DA_VD = 128
DA_HD = 64
DA_WIDTH = DA_HEADS * DA_VD
RET_HEADS = 4
RET_DV = 128
RET_DK = 64
RET_WIDTH = RET_HEADS * RET_DV
RET_QK_WIDTH = RET_HEADS * RET_DK
RET_CHUNK = 128
ROPE_BASE = 10000.0
N_EXPERTS = 32
TOP_K = 4
D_FF = 1024
SWIGLU_LIMIT = 7.0
SWIGLU_ALPHA = 1.702
NORM_EPS = 1e-5
NEG_INF = -1e30
LAM_INIT = 0.8 - 0.6 * math.exp(-0.3 * 0)
LOG2_E = math.log2(math.e)
LANES = 128
HALF = D_MODEL // 2
QUARTER = D_MODEL // 4
MOE_TM = 512
PREP_ROWS = 64
ROUTE_TILE = 512
SC_WINDOW = 128
SC_SUBCORES = 32
SC_UNIT = SC_WINDOW * SC_SUBCORES
VMEM_LIMIT = 56 * 1024 * 1024

_C_QA, _C_KA, _C_VA = 0, 512, 1024
_C_QR, _C_QRS, _C_KR, _C_KRS = 1536, 1792, 2048, 2304
_C_VR, _C_GR, _C_END = 2560, 3072, 3584


def _mm(a, b):
    return jnp.dot(a.astype(BF16), b.astype(BF16), preferred_element_type=F32)


def _bf(x):
    return x.astype(BF16).astype(F32)


def _rms(x):
    return x * lax.rsqrt(jnp.mean(x * x, axis=-1, keepdims=True) + NORM_EPS)


def _silu(g):
    return g / (1.0 + jnp.exp(-g))


def _pack_pairs(x):
    w = x.shape[-1] // 2
    lo = lax.bitcast_convert_type(x[:, :w].astype(BF16).astype(F32), jnp.uint32)
    hi = lax.bitcast_convert_type(x[:, w:].astype(BF16).astype(F32), jnp.uint32)
    return lax.bitcast_convert_type(hi | (lo >> 16), I32)


def _unpack_pairs(p):
    u = lax.bitcast_convert_type(p, jnp.uint32)
    lo = lax.bitcast_convert_type(u << 16, F32)
    hi = lax.bitcast_convert_type(u & jnp.uint32(0xFFFF0000), F32)
    return lo, hi


def _alibi_slopes():
    return np.asarray([2.0 ** (-8.0 * (h + 1) / DA_HEADS) for h in range(DA_HEADS)], np.float32)


def _inproj_kernel(x_ref, g_ref, w_ref, wt_ref, cos_ref, sin_ref, *outs, transposed):
    hm = (_rms(x_ref[...]) * g_ref[...]).astype(BF16)

    def mm(lo, hi):
        return _mm(hm, w_ref[:, lo:hi])

    def mm_t(lo, hi):
        return lax.dot_general(wt_ref[lo:hi, :], hm, (((1,), (1,)), ((), ())),
                               preferred_element_type=F32)

    if transposed:
        qat_ref, kat_ref, kab_ref, va_ref, vat_ref, qr_ref, kr_ref, vr_ref, gr_ref = outs
        qat_ref[0] = mm_t(_C_QA, _C_KA).astype(BF16)
        kat_ref[0] = mm_t(_C_KA, _C_VA)
        kab_ref[...] = mm(_C_KA, _C_VA).astype(BF16)
        va = mm(_C_VA, _C_QR)
        for h in range(DA_HEADS):
            va_ref[pl.ds(h, va.shape[0], stride=DA_HEADS), :] = va[:, h * DA_VD:(h + 1) * DA_VD]
        vat_ref[0] = mm_t(_C_VA, _C_QR).astype(BF16)
    else:
        qa_ref, ka_ref, va_ref, qr_ref, kr_ref, vr_ref, gr_ref = outs
        qa_ref[...] = mm(_C_QA, _C_KA)
        ka_ref[...] = mm(_C_KA, _C_VA)
        va_ref[...] = mm(_C_VA, _C_QR)
    cos = cos_ref[...]
    sin = sin_ref[...]
    qr_ref[...] = (mm(_C_QR, _C_QRS) * cos + mm(_C_QRS, _C_KR) * sin).astype(qr_ref.dtype)
    kr_ref[...] = (mm(_C_KR, _C_KRS) * cos + mm(_C_KRS, _C_VR) * sin).astype(kr_ref.dtype)
    vr_ref[...] = mm(_C_VR, _C_GR).astype(vr_ref.dtype)
    gr_ref[...] = mm(_C_GR, _C_END).astype(gr_ref.dtype)


def _inproj(x, g, w_ext, w_ext_t, cos_t, sin_t, *, tm, seq=None):
    n = x.shape[0]
    n_tab = cos_t.shape[0] // tm
    transposed = seq is not None
    act = BF16 if transposed else F32
    row = lambda w: pl.BlockSpec((tm, w), lambda i: (i, 0))
    tab = pl.BlockSpec((tm, RET_QK_WIDTH), lambda i: (i % n_tab, 0))
    sds = jax.ShapeDtypeStruct
    tail_shapes = (sds((n, RET_QK_WIDTH), act), sds((n, RET_QK_WIDTH), act),
                   sds((n, RET_WIDTH), act), sds((n, RET_WIDTH), act))
    tail_specs = (row(RET_QK_WIDTH), row(RET_QK_WIDTH), row(RET_WIDTH), row(RET_WIDTH))
    if transposed:
        n_s = seq // tm
        tr = pl.BlockSpec((1, DA_WIDTH, tm), lambda i: (i // n_s, 0, i % n_s))
        head_shapes = (sds((n // seq, DA_WIDTH, seq), BF16), sds((n // seq, DA_WIDTH, seq), F32),
                       sds((n, DA_WIDTH), BF16), sds((n * DA_HEADS, DA_VD), F32),
                       sds((n // seq, DA_WIDTH, seq), BF16))
        head_specs = (tr, tr, row(DA_WIDTH),
                      pl.BlockSpec((tm * DA_HEADS, DA_VD), lambda i: (i, 0)), tr)
    else:
        head_shapes = (sds((n, DA_WIDTH), F32),) * 3
        head_specs = (row(DA_WIDTH),) * 3
    return pl.pallas_call(
        functools.partial(_inproj_kernel, transposed=transposed),
        grid=(n // tm,),
        in_specs=[row(D_MODEL),
                  pl.BlockSpec((1, D_MODEL), lambda i: (0, 0)),
                  pl.BlockSpec((D_MODEL, _C_END), lambda i: (0, 0)),
                  pl.BlockSpec((_C_QR, D_MODEL), lambda i: (0, 0)),
                  tab, tab],
        out_specs=head_specs + tail_specs,
        out_shape=head_shapes + tail_shapes,
        compiler_params=pltpu.CompilerParams(dimension_semantics=("parallel",),
                                             vmem_limit_bytes=VMEM_LIMIT),
        name="inproj",
    )(x, g, w_ext, w_ext_t, cos_t, sin_t)


def _lambda_value(lamv_ref):
    lv = lamv_ref[...]
    s1 = jnp.sum(lv[0:1] * lv[1:2], axis=-1, keepdims=True)
    s2 = jnp.sum(lv[2:3] * lv[3:4], axis=-1, keepdims=True)
    return jnp.exp(s1) - jnp.exp(s2) + LAM_INIT


def _attn_prompt_kernel(qi_tab, ki_tab, qt_ref, k_ref, vt_ref, rel_ref, relm_ref, off_ref,
                        lamv_ref, subg_ref, o_ref, m_sc, l_sc, acc_sc):
    hd = pl.program_id(1)
    t = pl.program_id(2)
    qi = qi_tab[t]
    ki = ki_tab[t]

    @pl.when(ki == 0)
    def _():
        m_sc[...] = jnp.full(m_sc.shape, NEG_INF, F32)
        l_sc[...] = jnp.zeros(l_sc.shape, F32)
        acc_sc[...] = jnp.zeros(acc_sc.shape, F32)

    off = off_ref[hd, t]

    def update(rel):
        qt = qt_ref[0]
        k = k_ref[...]
        vt = vt_ref[0]
        row = lax.broadcasted_iota(I32, qt.shape, 0)
        qcs = [jnp.where((row >= c * DA_HD) & (row < (c + 1) * DA_HD), qt, jnp.zeros_like(qt))
               for c in range(2)]
        tts = [jnp.dot(k, qc, preferred_element_type=F32) + rel for qc in qcs]
        m_olds = [m_sc[c] for c in range(2)]
        m_news = [jnp.maximum(m_olds[c], jnp.max(tts[c], axis=0, keepdims=True) + off)
                  for c in range(2)]
        ps = [jnp.exp2(tts[c] + (off - m_news[c])) for c in range(2)]
        for c in range(2):
            alpha = jnp.exp2(m_olds[c] - m_news[c])
            l_sc[c] = alpha * l_sc[c] + jnp.sum(ps[c], axis=0, keepdims=True)
            acc_sc[c] = alpha * acc_sc[c] + jnp.dot(vt, ps[c].astype(BF16),
                                                    preferred_element_type=F32)
            m_sc[c] = m_news[c]

    @pl.when(ki < qi)
    def _():
        update(rel_ref[0])

    @pl.when(ki == qi)
    def _():
        update(relm_ref[0])
        lam = _lambda_value(lamv_ref)
        o = acc_sc[0] / l_sc[0] - lam * (acc_sc[1] / l_sc[1])
        ms = jnp.mean(o * o, axis=0, keepdims=True)
        a = o * lax.rsqrt(ms + NORM_EPS) * subg_ref[...] * (1.0 - LAM_INIT)
        o_ref[0] = a.astype(o_ref.dtype)


def _attn_prompt(qat, kab, vat, lamv, subg, *, batch, seq, tq):
    nq = seq // tq
    steps = [(i, j) for i in range(nq) for j in range(i + 1)]
    qi_tab = jnp.asarray([s[0] for s in steps], I32)
    ki_tab = jnp.asarray([s[1] for s in steps], I32)
    slopes = (_alibi_slopes().astype(np.float64) * LOG2_E).astype(np.float32)
    dist = (np.arange(tq)[None, :] - np.arange(tq)[:, None]).astype(np.float32)
    rel = -slopes[:, None, None] * dist[None]
    relm = np.where(dist[None] >= 0, rel, np.float32(NEG_INF)).astype(np.float32)
    blk = np.asarray([(s[0] - s[1]) * tq for s in steps], np.float32)
    off = -slopes[:, None] * blk[None, :]
    qspec = pl.BlockSpec((1, DA_VD, tq), lambda b, h, t, qt, kt: (b, h, qt[t]))
    kspec = pl.BlockSpec((tq, DA_VD), lambda b, h, t, qt, kt: (b * nq + kt[t], h))
    vspec = pl.BlockSpec((1, DA_VD, tq), lambda b, h, t, qt, kt: (b, h, kt[t]))
    relspec = pl.BlockSpec((1, tq, tq), lambda b, h, t, qt, kt: (h, 0, 0))
    grid_spec = pltpu.PrefetchScalarGridSpec(
        num_scalar_prefetch=2,
        grid=(batch, DA_HEADS, len(steps)),
        in_specs=[qspec, kspec, vspec, relspec, relspec,
                  pl.BlockSpec(memory_space=pltpu.SMEM),
                  pl.BlockSpec((4, DA_HD), lambda b, h, t, qt, kt: (0, 0)),
                  pl.BlockSpec((DA_VD, 1), lambda b, h, t, qt, kt: (0, 0))],
        out_specs=qspec,
        scratch_shapes=[pltpu.VMEM((2, 1, tq), F32), pltpu.VMEM((2, 1, tq), F32),
                        pltpu.VMEM((2, DA_VD, tq), F32)],
    )
    return pl.pallas_call(
        _attn_prompt_kernel,
        grid_spec=grid_spec,
        out_shape=jax.ShapeDtypeStruct((batch, DA_WIDTH, seq), BF16),
        compiler_params=pltpu.CompilerParams(
            dimension_semantics=("parallel", "parallel", "arbitrary"),
            vmem_limit_bytes=VMEM_LIMIT),
        name="attn_prompt",
    )(qi_tab, ki_tab, qat, kab, vat, jnp.asarray(rel), jnp.asarray(relm), jnp.asarray(off),
      lamv, subg.reshape(DA_VD, 1))


def _ret_prompt_kernel(q_ref, k_ref, v_ref, g_ref, dec_ref, qdec_ref, kdec_ref, sdec_ref,
                       r_ref, s_ref, *, batch):
    c = pl.program_id(0)

    @pl.when(c == 0)
    def _():
        s_ref[...] = jnp.zeros(s_ref.shape, F32)

    qdec = qdec_ref[...]
    kdec = kdec_ref[...]
    chains = [(b, h) for b in range(batch) for h in range(RET_HEADS)]
    qds, kds = {}, {}
    for b in range(batch):
        qds[b] = (q_ref[b].astype(F32) * qdec).astype(BF16)
        kds[b] = (k_ref[b].astype(F32) * kdec).astype(BF16)
    ks = lambda h: slice(h * RET_DK, (h + 1) * RET_DK)
    vs = lambda h: slice(h * RET_DV, (h + 1) * RET_DV)
    qks = {(b, h): lax.dot_general(q_ref[b][:, ks(h)], k_ref[b][:, ks(h)], (((1,), (1,)), ((), ())),
                                   preferred_element_type=F32) * dec_ref[h] for b, h in chains}
    s_olds = {(b, h): s_ref[b, h] for b, h in chains}
    os_ = {(b, h): jnp.dot(qks[b, h].astype(BF16), v_ref[b, :, vs(h)], preferred_element_type=F32)
                   + jnp.dot(qds[b][:, ks(h)], s_olds[b, h].astype(BF16), preferred_element_type=F32)
           for b, h in chains}
    for b, h in chains:
        s_ref[b, h] = sdec_ref[h] * s_olds[b, h] + lax.dot_general(
            kds[b][:, ks(h)], v_ref[b, :, vs(h)], (((0,), (0,)), ((), ())), preferred_element_type=F32)
    for b, h in chains:
        gate = _silu(g_ref[b, :, vs(h)].astype(F32))
        r_ref[b, :, vs(h)] = (_rms(os_[b, h]) * gate).astype(r_ref.dtype)


def _ret_tables(length):
    log_g = jnp.log(1.0 - 2.0 ** (-5.0 - jnp.arange(RET_HEADS, dtype=F32)))
    idx = jnp.arange(length, dtype=F32)
    diff = idx[:, None] - idx[None, :]
    dec = jnp.where(diff >= 0, jnp.exp(jnp.maximum(diff, 0.0)[None] * log_g[:, None, None]), 0.0)
    qdec = jnp.exp((idx + 1.0)[:, None] * log_g[None, :])
    kdec = jnp.exp((length - 1.0 - idx)[:, None] * log_g[None, :])
    sdec = jnp.exp(length * log_g)
    rep = lambda a: jnp.repeat(a, RET_DK, axis=1)
    sdec_b = jnp.broadcast_to(sdec[:, None, None], (RET_HEADS, RET_DK, RET_DV))
    return dec, rep(qdec), rep(kdec), sdec_b


def _ret_prompt(qr, kr, vr, gr, *, batch, seq):
    nc = seq // RET_CHUNK
    dec, qdec, kdec, sdec = _ret_tables(RET_CHUNK)
    q3 = qr.reshape(batch, seq, RET_QK_WIDTH)
    k3 = kr.reshape(batch, seq, RET_QK_WIDTH)
    v3 = vr.reshape(batch, seq, RET_WIDTH)
    g3 = gr.reshape(batch, seq, RET_WIDTH)
    blk = lambda w: pl.BlockSpec((batch, RET_CHUNK, w), lambda c: (0, c, 0))
    const = lambda shape: pl.BlockSpec(shape, lambda c: (0,) * len(shape))
    r, s = pl.pallas_call(
        functools.partial(_ret_prompt_kernel, batch=batch),
        grid=(nc,),
        in_specs=[blk(RET_QK_WIDTH), blk(RET_QK_WIDTH), blk(RET_WIDTH), blk(RET_WIDTH),
                  const((RET_HEADS, RET_CHUNK, RET_CHUNK)),
                  const((RET_CHUNK, RET_QK_WIDTH)), const((RET_CHUNK, RET_QK_WIDTH)),
                  const((RET_HEADS, RET_DK, RET_DV))],
        out_specs=(blk(RET_WIDTH), const((batch, RET_HEADS, RET_DK, RET_DV))),
        out_shape=(jax.ShapeDtypeStruct((batch, seq, RET_WIDTH), BF16),
                   jax.ShapeDtypeStruct((batch, RET_HEADS, RET_DK, RET_DV), F32)),
        compiler_params=pltpu.CompilerParams(dimension_semantics=("arbitrary",),
                                             vmem_limit_bytes=VMEM_LIMIT),
        name="ret_prompt",
    )(q3, k3, v3, g3, dec, qdec, kdec, sdec)
    return r.reshape(batch * seq, RET_WIDTH), s


def _attn_sample_kernel(pt_ref, q_ref, kn_ref, vn_ref, rel_ref, slope_ref, lamv_ref, subg_ref,
                        *rest, pages_per_step, past_len, n_steps):
    g_pages = pages_per_step
    k_refs = rest[:g_pages]
    v_refs = rest[g_pages:2 * g_pages]
    o_ref = rest[2 * g_pages]
    s_sc, wnew_sc, acc_sc = rest[2 * g_pages + 1:]
    j = pl.program_id(1)
    rows = 2 * DA_HEADS

    def q_rows():
        col_group = lax.broadcasted_iota(I32, (rows, DA_WIDTH), 1) // DA_HD
        row_id = lax.broadcasted_iota(I32, (rows, DA_WIDTH), 0)
        want = 2 * (row_id % DA_HEADS) + row_id // DA_HEADS
        qb = jnp.broadcast_to(q_ref[0], (rows, DA_WIDTH))
        return jnp.where(col_group == want, qb, 0.0)

    @pl.when(j < n_steps)
    def _():
        qr = q_rows().astype(BF16)
        for g in range(g_pages):
            page = j * g_pages + g
            s = jnp.dot(qr, k_refs[g][0].astype(BF16), preferred_element_type=F32)
            base = (page * LANES - past_len).astype(F32)
            s_sc[page] = s + (rel_ref[...] + slope_ref[...] * base)

    @pl.when(j == n_steps - 1)
    def _():
        s_all = s_sc[...]
        s_new = jnp.sum(q_rows() * kn_ref[0], axis=-1, keepdims=True)
        m = jnp.maximum(jnp.max(jnp.max(s_all, axis=0), axis=-1, keepdims=True), s_new)
        p = jnp.exp(s_all - m)
        p_new = jnp.exp(s_new - m)
        l = jnp.sum(jnp.sum(p, axis=0), axis=-1, keepdims=True) + p_new
        wn = p / l
        wn_new = p_new / l
        lam = _lambda_value(lamv_ref)
        s_sc[:, 0:DA_HEADS, :] = wn[:, 0:DA_HEADS, :] - lam * wn[:, DA_HEADS:rows, :]
        s_sc[:, DA_HEADS:rows, :] = jnp.zeros((s_sc.shape[0], DA_HEADS, LANES), F32)
        wnew_sc[...] = wn_new[0:DA_HEADS] - lam * wn_new[DA_HEADS:rows]
        acc_sc[...] = jnp.zeros(acc_sc.shape, F32)

    @pl.when(j >= n_steps)
    def _():
        lane = lax.broadcasted_iota(I32, (rows, LANES), 1)
        own = lane % DA_HEADS == lax.broadcasted_iota(I32, (rows, LANES), 0)
        acc = acc_sc[...]
        for g in range(g_pages):
            page = (j - n_steps) * g_pages + g
            w = s_sc[page]
            w_rows = jnp.concatenate(
                [jnp.where(own, jnp.take_along_axis(w, (LANES // DA_HEADS) * c + lane // DA_HEADS,
                                                    axis=1), 0.0)
                 for c in range(DA_HEADS)], axis=-1)
            acc = acc + jnp.dot(w_rows.astype(BF16), v_refs[g][0].astype(BF16),
                                preferred_element_type=F32)
        acc_sc[...] = acc

    @pl.when(j == 2 * n_steps - 1)
    def _():
        acc = acc_sc[...]
        w_new = wnew_sc[...]
        v_new = vn_ref[0]
        outs = []
        for h in range(DA_HEADS):
            vs = slice(h * DA_VD, (h + 1) * DA_VD)
            o = acc[h:h + 1, :] + w_new[h:h + 1, :] * v_new[:, vs]
            outs.append(_rms(o) * subg_ref[...] * (1.0 - LAM_INIT))
        o_ref[0] = jnp.concatenate(outs, axis=-1)


def _attn_sample(q, k_new, v_new, cache_kt, cache_vr, page_table, lamv, subg, *, pages_per_step):
    nb, n_pages = page_table.shape
    page = cache_kt.shape[2]
    assert page == LANES
    past_len = n_pages * page
    g_pages = pages_per_step
    n_steps = n_pages // g_pages
    rows = 2 * DA_HEADS
    slope_rows = np.tile(_alibi_slopes(), 2)[:, None]
    rel = slope_rows * np.arange(page, dtype=np.float32)[None, :]
    pt = page_table.reshape(-1).astype(I32)
    row3 = lambda: pl.BlockSpec((1, 1, DA_WIDTH), lambda b, j, pt: (b, 0, 0))
    const2 = lambda shape: pl.BlockSpec(shape, lambda b, j, pt: (0, 0))

    def k_spec(g):
        return pl.BlockSpec(
            (1, DA_WIDTH, page),
            lambda b, j, pt: (pt[b * n_pages + jnp.minimum(j, n_steps - 1) * g_pages + g], 0, 0))

    def v_spec(g):
        return pl.BlockSpec(
            (1, page * DA_HEADS, DA_VD),
            lambda b, j, pt: (pt[b * n_pages + jnp.maximum(j - n_steps, 0) * g_pages + g], 0, 0))

    grid_spec = pltpu.PrefetchScalarGridSpec(
        num_scalar_prefetch=1,
        grid=(nb, 2 * n_steps),
        in_specs=[row3(), row3(), row3(), const2((rows, page)), const2((rows, 1)),
                  const2((4, DA_HD)), const2((1, DA_VD))]
                 + [k_spec(g) for g in range(g_pages)] + [v_spec(g) for g in range(g_pages)],
        out_specs=row3(),
        scratch_shapes=[pltpu.VMEM((n_pages, rows, page), F32), pltpu.VMEM((DA_HEADS, 1), F32),
                        pltpu.VMEM((rows, DA_VD), F32)],
    )
    out = pl.pallas_call(
        functools.partial(_attn_sample_kernel, pages_per_step=g_pages, past_len=past_len,
                          n_steps=n_steps),
        grid_spec=grid_spec,
        out_shape=jax.ShapeDtypeStruct((nb, 1, DA_WIDTH), F32),
        compiler_params=pltpu.CompilerParams(dimension_semantics=("parallel", "arbitrary"),
                                             vmem_limit_bytes=VMEM_LIMIT),
        name="attn_sample",
    )(pt, q.reshape(nb, 1, DA_WIDTH), k_new.reshape(nb, 1, DA_WIDTH), v_new.reshape(nb, 1, DA_WIDTH),
      jnp.asarray(rel), jnp.asarray(slope_rows), lamv, subg,
      *([cache_kt] * g_pages), *([cache_vr] * g_pages))
    return out.reshape(nb, DA_WIDTH)


def _ret_sample_kernel(q_ref, k_ref, qc_ref, kc_ref, v_ref, g_ref, s_ref, qdec_ref, sdec_ref,
                       r_ref, so_ref):
    q = q_ref[0]
    k = k_ref[0]
    qd_col = _bf(qc_ref[0] * qdec_ref[...])
    k_col = kc_ref[0]
    outs = []
    for h in range(RET_HEADS):
        ks = slice(h * RET_DK, (h + 1) * RET_DK)
        vs = slice(h * RET_DV, (h + 1) * RET_DV)
        vh = v_ref[0, :, vs]
        s_old = s_ref[0, h]
        qk = jnp.sum(q[:, ks] * k[:, ks], axis=-1, keepdims=True)
        o = qk * vh + jnp.sum(qd_col[ks, :] * _bf(s_old), axis=0, keepdims=True)
        so_ref[0, h] = sdec_ref[h] * s_old + k_col[ks, :] * vh
        outs.append(_rms(o) * _silu(g_ref[0, :, vs]))
    r_ref[0] = jnp.concatenate(outs, axis=-1)


def _ret_sample(qr, kr, vr, gr, state):
    nb = qr.shape[0]
    _, qdec, _, sdec = _ret_tables(1)
    row3 = lambda w: pl.BlockSpec((1, 1, w), lambda b: (b, 0, 0))
    col3 = pl.BlockSpec((1, RET_QK_WIDTH, 1), lambda b: (b, 0, 0))
    st = pl.BlockSpec((1, RET_HEADS, RET_DK, RET_DV), lambda b: (b, 0, 0, 0))
    r, s_new = pl.pallas_call(
        _ret_sample_kernel,
        grid=(nb,),
        in_specs=[row3(RET_QK_WIDTH), row3(RET_QK_WIDTH), col3, col3, row3(RET_WIDTH),
                  row3(RET_WIDTH), st,
                  pl.BlockSpec((RET_QK_WIDTH, 1), lambda b: (0, 0)),
                  pl.BlockSpec((RET_HEADS, RET_DK, RET_DV), lambda b: (0, 0, 0))],
        out_specs=(row3(RET_WIDTH), st),
        out_shape=(jax.ShapeDtypeStruct((nb, 1, RET_WIDTH), F32),
                   jax.ShapeDtypeStruct((nb, RET_HEADS, RET_DK, RET_DV), F32)),
        compiler_params=pltpu.CompilerParams(dimension_semantics=("parallel",)),
        name="ret_sample",
    )(qr.reshape(nb, 1, -1), kr.reshape(nb, 1, -1), qr.reshape(nb, -1, 1), kr.reshape(nb, -1, 1),
      vr.reshape(nb, 1, -1), gr.reshape(nb, 1, -1), state, qdec.reshape(-1, 1), sdec)
    return r.reshape(nb, RET_WIDTH), s_new


def _outproj_kernel(a_ref, r_ref, x_ref, wo_ref, g_ref, wr_ref, br_ref,
                    x1_ref, hfp_ref, ti_ref, tg_ref, *, a_transposed):
    if a_transposed:
        mix_a = lax.dot_general(a_ref[0], wo_ref[:DA_WIDTH, :], (((0,), (0,)), ((), ())),
                                preferred_element_type=F32)
    else:
        mix_a = _mm(a_ref[...], wo_ref[:DA_WIDTH, :])
    mix = mix_a + _mm(r_ref[...], wo_ref[DA_WIDTH:, :])
    x1 = x_ref[...] + mix
    x1_ref[...] = x1
    hf = _rms(x1) * g_ref[...]
    hfp_ref[0] = _pack_pairs(hf[:, :HALF])
    hfp_ref[1] = _pack_pairs(hf[:, HALF:])
    logits = _mm(hf, wr_ref[...]) + br_ref[...]
    lane = lax.broadcasted_iota(I32, logits.shape, 1)
    work = logits
    top_v, top_i = [], []
    for _ in range(TOP_K):
        m = jnp.max(work, axis=-1, keepdims=True)
        idx = jnp.min(jnp.where(work == m, lane, LANES), axis=-1, keepdims=True)
        top_v.append(m)
        top_i.append(idx)
        work = jnp.where(lane == idx, -jnp.inf, work)
    es = [jnp.exp(v - top_v[0]) for v in top_v]
    denom = es[0] + es[1] + es[2] + es[3]
    ti = jnp.zeros(logits.shape, I32)
    tg = jnp.zeros(logits.shape, F32)
    for kk in range(TOP_K):
        ti = jnp.where(lane == kk, top_i[kk], ti)
        tg = jnp.where(lane == kk, es[kk] / denom, tg)
    ti_ref[...] = ti
    tg_ref[...] = tg


def _outproj(a, r, x, w_out, g, w_router, b_router, *, tm):
    n = x.shape[0]
    row = lambda w: pl.BlockSpec((tm, w), lambda i: (i, 0))
    const = lambda shape: pl.BlockSpec(shape, lambda i: (0, 0))
    a_transposed = a.ndim == 3
    if a_transposed:
        n_s = a.shape[2] // tm
        a_spec = pl.BlockSpec((1, DA_WIDTH, tm), lambda i: (i // n_s, 0, i % n_s))
    else:
        a_spec = row(DA_WIDTH)
    return pl.pallas_call(
        functools.partial(_outproj_kernel, a_transposed=a_transposed),
        grid=(n // tm,),
        in_specs=[a_spec, row(RET_WIDTH), row(D_MODEL), const((D_MODEL, D_MODEL)),
                  const((1, D_MODEL)), const((D_MODEL, LANES)), const((1, LANES))],
        out_specs=(row(D_MODEL), pl.BlockSpec((2, tm, QUARTER), lambda i: (0, i, 0)),
                   row(LANES), row(LANES)),
        out_shape=(jax.ShapeDtypeStruct((n, D_MODEL), F32),
                   jax.ShapeDtypeStruct((2, n, QUARTER), I32),
                   jax.ShapeDtypeStruct((n, LANES), I32),
                   jax.ShapeDtypeStruct((n, LANES), F32)),
        compiler_params=pltpu.CompilerParams(dimension_semantics=("parallel",),
                                             vmem_limit_bytes=VMEM_LIMIT),
        name="outproj",
    )(a, r, x, w_out, g, w_router, b_router)


def _sc_gather(x, idx):
    m = idx.shape[0]
    cols = x.shape[1]
    mesh = plsc.VectorSubcoreMesh(core_axis_name="c", subcore_axis_name="s")

    @pl.kernel(out_type=jax.ShapeDtypeStruct((m, cols), x.dtype), mesh=mesh)
    def gather_kernel(x_hbm, i_hbm, o_hbm):
        def body(i_vmem, o_vmem):
            pltpu.sync_copy(x_hbm.at[i_vmem.at[0]], o_vmem)

        pltpu.emit_pipeline(
            body,
            grid=(m // SC_WINDOW,),
            in_specs=[pl.BlockSpec((1, SC_WINDOW), lambda i: (0, i))],
            out_specs=[pl.BlockSpec((SC_WINDOW, cols), lambda i: (i, 0))],
            core_axis_name=("c", "s"),
            dimension_semantics=(pltpu.PARALLEL,),
        )(i_hbm, o_hbm)

    return gather_kernel(x, idx.reshape(1, m))


def _filler(n, modulus):
    return jnp.arange(n, dtype=I32) % modulus


def _sc_scatter(x, idx, out_rows):
    m = idx.shape[0]
    cols = x.shape[1]
    n_blk = x.shape[0] // 2 // SC_WINDOW
    mesh = plsc.VectorSubcoreMesh(core_axis_name="c", subcore_axis_name="s")

    @pl.kernel(out_type=jax.ShapeDtypeStruct((out_rows, cols), x.dtype), mesh=mesh)
    def scatter_kernel(x_hbm, i_hbm, o_hbm):
        def body(x_vmem, i_vmem):
            pltpu.sync_copy(x_vmem, o_hbm.at[i_vmem.at[0]])

        pltpu.emit_pipeline(
            body,
            grid=(m // SC_WINDOW,),
            in_specs=[pl.BlockSpec((SC_WINDOW, cols),
                                   lambda i: ((i // (TOP_K * n_blk)) * n_blk + i % n_blk, 0)),
                      pl.BlockSpec((1, SC_WINDOW), lambda i: (0, i))],
            out_specs=[],
            core_axis_name=("c", "s"),
            dimension_semantics=(pltpu.PARALLEL,),
        )(x_hbm, i_hbm)

    return scatter_kernel(x, idx.reshape(1, m))


def _rank_kernel(ti_ref, tri_ref, rank_ref, cnt_ref, base_sc):
    @pl.when(pl.program_id(0) == 0)
    def _():
        base_sc[...] = jnp.zeros(base_sc.shape, F32)

    ti = ti_ref[...]
    lane = lax.broadcasted_iota(I32, ti.shape, 1)
    picks = [lane == ti[:, k:k + 1] for k in range(TOP_K)]
    onehot = jnp.zeros(ti.shape, F32)
    for pk in picks:
        onehot = onehot + jnp.where(pk, 1.0, 0.0)
    pos = jnp.dot(tri_ref[...], onehot.astype(BF16), preferred_element_type=F32) + base_sc[...]
    rank = jnp.zeros(ti.shape, I32)
    for k, pk in enumerate(picks):
        r_k = jnp.sum(jnp.where(pk, pos, 0.0), axis=-1, keepdims=True)
        rank = jnp.where(lane == k, r_k.astype(I32), rank)
    rank_ref[...] = rank
    base_sc[...] = base_sc[...] + jnp.sum(onehot, axis=0, keepdims=True)
    cnt_ref[...] = base_sc[...]


def _rank(ti, *, tr):
    n = ti.shape[0]
    tri = jnp.asarray(np.tril(np.ones((tr, tr), np.float32), -1), BF16)
    return pl.pallas_call(
        _rank_kernel,
        grid=(n // tr,),
        in_specs=[pl.BlockSpec((tr, LANES), lambda i: (i, 0)),
                  pl.BlockSpec((tr, tr), lambda i: (0, 0))],
        out_specs=(pl.BlockSpec((tr, LANES), lambda i: (i, 0)),
                   pl.BlockSpec((1, LANES), lambda i: (0, 0))),
        out_shape=(jax.ShapeDtypeStruct((n, LANES), I32), jax.ShapeDtypeStruct((1, LANES), F32)),
        scratch_shapes=[pltpu.VMEM((1, LANES), F32)],
        compiler_params=pltpu.CompilerParams(dimension_semantics=("arbitrary",)),
        name="route_rank",
    )(ti, tri)


def _prep_expert_weights(wup_ref, wdn_ref, wupb, wdnb):
    lane = lax.broadcasted_iota(I32, (PREP_ROWS, LANES), 1)
    first_half = lane < LANES // 2
    idx_even = (2 * lane) % LANES
    idx_odd = (2 * lane + 1) % LANES

    def body(r, carry):
        rows = pl.ds(pl.multiple_of(r * PREP_ROWS, PREP_ROWS), PREP_ROWS)
        for ct in range(D_FF // LANES):
            a = wup_ref[0, rows, 2 * ct * LANES:(2 * ct + 1) * LANES]
            b = wup_ref[0, rows, (2 * ct + 1) * LANES:(2 * ct + 2) * LANES]
            even = jnp.where(first_half, jnp.take_along_axis(a, idx_even, axis=1),
                             jnp.take_along_axis(b, idx_even, axis=1))
            odd = jnp.where(first_half, jnp.take_along_axis(a, idx_odd, axis=1),
                            jnp.take_along_axis(b, idx_odd, axis=1))
            wupb[rows, ct * LANES:(ct + 1) * LANES] = even.astype(BF16)
            wupb[rows, D_FF + ct * LANES:D_FF + (ct + 1) * LANES] = odd.astype(BF16)
        wdnb[rows, :] = wdn_ref[0, rows, :].astype(BF16)
        return carry

    lax.fori_loop(0, D_MODEL // PREP_ROWS, body, 0)


def _moe_kernel(be_ref, na_ref, nv_ref, xs_ref, wup_ref, bup_ref, wdn_ref, bdn_ref, out_ref,
                wupb, wdnb):
    i = pl.program_id(0)
    n_act = na_ref[0]
    e = be_ref[jnp.minimum(i, n_act - 1)]
    e_prev = be_ref[jnp.maximum(i - 1, 0)]

    @pl.when(jnp.logical_and(i < n_act, jnp.logical_or(i == 0, e != e_prev)))
    def _():
        _prep_expert_weights(wup_ref, wdn_ref, wupb, wdnb)

    @pl.when(i < n_act)
    def _():
        lo0, hi0 = _unpack_pairs(xs_ref[0])
        lo1, hi1 = _unpack_pairs(xs_ref[1])
        x = jnp.concatenate([lo0, hi0, lo1, hi1], axis=-1)
        live = lax.broadcasted_iota(I32, x.shape, 0) < nv_ref[i]
        x = jnp.where(live, x, 0.0).astype(BF16)
        u = jnp.dot(x, wupb[...], preferred_element_type=F32) + bup_ref[0]
        g = jnp.minimum(u[:, :D_FF], SWIGLU_LIMIT)
        lin = jnp.clip(u[:, D_FF:], -SWIGLU_LIMIT, SWIGLU_LIMIT)
        y = g * (1.0 / (1.0 + jnp.exp(-SWIGLU_ALPHA * g))) * (lin + 1.0)
        o = jnp.dot(y.astype(BF16), wdnb[...], preferred_element_type=F32) + bdn_ref[0]
        out_ref[0] = _pack_pairs(o[:, :HALF])
        out_ref[1] = _pack_pairs(o[:, HALF:])

    @pl.when(i >= n_act)
    def _():
        out_ref[...] = jnp.zeros(out_ref.shape, I32)


def _moe(block_e, n_active, n_valid, xs, w_up, b_up_d, w_down, b_down):
    n_blocks = block_e.shape[0]
    n_slots = n_blocks * MOE_TM

    def act(i, na):
        return jnp.minimum(i, na[0] - 1)

    grid_spec = pltpu.PrefetchScalarGridSpec(
        num_scalar_prefetch=3,
        grid=(n_blocks,),
        in_specs=[
            pl.BlockSpec((2, MOE_TM, QUARTER), lambda i, be, na, nv: (0, act(i, na), 0)),
            pl.BlockSpec((1, D_MODEL, 2 * D_FF), lambda i, be, na, nv: (be[act(i, na)], 0, 0)),
            pl.BlockSpec((1, 1, 2 * D_FF), lambda i, be, na, nv: (be[act(i, na)], 0, 0)),
            pl.BlockSpec((1, D_FF, D_MODEL), lambda i, be, na, nv: (be[act(i, na)], 0, 0)),
            pl.BlockSpec((1, 1, D_MODEL), lambda i, be, na, nv: (be[act(i, na)], 0, 0)),
        ],
        out_specs=pl.BlockSpec((2, MOE_TM, QUARTER), lambda i, be, na, nv: (0, i, 0)),
        scratch_shapes=[pltpu.VMEM((D_MODEL, 2 * D_FF), BF16), pltpu.VMEM((D_FF, D_MODEL), BF16)],
    )
    return pl.pallas_call(
        _moe_kernel,
        grid_spec=grid_spec,
        out_shape=jax.ShapeDtypeStruct((2, n_slots, QUARTER), I32),
        compiler_params=pltpu.CompilerParams(dimension_semantics=("arbitrary",),
                                             vmem_limit_bytes=VMEM_LIMIT),
        name="moe_ffn",
    )(block_e, n_active, n_valid, xs, w_up, b_up_d, w_down, b_down)


def _slot_layout(counts, top_i, rank, n_blocks):
    experts = jnp.arange(N_EXPERTS, dtype=I32)
    padded = (counts + MOE_TM - 1) // MOE_TM * MOE_TM
    pad_end = jnp.cumsum(padded)
    pad_start = pad_end - padded
    pick = top_i[:, :, None] == experts[None, None, :]
    dest = rank + jnp.sum(jnp.where(pick, pad_start[None, None, :], 0), axis=-1)
    blk_start = jnp.arange(n_blocks, dtype=I32) * MOE_TM
    block_e = jnp.minimum(jnp.sum((blk_start[:, None] >= pad_end[None, :]).astype(I32), axis=1),
                          N_EXPERTS - 1)
    mine = block_e[:, None] == experts[None, :]
    count_b = jnp.sum(jnp.where(mine, counts[None, :], 0), axis=1)
    start_b = jnp.sum(jnp.where(mine, pad_start[None, :], 0), axis=1)
    live = jnp.clip(count_b - (blk_start - start_b), 0, MOE_TM)
    n_active = (jnp.sum(padded) // MOE_TM).astype(I32).reshape(1)
    return dest.astype(I32), block_e.astype(I32), live.astype(I32), n_active


def _combine_kernel(yg_ref, tg_ref, x1_ref, g_ref, o_ref):
    tg = tg_ref[...]
    parts = [jnp.zeros((x1_ref.shape[0], QUARTER), F32) for _ in range(4)]
    for kk in range(TOP_K):
        gate = tg[:, kk:kk + 1]
        lo0, hi0 = _unpack_pairs(yg_ref[0, kk])
        lo1, hi1 = _unpack_pairs(yg_ref[1, kk])
        for j, piece in enumerate((lo0, hi0, lo1, hi1)):
            parts[j] = parts[j] + gate * piece
    x2 = x1_ref[...] + jnp.concatenate(parts, axis=-1)
    o_ref[...] = _rms(x2) * g_ref[...]


def _combine(yg, tg, x1, g, *, tm, row_offset):
    n = x1.shape[0]
    blk0 = row_offset // tm
    return pl.pallas_call(
        _combine_kernel,
        grid=(n // tm,),
        in_specs=[pl.BlockSpec((2, TOP_K, tm, QUARTER), lambda i: (0, 0, i + blk0, 0)),
                  pl.BlockSpec((tm, LANES), lambda i: (i, 0)),
                  pl.BlockSpec((tm, D_MODEL), lambda i: (i, 0)),
                  pl.BlockSpec((1, D_MODEL), lambda i: (0, 0))],
        out_specs=pl.BlockSpec((tm, D_MODEL), lambda i: (i, 0)),
        out_shape=jax.ShapeDtypeStruct((n, D_MODEL), F32),
        compiler_params=pltpu.CompilerParams(dimension_semantics=("parallel",),
                                             vmem_limit_bytes=VMEM_LIMIT),
        name="combine",
    )(yg, tg, x1, g)


def _extend_w_in(w):
    swap = np.arange(RET_QK_WIDTH).reshape(RET_HEADS, 2, RET_DK // 2)[:, ::-1, :].reshape(-1)
    qa, ka, va = w[:, 0:512], w[:, 512:1024], w[:, 1024:1536]
    qr, kr = w[:, 1536:1792], w[:, 1792:2048]
    vr, gr = w[:, 2048:2560], w[:, 2560:3072]
    sa = DA_HD ** -0.5
    sr = RET_DK ** -0.5
    return jnp.concatenate([qa * sa, ka, va, qr, qr[:, swap], kr * sr, kr[:, swap] * sr, vr, gr], axis=1)


def _rotary_tables(pos):
    half = RET_DK // 2
    inv = ROPE_BASE ** (-jnp.arange(half, dtype=F32) / half)
    ang = pos.astype(F32)[:, None] * inv[None, :]
    cos = jnp.cos(ang)
    sin = jnp.sin(ang)
    cos_t = jnp.tile(jnp.concatenate([cos, cos], axis=1), (1, RET_HEADS))
    sin_t = jnp.tile(jnp.concatenate([-sin, sin], axis=1), (1, RET_HEADS))
    return cos_t, sin_t


def kernel(x_prompt, x_sample, cache_k, cache_v, state_ret, page_table, norm_attn_g, w_in, lam_q1,
           lam_k1, lam_q2, lam_k2, da_subln_g, w_out, norm_ffn_g, w_router, b_router, w_up, b_up,
           w_down, b_down, norm_final_g):
    return _forward(x_prompt, x_sample, cache_k, cache_v, state_ret, page_table, norm_attn_g, w_in,
                    lam_q1, lam_k1, lam_q2, lam_k2, da_subln_g, w_out, norm_ffn_g, w_router,
                    b_router, w_up, b_up, w_down, b_down, norm_final_g,
                    tm=512, tq=512, pages_per_step=32, gather=_sc_gather, scatter=_sc_scatter)


def _forward(x_prompt, x_sample, cache_k, cache_v, state_ret, page_table, norm_attn_g, w_in, lam_q1,
             lam_k1, lam_q2, lam_k2, da_subln_g, w_out, norm_ffn_g, w_router, b_router, w_up, b_up,
             w_down, b_down, norm_final_g, *, tm, tq, pages_per_step, gather, scatter):
    batch, seq, _ = x_prompt.shape
    nb = x_sample.shape[0]
    n_pool, page = cache_k.shape[1], cache_k.shape[2]
    past_len = page_table.shape[1] * page
    n_prompt = batch * seq

    w_ext = _extend_w_in(w_in[0])
    w_ext_b = w_ext.astype(BF16)
    w_ext_t = jnp.concatenate([(w_ext[:, :_C_KA] * LOG2_E).astype(BF16), w_ext_b[:, _C_KA:_C_QR]],
                              axis=1).T
    g_attn = norm_attn_g[0].reshape(1, D_MODEL)
    g_ffn = norm_ffn_g[0].reshape(1, D_MODEL)
    g_fin = norm_final_g.reshape(1, D_MODEL)
    lamv = jnp.stack([lam_q1[0], lam_k1[0], lam_q2[0], lam_k2[0]]).astype(F32)
    subg = da_subln_g[0].reshape(1, DA_VD)
    w_o_b = w_out[0].astype(BF16)
    w_r = jnp.pad(w_router[0], ((0, 0), (0, LANES - N_EXPERTS))).astype(BF16)
    b_r = jnp.pad(b_router[0], (0, LANES - N_EXPERTS), constant_values=NEG_INF).reshape(1, LANES)
    assert cache_k.shape[0] == 1 and w_up.shape[0] == 1
    w_up_e = w_up.reshape(N_EXPERTS, D_MODEL, 2 * D_FF)
    b_up_d = jnp.concatenate([b_up[0][:, 0::2], b_up[0][:, 1::2]], axis=-1).reshape(N_EXPERTS, 1, 2 * D_FF)
    w_down_e = w_down.reshape(N_EXPERTS, D_FF, D_MODEL)
    b_dn = b_down.reshape(N_EXPERTS, 1, D_MODEL)
    cos_p, sin_p = _rotary_tables(jnp.arange(seq))
    cos_s, sin_s = _rotary_tables(past_len + jnp.zeros((nb,), I32))

    xp = x_prompt.reshape(n_prompt, D_MODEL)
    qat, kat, kab, va, vat, qr, kr, vr, gr = _inproj(xp, g_attn, w_ext_b, w_ext_t, cos_p, sin_p,
                                                     tm=tm, seq=seq)
    a_p = _attn_prompt(qat, kab, vat, lamv, subg, batch=batch, seq=seq, tq=tq)
    r_p, ret_p = _ret_prompt(qr, kr, vr, gr, batch=batch, seq=seq)
    x1_p, hfp_p, ti_p, tg_p = _outproj(a_p, r_p, xp, w_o_b, g_ffn, w_r, b_r, tm=tm)

    xs_ = x_sample.reshape(nb, D_MODEL)
    qa_s, ka_s, va_s, qr_s, kr_s, vr_s, gr_s = _inproj(xs_, g_attn, w_ext_b, w_ext_t, cos_s, sin_s,
                                                       tm=nb)
    ckt = jnp.transpose(cache_k.reshape(n_pool, page, DA_WIDTH), (0, 2, 1))
    cvr = cache_v.reshape(n_pool, page * DA_HEADS, DA_VD)
    a_s = _attn_sample(qa_s, ka_s, va_s, ckt, cvr, page_table, lamv, subg,
                       pages_per_step=pages_per_step)
    r_s, ret_s = _ret_sample(qr_s, kr_s, vr_s, gr_s,
                             state_ret.reshape(nb, RET_HEADS, RET_DK, RET_DV))
    x1_s, hfp_s, ti_s, tg_s = _outproj(a_s, r_s, xs_, w_o_b, g_ffn, w_r, b_r, tm=nb)

    n_tok = n_prompt + nb
    tok_pad = -(-n_tok // ROUTE_TILE) * ROUTE_TILE
    pad = tok_pad - n_tok
    hfp = jnp.concatenate([hfp_p, hfp_s, jnp.zeros((2, pad, QUARTER), I32)], axis=1)
    ti_all = jnp.concatenate([ti_p, ti_s, jnp.full((pad, LANES), LANES - 1, I32)], axis=0)
    rank, cnt = _rank(ti_all, tr=ROUTE_TILE)
    counts = cnt[0, :N_EXPERTS].astype(I32)
    n_blocks = -(-n_tok * TOP_K // MOE_TM) + N_EXPERTS
    n_slots = n_blocks * MOE_TM
    dest, block_e, n_live, n_active = _slot_layout(counts, ti_all[:n_tok, :TOP_K],
                                                   rank[:n_tok, :TOP_K], n_blocks)
    slots_pad = -(-(n_slots + 1) // (SC_UNIT // 2)) * (SC_UNIT // 2)
    dump = n_slots + _filler(pad, slots_pad - n_slots)
    dest_sc = jnp.concatenate([dest, jnp.broadcast_to(dump[:, None], (pad, TOP_K))], axis=0).T
    xs_sorted = scatter(hfp.reshape(2 * tok_pad, QUARTER),
                        jnp.concatenate([dest_sc, slots_pad + dest_sc]).reshape(-1), 2 * slots_pad)
    out_sorted = _moe(block_e, n_active, n_live, xs_sorted.reshape(2, slots_pad, QUARTER),
                      w_up_e, b_up_d, w_down_e, b_dn)
    fill = jnp.broadcast_to(_filler(pad, n_slots)[:, None], (pad, TOP_K))
    dest_g = jnp.concatenate([dest, fill], axis=0).T
    yg = gather(out_sorted.reshape(2 * n_slots, QUARTER),
                jnp.concatenate([dest_g, n_slots + dest_g]).reshape(-1))
    yg = yg.reshape(2, TOP_K, tok_pad, QUARTER)
    y_p = _combine(yg, tg_p, x1_p, g_fin, tm=tm, row_offset=0)
    y_s = _combine(yg, tg_s, x1_s, g_fin, tm=nb, row_offset=n_prompt)

    depth = 1
    return (y_p.reshape(batch, seq, D_MODEL),
            y_s.reshape(nb, 1, D_MODEL),
            jnp.transpose(kat.reshape(depth, batch, DA_HEADS, 2, DA_HD, seq), (0, 1, 5, 2, 3, 4)),
            va.reshape(depth, batch, seq, DA_HEADS, DA_VD),
            ret_p.reshape(depth, batch, RET_HEADS, RET_DK, RET_DV),
            ka_s.reshape(depth, nb, 1, DA_HEADS, 2, DA_HD),
            va_s.reshape(depth, nb, 1, DA_HEADS, DA_VD),
            ret_s.reshape(depth, nb, RET_HEADS, RET_DK, RET_DV))
```
